```python
import jax, jax.numpy as jnp
from jax import lax
import numpy as np

D_MODEL = 1024
BATCH = 2
SEQ = 16384
DEPTH = 1
DEC_BATCH = 2
DEC_SEQ = 8192
PAST_LEN = 128

HEAD_DIM = 64
N_FOURIER_GROUPS = 8
FOURIER_WIDTH = N_FOURIER_GROUPS * HEAD_DIM
N_ATTN_HEADS = 8
ATTN_WIDTH = N_ATTN_HEADS * HEAD_DIM
MIX_WIDTH = FOURIER_WIDTH + ATTN_WIDTH
IN_WIDTH = FOURIER_WIDTH + 3 * ATTN_WIDTH
DILATED_GROUPS = ((128, 1), (512, 4), (2048, 16))
MAX_HALF = max(w // 2 for w, _ in DILATED_GROUPS)
Q_BLOCK = 128
ROPE_THETA = 500000.0
ROPE_DIM = HEAD_DIM // 4
PEER_HEADS = 8
N_KEYS = 128
N_EXPERTS = N_KEYS * N_KEYS
PEER_TOPK = 16
HALF_Q = 128
QUERY_DIM = 2 * HALF_Q
TOKEN_BLOCK = 128
EPS = 1e-6

kernel_name = "hymba_fnet_longnet_peer_encoder"

F32 = jnp.float32


def rmsnorm(x, g):
    xf = x.astype(F32)
    y = xf * lax.rsqrt(jnp.mean(xf * xf, axis=-1, keepdims=True) + EPS)
    return (y * g.astype(F32)).astype(x.dtype)


def partial_rope(x):
    S = x.shape[1]
    half = ROPE_DIM // 2
    inv = ROPE_THETA ** (-(jnp.arange(half, dtype=F32) * 2.0) / ROPE_DIM)
    ang = jnp.arange(S, dtype=F32)[:, None] * inv[None, :]
    cos = jnp.cos(ang)[None, :, None, :]
    sin = jnp.sin(ang)[None, :, None, :]
    xf = x.astype(F32)
    x1 = xf[..., :half]
    x2 = xf[..., half:ROPE_DIM]
    out = jnp.concatenate([x1 * cos - x2 * sin, x1 * sin + x2 * cos, xf[..., ROPE_DIM:]], axis=-1)
    return out.astype(x.dtype)


def fourier_mix(u, w_fourier):
    B, S, _ = u.shape
    ug = u.reshape(B, S, N_FOURIER_GROUPS, HEAD_DIM).astype(F32)
    f = jnp.fft.fft2(ug, axes=(1, 3), norm="ortho").real
    y = jnp.einsum('bsgc,gce->bsge', f, w_fourier.astype(F32))
    return y.reshape(B, S, FOURIER_WIDTH).astype(u.dtype)


def dilated_attention(q, k, v):
    B, S, H, Dh = q.shape
    scale = 1.0 / np.sqrt(Dh)
    pad = ((0, 0), (MAX_HALF, MAX_HALF), (0, 0), (0, 0))
    kp = jnp.pad(k, pad)
    vp = jnp.pad(v, pad)
    n_blk = S // Q_BLOCK
    qi = jnp.arange(Q_BLOCK, dtype=jnp.int32)
    offsets = [jnp.asarray(np.arange(-(w // 2), w // 2 + 1, d), dtype=jnp.int32)
               for (w, d) in DILATED_GROUPS]

    def block(b):
        q0 = b * Q_BLOCK
        qb = lax.dynamic_slice_in_dim(q, q0, Q_BLOCK, axis=1).astype(F32) * scale
        kb = lax.dynamic_slice_in_dim(kp, q0, Q_BLOCK + 2 * MAX_HALF, axis=1)
        vb = lax.dynamic_slice_in_dim(vp, q0, Q_BLOCK + 2 * MAX_HALF, axis=1)
        outs, lses = [], []
        for off in offsets:
            idx = qi[:, None] + MAX_HALF + off[None, :]
            pos = q0 + qi[:, None] + off[None, :]
            valid = (pos >= 0) & (pos < S)
            kg = kb[:, idx].astype(F32)
            vg = vb[:, idx].astype(F32)
            s = jnp.einsum('bqhd,bqjhd->bqhj', qb, kg)
            s = jnp.where(valid[None, :, None, :], s, -1e30)
            m = jnp.max(s, axis=-1, keepdims=True)
            p = jnp.exp(s - m)
            l = jnp.sum(p, axis=-1)
            o = jnp.einsum('bqhj,bqjhd->bqhd', p, vg) / l[..., None]
            outs.append(o)
            lses.append(m[..., 0] + jnp.log(l))
        alpha = jax.nn.softmax(jnp.stack(lses, axis=0), axis=0)
        o = jnp.sum(alpha[..., None] * jnp.stack(outs, axis=0), axis=0)
        return o.astype(q.dtype)

    out = lax.map(block, jnp.arange(n_blk, dtype=jnp.int32))
    return out.transpose(1, 0, 2, 3, 4).reshape(B, S, H * Dh)


def peer(h, w_query, sub_keys, expert_u, expert_v):
    B, S, D = h.shape
    T = B * S
    tb = h.reshape(T // TOKEN_BLOCK, TOKEN_BLOCK, D)
    K = PEER_TOPK

    def block(xb):
        q = (xb @ w_query).astype(F32).reshape(TOKEN_BLOCK, PEER_HEADS, 2, HALF_Q)
        s = jnp.einsum('thpc,hpkc->thpk', q, sub_keys.astype(F32))
        v_half, i_half = lax.top_k(s, K)
        cand = v_half[:, :, 0, :, None] + v_half[:, :, 1, None, :]
        top_s, flat = lax.top_k(cand.reshape(TOKEN_BLOCK, PEER_HEADS, K * K), K)
        i1 = jnp.take_along_axis(i_half[:, :, 0], flat // K, axis=-1)
        i2 = jnp.take_along_axis(i_half[:, :, 1], flat % K, axis=-1)
        e = (i1 * N_KEYS + i2).reshape(TOKEN_BLOCK, PEER_HEADS * K)
        g = jax.nn.softmax(top_s, axis=-1).reshape(TOKEN_BLOCK, PEER_HEADS * K)
        u_sel = expert_u[e].astype(F32)
        v_sel = expert_v[e].astype(F32)
        a = jax.nn.gelu(jnp.einsum('td,tkd->tk', xb.astype(F32), u_sel), approximate=False)
        out = jnp.einsum('tk,tkd->td', g * a, v_sel)
        return out.astype(xb.dtype)

    out = lax.map(block, tb)
    return out.reshape(B, S, D)


def encoder(x, norm1_g, w_in, w_fourier, w_out, norm2_g, w_query, sub_keys,
            expert_u, expert_v, final_g):
    B, S, _ = x.shape
    for l in range(DEPTH):
        h = rmsnorm(x, norm1_g[l])
        p = h @ w_in[l]
        f_in = p[..., :FOURIER_WIDTH]
        q = p[..., FOURIER_WIDTH:FOURIER_WIDTH + ATTN_WIDTH].reshape(B, S, N_ATTN_HEADS, HEAD_DIM)
        k = p[..., FOURIER_WIDTH + ATTN_WIDTH:FOURIER_WIDTH + 2 * ATTN_WIDTH].reshape(B, S, N_ATTN_HEADS, HEAD_DIM)
        v = p[..., FOURIER_WIDTH + 2 * ATTN_WIDTH:].reshape(B, S, N_ATTN_HEADS, HEAD_DIM)
        q = partial_rope(q)
        k = partial_rope(k)
        f_out = fourier_mix(f_in, w_fourier[l])
        a_out = dilated_attention(q, k, v)
        x = x + jnp.concatenate([f_out, a_out], axis=-1) @ w_out[l]
        x = x + peer(rmsnorm(x, norm2_g[l]), w_query[l], sub_keys[l], expert_u[l], expert_v[l])
    return rmsnorm(x, final_g)


def setup_inputs(seed: int = 0) -> dict:
    key = jax.random.key(seed)
    ks = jax.random.split(key, 12)
    nrm = jax.random.normal
    return {
        "x_prompt": nrm(ks[0], (BATCH, SEQ, D_MODEL), F32),
        "x_sample": nrm(ks[1], (DEC_BATCH, DEC_SEQ, D_MODEL), F32),
        "norm1_g": 1.0 + 0.01 * nrm(ks[2], (DEPTH, D_MODEL), F32),
        "w_in": nrm(ks[3], (DEPTH, D_MODEL, IN_WIDTH), F32) * D_MODEL ** -0.5,
        "w_fourier": nrm(ks[4], (DEPTH, N_FOURIER_GROUPS, HEAD_DIM, HEAD_DIM), F32) * HEAD_DIM ** -0.5,
        "w_out": nrm(ks[5], (DEPTH, MIX_WIDTH, D_MODEL), F32) * MIX_WIDTH ** -0.5,
        "norm2_g": 1.0 + 0.01 * nrm(ks[6], (DEPTH, D_MODEL), F32),
        "w_query": nrm(ks[7], (DEPTH, D_MODEL, PEER_HEADS * QUERY_DIM), F32) * D_MODEL ** -0.5,
        "sub_keys": nrm(ks[8], (DEPTH, PEER_HEADS, 2, N_KEYS, HALF_Q), F32) * HALF_Q ** -0.5,
        "expert_u": nrm(ks[9], (DEPTH, N_EXPERTS, D_MODEL), F32) * D_MODEL ** -0.5,
        "expert_v": nrm(ks[10], (DEPTH, N_EXPERTS, D_MODEL), F32) * 0.3,
        "final_g": 1.0 + 0.01 * nrm(ks[11], (D_MODEL,), F32),
    }


def reference(x_prompt, x_sample, norm1_g, w_in, w_fourier, w_out, norm2_g, w_query,
              sub_keys, expert_u, expert_v, final_g):
    y_prompt = encoder(x_prompt, norm1_g, w_in, w_fourier, w_out, norm2_g, w_query,
                       sub_keys, expert_u, expert_v, final_g)
    y_sample = encoder(x_sample, norm1_g, w_in, w_fourier, w_out, norm2_g, w_query,
                       sub_keys, expert_u, expert_v, final_g)
    return (y_prompt, y_sample)
```

```python
import functools

import numpy as np
import jax
import jax.numpy as jnp
from jax import lax
from jax.experimental import pallas as pl
from jax.experimental.pallas import tpu as pltpu

F32 = jnp.float32
BF16 = jnp.bfloat16

D_MODEL = 1024
HEAD_DIM = 64
FOURIER_WIDTH = 512
ATTN_WIDTH = 512
IN_WIDTH = 2048
N_GROUPS = 8
DILATIONS = (1, 4, 16)
HALF_WINDOW = 64
ROPE_THETA = 500000.0
ROPE_DIM = 16
PEER_HEADS = 8
N_KEYS = 128
N_EXPERTS = N_KEYS * N_KEYS
TOPK = 16
EPS = 1e-6
NEG = -1e30

LANES = 128
DFT_INNER = 128
G_PITCH = 136
VMEM_LIMIT = 56 * 2**20


def _params(sem, vmem=None):
    return pltpu.CompilerParams(dimension_semantics=sem, vmem_limit_bytes=vmem or VMEM_LIMIT)


def _dot(a, b):
    return jnp.dot(a, b, preferred_element_type=F32)


def _dot_nt(a, b):
    return lax.dot_general(a, b, (((1,), (1,)), ((), ())), preferred_element_type=F32)


def _proj_body(x_ref, g_ref, w_ref, c_ref, sa_ref, sb_ref, f_ref, q_ref, k_ref, v_ref):
    x = x_ref[...]
    ms = jnp.mean(x * x, axis=-1, keepdims=True)
    h = (x * lax.rsqrt(ms + EPS) * g_ref[...]).astype(BF16)
    p = _dot(h, w_ref[...])
    f_ref[...] = p[:, :FOURIER_WIDTH].astype(BF16)
    c, sa, sb = c_ref[...], sa_ref[...], sb_ref[...]
    for j in range(ATTN_WIDTH // LANES):
        lo = j * LANES
        qc = p[:, FOURIER_WIDTH + lo:FOURIER_WIDTH + lo + LANES]
        kc = p[:, FOURIER_WIDTH + ATTN_WIDTH + lo:FOURIER_WIDTH + ATTN_WIDTH + lo + LANES]
        qr = qc * c + pltpu.roll(qc, LANES - 8, 1) * sa + pltpu.roll(qc, 8, 1) * sb
        kr = kc * c + pltpu.roll(kc, LANES - 8, 1) * sa + pltpu.roll(kc, 8, 1) * sb
        q_ref[:, lo:lo + LANES] = (qr * (HEAD_DIM ** -0.5)).astype(BF16)
        k_ref[:, lo:lo + LANES] = kr.astype(BF16)
    v_ref[...] = p[:, FOURIER_WIDTH + 2 * ATTN_WIDTH:].astype(BF16)


def _rope_tables(S):
    half = ROPE_DIM // 2
    inv = ROPE_THETA ** (-(jnp.arange(half, dtype=F32) * 2.0) / ROPE_DIM)
    ang = jnp.arange(S, dtype=F32)[:, None] * inv[None, :]
    cos, sin = jnp.cos(ang), jnp.sin(ang)
    one = jnp.ones((S, HEAD_DIM - ROPE_DIM), F32)
    zero = jnp.zeros((S, HEAD_DIM - ROPE_DIM), F32)
    z8 = jnp.zeros((S, half), F32)
    c = jnp.concatenate([cos, cos, one], axis=1)
    sa = jnp.concatenate([-sin, z8, zero], axis=1)
    sb = jnp.concatenate([z8, sin, zero], axis=1)
    rep = LANES // HEAD_DIM
    return jnp.tile(c, (1, rep)), jnp.tile(sa, (1, rep)), jnp.tile(sb, (1, rep))


def _proj(x, g, w_bf):
    B, S, _ = x.shape
    tm = min(512, S)
    c, sa, sb = _rope_tables(S)
    tab = pl.BlockSpec((tm, LANES), lambda b, i: (i, 0))
    out = pl.BlockSpec((None, tm, ATTN_WIDTH), lambda b, i: (b, i, 0))
    shp = jax.ShapeDtypeStruct((B, S, ATTN_WIDTH), BF16)
    return pl.pallas_call(
        _proj_body,
        grid=(B, S // tm),
        in_specs=[pl.BlockSpec((None, tm, D_MODEL), lambda b, i: (b, i, 0)),
                  pl.BlockSpec((1, D_MODEL), lambda b, i: (0, 0)),
                  pl.BlockSpec((D_MODEL, IN_WIDTH), lambda b, i: (0, 0)),
                  tab, tab, tab],
        out_specs=[out, out, out, out],
        out_shape=[shp, shp, shp, shp],
        compiler_params=_params(("parallel", "parallel")),
        name="proj",
    )(x, g.reshape(1, D_MODEL), w_bf, c, sa, sb)


def _cos_sin(n_rows, n_cols, period):
    i = jnp.arange(n_rows, dtype=jnp.int32)[:, None]
    j = jnp.arange(n_cols, dtype=jnp.int32)[None, :]
    ang = ((i * j) % period).astype(F32) * (2.0 * np.pi / period)
    return jnp.cos(ang), jnp.sin(ang)


def _dft_a_body(n1, d_ref, x_ref, yr_ref, yi_ref):
    y = _dot(d_ref[...], x_ref[...])
    yr_ref[...] = y[:n1].astype(BF16)
    yi_ref[...] = y[n1:].astype(BF16)


def _dft_b_body(ct, scale, yr_ref, yi_ref, tc_ref, ts_ref, d2_ref, c64_ref, s64_ref, wf_ref, o_ref):
    rep = FOURIER_WIDTH // LANES
    for cc in range(ct):
        yr = yr_ref[cc].astype(F32)
        yi = yi_ref[cc].astype(F32)
        tc = jnp.concatenate([tc_ref[cc]] * rep, axis=1)
        ts = jnp.concatenate([ts_ref[cc]] * rep, axis=1)
        zr = yr * tc + yi * ts
        zi = yi * tc - yr * ts
        z = jnp.concatenate([zr, zi], axis=0).astype(BF16)
        xx = _dot(d2_ref[...], z)
        xr = xx[:DFT_INNER].astype(BF16)
        xi = xx[DFT_INNER:].astype(BF16)
        re = (_dot(xr, c64_ref[...]) + _dot(xi, s64_ref[...])) * scale
        o_ref[cc] = _dot(re.astype(BF16), wf_ref[...]).astype(BF16)


def _block_diag(m):
    G, n, _ = m.shape
    eye = jnp.eye(G, dtype=m.dtype)
    return (eye[:, None, :, None] * m[:, :, None, :]).reshape(G * n, G * n)


def _fourier(f_in, w_fourier):
    B, S, W = f_in.shape
    n2 = DFT_INNER
    n1 = S // n2
    cols = n2 * W
    tcw = 4096
    c1, s1 = _cos_sin(n1, n1, n1)
    d1 = jnp.concatenate([c1, -s1], axis=0).astype(BF16)
    xa = f_in.reshape(B, n1, cols)
    blk = pl.BlockSpec((None, n1, tcw), lambda b, j: (b, 0, j))
    yshape = jax.ShapeDtypeStruct((B, n1, cols), BF16)
    yr, yi = pl.pallas_call(
        functools.partial(_dft_a_body, n1),
        grid=(B, cols // tcw),
        in_specs=[pl.BlockSpec((2 * n1, n1), lambda b, j: (0, 0)), blk],
        out_specs=[blk, blk],
        out_shape=[yshape, yshape],
        compiler_params=_params(("parallel", "parallel")),
        name="dft_a",
    )(d1, xa)

    yr = yr.reshape(B, n1, n2, W)
    yi = yi.reshape(B, n1, n2, W)
    twc, tws = _cos_sin(n1, n2, S)
    twc = jnp.broadcast_to(twc[:, :, None], (n1, n2, LANES))
    tws = jnp.broadcast_to(tws[:, :, None], (n1, n2, LANES))
    c2, s2 = _cos_sin(n2, n2, n2)
    d2 = jnp.concatenate([jnp.concatenate([c2, s2], axis=1),
                          jnp.concatenate([-s2, c2], axis=1)], axis=0).astype(BF16)
    cg, sg = _cos_sin(HEAD_DIM, HEAD_DIM, HEAD_DIM)
    c64 = _block_diag(jnp.broadcast_to(cg, (N_GROUPS, HEAD_DIM, HEAD_DIM))).astype(BF16)
    s64 = _block_diag(jnp.broadcast_to(sg, (N_GROUPS, HEAD_DIM, HEAD_DIM))).astype(BF16)
    wf = _block_diag(w_fourier).astype(BF16)
    ct = 8
    scale = float(1.0 / np.sqrt(S * HEAD_DIM))
    yblk = pl.BlockSpec((None, ct, n2, W), lambda b, i: (b, i, 0, 0))
    tblk = pl.BlockSpec((ct, n2, LANES), lambda b, i: (i, 0, 0))
    full = lambda r, c: pl.BlockSpec((r, c), lambda b, i: (0, 0))
    out = pl.pallas_call(
        functools.partial(_dft_b_body, ct, scale),
        grid=(B, n1 // ct),
        in_specs=[yblk, yblk, tblk, tblk, full(2 * n2, 2 * n2), full(W, W), full(W, W), full(W, W)],
        out_specs=yblk,
        out_shape=jax.ShapeDtypeStruct((B, n1, n2, W), BF16),
        compiler_params=_params(("parallel", "parallel")),
        name="dft_b",
    )(yr, yi, twc, tws, d2, c64, s64, wf)
    return out.transpose(0, 2, 1, 3).reshape(B, S, W)


def _attn_body(tq, L, q_ref, kp_ref, kc_ref, kn_ref, vp_ref, vc_ref, vn_ref, o_ref, lse_ref):
    i = pl.program_id(2)
    sub = 128
    win = sub + 2 * HALF_WINDOW
    lane = lax.broadcasted_iota(jnp.int32, (sub, LANES), 1)
    lo_half = lane < HEAD_DIM
    r = lax.broadcasted_iota(jnp.int32, (sub, win), 0)
    c = lax.broadcasted_iota(jnp.int32, (sub, win), 1)
    band = (c >= r) & (c <= r + 2 * HALF_WINDOW)
    for pair in range(ATTN_WIDTH // LANES):
        sl = slice(pair * LANES, (pair + 1) * LANES)
        k2 = jnp.concatenate([kp_ref[:, sl], kc_ref[:, sl], kn_ref[:, sl]], axis=0)
        v2 = jnp.concatenate([vp_ref[:, sl], vc_ref[:, sl], vn_ref[:, sl]], axis=0)
        for sb in range(tq // sub):
            rows = slice(sb * sub, (sb + 1) * sub)
            qf = q_ref[rows, sl].astype(F32)
            ks = k2[sb * sub:sb * sub + win]
            vs = v2[sb * sub:sb * sub + win]
            kpos = i * tq + sb * sub - HALF_WINDOW + c
            valid = band & (kpos >= 0) & (kpos < L)
            outs, lses = [], []
            for hh in range(2):
                mine = lo_half if hh == 0 else jnp.logical_not(lo_half)
                qm = jnp.where(mine, qf, 0.0).astype(BF16)
                s = jnp.where(valid, _dot_nt(qm, ks), NEG)
                m = jnp.max(s, axis=1, keepdims=True)
                p = jnp.exp(s - m)
                l = jnp.sum(p, axis=1, keepdims=True)
                o = _dot(p.astype(BF16), vs) * (1.0 / l)
                outs.append(o)
                lses.append(m + jnp.log(l))
            o_ref[rows, sl] = jnp.where(lo_half, outs[0], outs[1])
            lse_ref[rows, sl] = jnp.where(lo_half, lses[0], lses[1])


def _attention_group(q, k, v, R):
    B, S, W = q.shape
    L = S // R
    tq = min(256, L)
    hb = tq // HALF_WINDOW
    nhalo = L // HALF_WINDOW
    view = lambda a: a.reshape(B, L, R * W)
    ctr = pl.BlockSpec((None, tq, W), lambda b, r, i: (b, i, r))
    prev = pl.BlockSpec((None, HALF_WINDOW, W), lambda b, r, i: (b, jnp.maximum(i * hb - 1, 0), r))
    nxt = pl.BlockSpec((None, HALF_WINDOW, W), lambda b, r, i: (b, jnp.minimum((i + 1) * hb, nhalo - 1), r))
    shp = jax.ShapeDtypeStruct((B, L, R * W), F32)
    o, lse = pl.pallas_call(
        functools.partial(_attn_body, tq, L),
        grid=(B, R, L // tq),
        in_specs=[ctr, prev, ctr, nxt, prev, ctr, nxt],
        out_specs=[ctr, ctr],
        out_shape=[shp, shp],
        compiler_params=_params(("parallel", "parallel", "parallel")),
        name=f"attn_d{R}",
    )(view(q), view(k), view(k), view(k), view(v), view(v), view(v))
    return o.reshape(B, S, W), lse.reshape(B, S, W)


def _mix_body(x_ref, f_ref, o1_ref, o2_ref, o3_ref, l1_ref, l2_ref, l3_ref, wf_ref, wa_ref, g_ref, wq_ref,
              x1_ref, xn_ref, qp_ref):
    l1, l2, l3 = l1_ref[...], l2_ref[...], l3_ref[...]
    mx = jnp.maximum(jnp.maximum(l1, l2), l3)
    e1, e2, e3 = jnp.exp(l1 - mx), jnp.exp(l2 - mx), jnp.exp(l3 - mx)
    a = (e1 * o1_ref[...] + e2 * o2_ref[...] + e3 * o3_ref[...]) * (1.0 / (e1 + e2 + e3))
    x1 = x_ref[...] + _dot(f_ref[...], wf_ref[...]) + _dot(a.astype(BF16), wa_ref[...])
    x1_ref[...] = x1
    ms = jnp.mean(x1 * x1, axis=-1, keepdims=True)
    xn = (x1 * lax.rsqrt(ms + EPS) * g_ref[...]).astype(BF16)
    xn_ref[...] = xn
    qp_ref[...] = _dot(xn, wq_ref[...])


def _mix(x, f_out, os_, lses, w_out_bf, g2, wq_bf):
    N = x.shape[0]
    tm = 256
    row = lambda w: pl.BlockSpec((tm, w), lambda i: (i, 0))
    full = lambda r, c: pl.BlockSpec((r, c), lambda i: (0, 0))
    qw = wq_bf.shape[1]
    return pl.pallas_call(
        _mix_body,
        grid=(N // tm,),
        in_specs=[row(D_MODEL), row(FOURIER_WIDTH)] + [row(ATTN_WIDTH)] * 6
                 + [full(FOURIER_WIDTH, D_MODEL), full(ATTN_WIDTH, D_MODEL), full(1, D_MODEL), full(D_MODEL, qw)],
        out_specs=[row(D_MODEL), row(D_MODEL), row(qw)],
        out_shape=[jax.ShapeDtypeStruct((N, D_MODEL), F32), jax.ShapeDtypeStruct((N, D_MODEL), BF16),
                   jax.ShapeDtypeStruct((N, qw), F32)],
        compiler_params=_params(("parallel",)),
        name="mix",
    )(x, f_out, *os_, *lses, w_out_bf[:FOURIER_WIDTH], w_out_bf[FOURIER_WIDTH:], g2.reshape(1, D_MODEL), wq_bf)


def _extract_topk(s, n_rows):
    tq = s.shape[1]
    idx_all = lax.broadcasted_iota(jnp.int32, (n_rows, tq), 0).astype(F32)
    rank = lax.broadcasted_iota(jnp.int32, (TOPK, tq), 0)
    vals = jnp.zeros((TOPK, tq), F32)
    idxs = jnp.zeros((TOPK, tq), F32)
    for kk in range(TOPK):
        m = jnp.max(s, axis=0, keepdims=True)
        first = jnp.min(jnp.where(s == m, idx_all, float(n_rows)), axis=0, keepdims=True)
        s = jnp.where(idx_all == first, -jnp.inf, s)
        vals = jnp.where(rank == kk, m, vals)
        idxs = jnp.where(rank == kk, first, idxs)
    return vals, idxs


def _select_rows(sel, table):
    out = jnp.zeros_like(sel)
    for a in range(TOPK):
        out = jnp.where(sel == float(a), table[a:a + 1, :], out)
    return out


def _topk_body(qp_ref, keys_ref, i1_ref, i2_ref, g_ref):
    i1s, i2s, gs = [], [], []
    for h in range(PEER_HEADS):
        halves = []
        for p in range(2):
            j = h * 2 + p
            qc = qp_ref[:, j * N_KEYS:(j + 1) * N_KEYS].astype(BF16)
            st = _dot_nt(keys_ref[j], qc)
            halves.append(_extract_topk(st, N_KEYS))
        (v1, x1), (v2, x2) = halves
        cand = jnp.concatenate([v1[a:a + 1, :] + v2 for a in range(TOPK)], axis=0)
        top_s, flat = _extract_topk(cand, TOPK * TOPK)
        a_sel = jnp.floor(flat * (1.0 / TOPK))
        b_sel = flat - a_sel * TOPK
        i1s.append(_select_rows(a_sel, x1))
        i2s.append(_select_rows(b_sel, x2))
        e = jnp.exp(top_s - top_s[0:1, :])
        gs.append(e * (1.0 / jnp.sum(e, axis=0, keepdims=True)))
    i1_ref[...] = jnp.concatenate(i1s, axis=0).T.astype(jnp.int32)
    i2_ref[...] = jnp.concatenate(i2s, axis=0).T.astype(jnp.int32)
    g_ref[...] = jnp.concatenate(gs, axis=0).T


def _peer_topk(qp, keys_bf):
    N, qw = qp.shape
    tq = 256
    nsel = PEER_HEADS * TOPK
    out = pl.BlockSpec((tq, nsel), lambda i: (i, 0))
    return pl.pallas_call(
        _topk_body,
        grid=(N // tq,),
        in_specs=[pl.BlockSpec((tq, qw), lambda i: (i, 0)),
                  pl.BlockSpec((2 * PEER_HEADS, N_KEYS, N_KEYS), lambda i: (0, 0, 0))],
        out_specs=[out, out, out],
        out_shape=[jax.ShapeDtypeStruct((N, nsel), jnp.int32), jax.ShapeDtypeStruct((N, nsel), jnp.int32),
                   jax.ShapeDtypeStruct((N, nsel), F32)],
        compiler_params=_params(("parallel",)),
        name="peer_topk",
    )(qp, keys_bf)


def _peer_body(tq, ni, x1_ref, xn_ref, i1_ref, i2_ref, g_ref, u_ref, v_ref, gf_ref, o_ref, gate_ref, w_ref):
    c = pl.program_id(1)

    @pl.when(c == 0)
    def _():
        o_ref[...] = jnp.zeros_like(o_ref)
        row = lax.broadcasted_iota(jnp.int32, (N_KEYS, N_KEYS), 0)

        def build(t, carry):
            p1 = jnp.where(i1_ref[pl.ds(t, 1), :] == row, g_ref[pl.ds(t, 1), :], 0.0).astype(BF16)
            p2 = jnp.where(i2_ref[pl.ds(t, 1), :] == row, 1.0, 0.0).astype(BF16)
            gate_ref[pl.ds(pl.multiple_of(t * G_PITCH, 8), N_KEYS), :] = _dot_nt(p1, p2)
            return carry

        lax.fori_loop(0, tq, build, 0)

    a = _dot_nt(xn_ref[...], u_ref[...])
    act = 0.5 * a * (1.0 + lax.erf(a * float(np.sqrt(0.5))))
    for ii in range(ni):
        gsl = gate_ref[pl.ds(c * ni + ii, tq, stride=G_PITCH), :]
        w_ref[:, ii * N_KEYS:(ii + 1) * N_KEYS] = (act[:, ii * N_KEYS:(ii + 1) * N_KEYS] * gsl).astype(BF16)
    o_ref[...] += _dot(w_ref[...], v_ref[...])

    @pl.when(c == pl.num_programs(1) - 1)
    def _():
        y = x1_ref[...] + o_ref[...]
        ms = jnp.mean(y * y, axis=-1, keepdims=True)
        o_ref[...] = y * lax.rsqrt(ms + EPS) * gf_ref[...]


def _peer(x1, xn, i1, i2, g, u_bf, v_bf, gf):
    N = x1.shape[0]
    tq = 256
    ni = 16
    ec = ni * N_KEYS
    nsel = PEER_HEADS * TOPK
    tok = lambda w: pl.BlockSpec((tq, w), lambda i, c: (i, 0))
    tab = pl.BlockSpec((ec, D_MODEL), lambda i, c: (c, 0))
    return pl.pallas_call(
        functools.partial(_peer_body, tq, ni),
        grid=(N // tq, N_EXPERTS // ec),
        in_specs=[tok(D_MODEL), tok(D_MODEL), tok(nsel), tok(nsel), tok(nsel), tab, tab,
                  pl.BlockSpec((1, D_MODEL), lambda i, c: (0, 0))],
        out_specs=tok(D_MODEL),
        out_shape=jax.ShapeDtypeStruct((N, D_MODEL), F32),
        scratch_shapes=[pltpu.VMEM((tq * G_PITCH, N_KEYS), F32), pltpu.VMEM((tq, ec), BF16)],
        compiler_params=_params(("parallel", "arbitrary")),
        name="peer",
    )(x1, xn, i1, i2, g, u_bf, v_bf, gf.reshape(1, D_MODEL))


def _encoder(x, w):
    B, S, _ = x.shape
    f_in, q, k, v = _proj(x, w["norm1_g"], w["w_in"])
    f_out = _fourier(f_in, w["w_fourier"])
    groups = [_attention_group(q, k, v, R) for R in DILATIONS]
    N = B * S
    flat = lambda a: a.reshape(N, a.shape[-1])
    x1, xn, qp = _mix(flat(x), flat(f_out), [flat(o) for o, _ in groups], [flat(l) for _, l in groups],
                      w["w_out"], w["norm2_g"], w["w_query"])
    i1, i2, g = _peer_topk(qp, w["sub_keys"])
    y = _peer(x1, xn, i1, i2, g, w["expert_u"], w["expert_v"], w["final_g"])
    return y.reshape(B, S, D_MODEL)


def kernel(x_prompt, x_sample, norm1_g, w_in, w_fourier, w_out, norm2_g, w_query, sub_keys, expert_u, expert_v, final_g):
    w = {
        "norm1_g": norm1_g[0], "w_in": w_in[0].astype(BF16), "w_fourier": w_fourier[0],
        "w_out": w_out[0].astype(BF16), "norm2_g": norm2_g[0], "w_query": w_query[0].astype(BF16),
        "sub_keys": sub_keys[0].reshape(2 * PEER_HEADS, N_KEYS, N_KEYS).astype(BF16),
        "expert_u": expert_u[0].astype(BF16), "expert_v": expert_v[0].astype(BF16), "final_g": final_g,
    }
    return (_encoder(x_prompt, w), _encoder(x_sample, w))
```

```python
import functools

import numpy as np
import jax
import jax.numpy as jnp
from jax import lax
from jax.experimental import pallas as pl
from jax.experimental.pallas import tpu as pltpu

F32 = jnp.float32
BF16 = jnp.bfloat16

D_MODEL = 1024
HEAD_DIM = 64
FOURIER_WIDTH = 512
ATTN_WIDTH = 512
IN_WIDTH = 2048
N_GROUPS = 8
DILATIONS = (1, 4, 16)
HALF_WINDOW = 64
ROPE_THETA = 500000.0
ROPE_DIM = 16
PEER_HEADS = 8
N_KEYS = 128
N_EXPERTS = N_KEYS * N_KEYS
TOPK = 16
EPS = 1e-6
NEG = -1e30

LANES = 128
DFT_INNER = 128
G_PITCH = 72
ID_NONE = 1e9
VMEM_LIMIT = 56 * 2**20


def _params(sem, vmem=None):
    return pltpu.CompilerParams(dimension_semantics=sem, vmem_limit_bytes=vmem or VMEM_LIMIT)


def _dot(a, b):
    return jnp.dot(a, b, preferred_element_type=F32)


def _dot_nt(a, b):
    return lax.dot_general(a, b, (((1,), (1,)), ((), ())), preferred_element_type=F32)


def _proj_body(x_ref, g_ref, w_ref, c_ref, sa_ref, sb_ref, f_ref, q_ref, k_ref, v_ref):
    x = x_ref[...]
    ms = jnp.mean(x * x, axis=-1, keepdims=True)
    h = (x * lax.rsqrt(ms + EPS) * g_ref[...]).astype(BF16)
    p = _dot(h, w_ref[...])
    f_ref[...] = p[:, :FOURIER_WIDTH].astype(BF16)
    c, sa, sb = c_ref[...], sa_ref[...], sb_ref[...]
    for j in range(ATTN_WIDTH // LANES):
        lo = j * LANES
        qc = p[:, FOURIER_WIDTH + lo:FOURIER_WIDTH + lo + LANES]
        kc = p[:, FOURIER_WIDTH + ATTN_WIDTH + lo:FOURIER_WIDTH + ATTN_WIDTH + lo + LANES]
        qr = qc * c + pltpu.roll(qc, LANES - 8, 1) * sa + pltpu.roll(qc, 8, 1) * sb
        kr = kc * c + pltpu.roll(kc, LANES - 8, 1) * sa + pltpu.roll(kc, 8, 1) * sb
        q_ref[:, lo:lo + LANES] = (qr * (HEAD_DIM ** -0.5)).astype(BF16)
        k_ref[:, lo:lo + LANES] = kr.astype(BF16)
    v_ref[...] = p[:, FOURIER_WIDTH + 2 * ATTN_WIDTH:].astype(BF16)


def _rope_tables(S):
    half = ROPE_DIM // 2
    inv = ROPE_THETA ** (-(jnp.arange(half, dtype=F32) * 2.0) / ROPE_DIM)
    ang = jnp.arange(S, dtype=F32)[:, None] * inv[None, :]
    cos, sin = jnp.cos(ang), jnp.sin(ang)
    one = jnp.ones((S, HEAD_DIM - ROPE_DIM), F32)
    zero = jnp.zeros((S, HEAD_DIM - ROPE_DIM), F32)
    z8 = jnp.zeros((S, half), F32)
    c = jnp.concatenate([cos, cos, one], axis=1)
    sa = jnp.concatenate([-sin, z8, zero], axis=1)
    sb = jnp.concatenate([z8, sin, zero], axis=1)
    rep = LANES // HEAD_DIM
    return jnp.tile(c, (1, rep)), jnp.tile(sa, (1, rep)), jnp.tile(sb, (1, rep))


def _proj(x, g, w_bf):
    B, S, _ = x.shape
    tm = min(512, S)
    c, sa, sb = _rope_tables(S)
    tab = pl.BlockSpec((tm, LANES), lambda b, i: (i, 0))
    out = pl.BlockSpec((None, tm, ATTN_WIDTH), lambda b, i: (b, i, 0))
    shp = jax.ShapeDtypeStruct((B, S, ATTN_WIDTH), BF16)
    return pl.pallas_call(
        _proj_body,
        grid=(B, S // tm),
        in_specs=[pl.BlockSpec((None, tm, D_MODEL), lambda b, i: (b, i, 0)),
                  pl.BlockSpec((1, D_MODEL), lambda b, i: (0, 0)),
                  pl.BlockSpec((D_MODEL, IN_WIDTH), lambda b, i: (0, 0)),
                  tab, tab, tab],
        out_specs=[out, out, out, out],
        out_shape=[shp, shp, shp, shp],
        compiler_params=_params(("parallel", "parallel")),
        name="proj",
    )(x, g.reshape(1, D_MODEL), w_bf, c, sa, sb)


def _cos_sin(n_rows, n_cols, period):
    i = jnp.arange(n_rows, dtype=jnp.int32)[:, None]
    j = jnp.arange(n_cols, dtype=jnp.int32)[None, :]
    ang = ((i * j) % period).astype(F32) * (2.0 * np.pi / period)
    return jnp.cos(ang), jnp.sin(ang)


def _dft_a_body(n1, d_ref, x_ref, yr_ref, yi_ref):
    y = _dot(d_ref[...], x_ref[...])
    yr_ref[...] = y[:n1].astype(BF16)
    yi_ref[...] = y[n1:].astype(BF16)


def _dft_b_body(ct, scale, yr_ref, yi_ref, tc_ref, ts_ref, d2_ref, c64_ref, s64_ref, wf_ref, o_ref):
    rep = FOURIER_WIDTH // LANES
    for cc in range(ct):
        yr = yr_ref[cc].astype(F32)
        yi = yi_ref[cc].astype(F32)
        tc = jnp.concatenate([tc_ref[cc]] * rep, axis=1)
        ts = jnp.concatenate([ts_ref[cc]] * rep, axis=1)
        zr = yr * tc + yi * ts
        zi = yi * tc - yr * ts
        z = jnp.concatenate([zr, zi], axis=0).astype(BF16)
        xx = _dot(d2_ref[...], z)
        xr = xx[:DFT_INNER].astype(BF16)
        xi = xx[DFT_INNER:].astype(BF16)
        re = (_dot(xr, c64_ref[...]) + _dot(xi, s64_ref[...])) * scale
        o_ref[cc] = _dot(re.astype(BF16), wf_ref[...]).astype(BF16)


def _block_diag(m):
    G, n, _ = m.shape
    eye = jnp.eye(G, dtype=m.dtype)
    return (eye[:, None, :, None] * m[:, :, None, :]).reshape(G * n, G * n)


def _fourier(f_in, w_fourier):
    B, S, W = f_in.shape
    n2 = DFT_INNER
    n1 = S // n2
    cols = n2 * W
    tcw = 4096
    c1, s1 = _cos_sin(n1, n1, n1)
    d1 = jnp.concatenate([c1, -s1], axis=0).astype(BF16)
    xa = f_in.reshape(B, n1, cols)
    blk = pl.BlockSpec((None, n1, tcw), lambda b, j: (b, 0, j))
    yshape = jax.ShapeDtypeStruct((B, n1, cols), BF16)
    yr, yi = pl.pallas_call(
        functools.partial(_dft_a_body, n1),
        grid=(B, cols // tcw),
        in_specs=[pl.BlockSpec((2 * n1, n1), lambda b, j: (0, 0)), blk],
        out_specs=[blk, blk],
        out_shape=[yshape, yshape],
        compiler_params=_params(("parallel", "parallel")),
        name="dft_a",
    )(d1, xa)

    yr = yr.reshape(B, n1, n2, W)
    yi = yi.reshape(B, n1, n2, W)
    twc, tws = _cos_sin(n1, n2, S)
    twc = jnp.broadcast_to(twc[:, :, None], (n1, n2, LANES))
    tws = jnp.broadcast_to(tws[:, :, None], (n1, n2, LANES))
    c2, s2 = _cos_sin(n2, n2, n2)
    d2 = jnp.concatenate([jnp.concatenate([c2, s2], axis=1),
                          jnp.concatenate([-s2, c2], axis=1)], axis=0).astype(BF16)
    cg, sg = _cos_sin(HEAD_DIM, HEAD_DIM, HEAD_DIM)
    c64 = _block_diag(jnp.broadcast_to(cg, (N_GROUPS, HEAD_DIM, HEAD_DIM))).astype(BF16)
    s64 = _block_diag(jnp.broadcast_to(sg, (N_GROUPS, HEAD_DIM, HEAD_DIM))).astype(BF16)
    wf = _block_diag(w_fourier).astype(BF16)
    ct = 8
    scale = float(1.0 / np.sqrt(S * HEAD_DIM))
    yblk = pl.BlockSpec((None, ct, n2, W), lambda b, i: (b, i, 0, 0))
    tblk = pl.BlockSpec((ct, n2, LANES), lambda b, i: (i, 0, 0))
    full = lambda r, c: pl.BlockSpec((r, c), lambda b, i: (0, 0))
    out = pl.pallas_call(
        functools.partial(_dft_b_body, ct, scale),
        grid=(B, n1 // ct),
        in_specs=[yblk, yblk, tblk, tblk, full(2 * n2, 2 * n2), full(W, W), full(W, W), full(W, W)],
        out_specs=yblk,
        out_shape=jax.ShapeDtypeStruct((B, n1, n2, W), BF16),
        compiler_params=_params(("parallel", "parallel")),
        name="dft_b",
    )(yr, yi, twc, tws, d2, c64, s64, wf)
    return out.transpose(0, 2, 1, 3).reshape(B, S, W)


def _attn_body(tq, L, q_ref, kp_ref, kc_ref, kn_ref, vp_ref, vc_ref, vn_ref, o_ref, lse_ref):
    i = pl.program_id(2)
    sub = 128
    win = sub + 2 * HALF_WINDOW
    lane = lax.broadcasted_iota(jnp.int32, (sub, LANES), 1)
    lo_half = lane < HEAD_DIM
    r = lax.broadcasted_iota(jnp.int32, (sub, win), 0)
    c = lax.broadcasted_iota(jnp.int32, (sub, win), 1)
    band = (c >= r) & (c <= r + 2 * HALF_WINDOW)
    for pair in range(ATTN_WIDTH // LANES):
        sl = slice(pair * LANES, (pair + 1) * LANES)
        k2 = jnp.concatenate([kp_ref[:, sl], kc_ref[:, sl], kn_ref[:, sl]], axis=0)
        v2 = jnp.concatenate([vp_ref[:, sl], vc_ref[:, sl], vn_ref[:, sl]], axis=0)
        for sb in range(tq // sub):
            rows = slice(sb * sub, (sb + 1) * sub)
            qf = q_ref[rows, sl].astype(F32)
            ks = k2[sb * sub:sb * sub + win]
            vs = v2[sb * sub:sb * sub + win]
            kpos = i * tq + sb * sub - HALF_WINDOW + c
            valid = band & (kpos >= 0) & (kpos < L)
            outs, lses = [], []
            for hh in range(2):
                mine = lo_half if hh == 0 else jnp.logical_not(lo_half)
                qm = jnp.where(mine, qf, 0.0).astype(BF16)
                s = jnp.where(valid, _dot_nt(qm, ks), NEG)
                m = jnp.max(s, axis=1, keepdims=True)
                p = jnp.exp(s - m)
                l = jnp.sum(p, axis=1, keepdims=True)
                o = _dot(p.astype(BF16), vs) * (1.0 / l)
                outs.append(o)
                lses.append(m + jnp.log(l))
            o_ref[rows, sl] = jnp.where(lo_half, outs[0], outs[1])
            lse_ref[rows, sl] = jnp.where(lo_half, lses[0], lses[1])


def _attention_group(q, k, v, R):
    B, S, W = q.shape
    L = S // R
    tq = min(256, L)
    hb = tq // HALF_WINDOW
    nhalo = L // HALF_WINDOW
    view = lambda a: a.reshape(B, L, R * W)
    ctr = pl.BlockSpec((None, tq, W), lambda b, r, i: (b, i, r))
    prev = pl.BlockSpec((None, HALF_WINDOW, W), lambda b, r, i: (b, jnp.maximum(i * hb - 1, 0), r))
    nxt = pl.BlockSpec((None, HALF_WINDOW, W), lambda b, r, i: (b, jnp.minimum((i + 1) * hb, nhalo - 1), r))
    shp = jax.ShapeDtypeStruct((B, L, R * W), F32)
    o, lse = pl.pallas_call(
        functools.partial(_attn_body, tq, L),
        grid=(B, R, L // tq),
        in_specs=[ctr, prev, ctr, nxt, prev, ctr, nxt],
        out_specs=[ctr, ctr],
        out_shape=[shp, shp],
        compiler_params=_params(("parallel", "parallel", "parallel")),
        name=f"attn_d{R}",
    )(view(q), view(k), view(k), view(k), view(v), view(v), view(v))
    return o.reshape(B, S, W), lse.reshape(B, S, W)


def _mix_body(x_ref, f_ref, o1_ref, o2_ref, o3_ref, l1_ref, l2_ref, l3_ref, wf_ref, wa_ref, g_ref, wq_ref,
              x1_ref, xn_ref, qp_ref):
    l1, l2, l3 = l1_ref[...], l2_ref[...], l3_ref[...]
    mx = jnp.maximum(jnp.maximum(l1, l2), l3)
    e1, e2, e3 = jnp.exp(l1 - mx), jnp.exp(l2 - mx), jnp.exp(l3 - mx)
    a = (e1 * o1_ref[...] + e2 * o2_ref[...] + e3 * o3_ref[...]) * (1.0 / (e1 + e2 + e3))
    x1 = x_ref[...] + _dot(f_ref[...], wf_ref[...]) + _dot(a.astype(BF16), wa_ref[...])
    x1_ref[...] = x1
    ms = jnp.mean(x1 * x1, axis=-1, keepdims=True)
    xn = (x1 * lax.rsqrt(ms + EPS) * g_ref[...]).astype(BF16)
    xn_ref[...] = xn
    qp_ref[...] = _dot(xn, wq_ref[...])


def _mix(x, f_out, os_, lses, w_out_bf, g2, wq_bf):
    N = x.shape[0]
    tm = 256
    row = lambda w: pl.BlockSpec((tm, w), lambda i: (i, 0))
    full = lambda r, c: pl.BlockSpec((r, c), lambda i: (0, 0))
    qw = wq_bf.shape[1]
    return pl.pallas_call(
        _mix_body,
        grid=(N // tm,),
        in_specs=[row(D_MODEL), row(FOURIER_WIDTH)] + [row(ATTN_WIDTH)] * 6
                 + [full(FOURIER_WIDTH, D_MODEL), full(ATTN_WIDTH, D_MODEL), full(1, D_MODEL), full(D_MODEL, qw)],
        out_specs=[row(D_MODEL), row(D_MODEL), row(qw)],
        out_shape=[jax.ShapeDtypeStruct((N, D_MODEL), F32), jax.ShapeDtypeStruct((N, D_MODEL), BF16),
                   jax.ShapeDtypeStruct((N, qw), F32)],
        compiler_params=_params(("parallel",)),
        name="mix",
    )(x, f_out, *os_, *lses, w_out_bf[:FOURIER_WIDTH], w_out_bf[FOURIER_WIDTH:], g2.reshape(1, D_MODEL), wq_bf)


def _extract_topk(s, ids):
    tq = s.shape[1]
    rank = lax.broadcasted_iota(jnp.int32, (TOPK, tq), 0)
    vals = jnp.zeros((TOPK, tq), F32)
    idxs = jnp.zeros((TOPK, tq), F32)
    for kk in range(TOPK):
        m = jnp.max(s, axis=0, keepdims=True)
        first = jnp.min(jnp.where(s == m, ids, ID_NONE), axis=0, keepdims=True)
        s = jnp.where(ids == first, -jnp.inf, s)
        vals = jnp.where(rank == kk, m, vals)
        idxs = jnp.where(rank == kk, first, idxs)
    return vals, idxs


def _select_rows(sel, table):
    out = jnp.zeros_like(sel)
    for a in range(TOPK):
        out = jnp.where(sel == float(a), table[a:a + 1, :], out)
    return out


def _candidate_ids():
    ids = [a * TOPK for a in range(TOPK)]
    for b in range(1, 8):
        ids += [a * TOPK + b if (a + 1) * (b + 1) <= TOPK else ID_NONE for a in range(8)]
    ids += list(range(8, TOPK))
    return np.asarray(ids, np.float32)


def _topk_body(qp_ref, keys_ref, ids_ref, i1_ref, i2_ref, g_ref):
    tq = qp_ref.shape[0]
    key_ids = lax.broadcasted_iota(jnp.int32, (N_KEYS, tq), 0).astype(F32)
    cand_ids = ids_ref[...]
    i1s, i2s, gs = [], [], []
    for h in range(PEER_HEADS):
        halves = []
        for p in range(2):
            j = h * 2 + p
            qc = qp_ref[:, j * N_KEYS:(j + 1) * N_KEYS].astype(BF16)
            st = _dot_nt(keys_ref[j], qc)
            halves.append(_extract_topk(st, key_ids))
        (v1, x1), (v2, x2) = halves
        cand = jnp.concatenate([v1 + v2[0:1, :]] + [v1[0:8, :] + v2[b:b + 1, :] for b in range(1, 8)]
                               + [v1[0:1, :] + v2[8:TOPK, :]], axis=0)
        cand = jnp.where(cand_ids < ID_NONE, cand, -jnp.inf)
        top_s, flat = _extract_topk(cand, cand_ids)
        a_sel = jnp.floor(flat * (1.0 / TOPK))
        b_sel = flat - a_sel * TOPK
        i1s.append(_select_rows(a_sel, x1))
        i2s.append(_select_rows(b_sel, x2))
        e = jnp.exp(top_s - top_s[0:1, :])
        gs.append(e * (1.0 / jnp.sum(e, axis=0, keepdims=True)))
    i1_ref[...] = jnp.concatenate(i1s, axis=0).T.astype(jnp.int32)
    i2_ref[...] = jnp.concatenate(i2s, axis=0).T.astype(jnp.int32)
    g_ref[...] = jnp.concatenate(gs, axis=0).T


def _peer_topk(qp, keys_bf):
    N, qw = qp.shape
    tq = 256
    nsel = PEER_HEADS * TOPK
    ids = _candidate_ids()
    ids = jnp.asarray(np.broadcast_to(ids[:, None], (ids.shape[0], tq)))
    out = pl.BlockSpec((tq, nsel), lambda i: (i, 0))
    return pl.pallas_call(
        _topk_body,
        grid=(N // tq,),
        in_specs=[pl.BlockSpec((tq, qw), lambda i: (i, 0)),
                  pl.BlockSpec((2 * PEER_HEADS, N_KEYS, N_KEYS), lambda i: (0, 0, 0)),
                  pl.BlockSpec(ids.shape, lambda i: (0, 0))],
        out_specs=[out, out, out],
        out_shape=[jax.ShapeDtypeStruct((N, nsel), jnp.int32), jax.ShapeDtypeStruct((N, nsel), jnp.int32),
                   jax.ShapeDtypeStruct((N, nsel), F32)],
        compiler_params=_params(("parallel",)),
        name="peer_topk",
    )(qp, keys_bf, ids)


def _pack_bf16_pair(a, b):
    abits = lax.bitcast_convert_type(a.astype(BF16).astype(F32), jnp.uint32)
    bbits = lax.bitcast_convert_type(b.astype(BF16).astype(F32), jnp.uint32)
    return lax.bitcast_convert_type(abits | (bbits >> 16), jnp.int32)


def _unpack_bf16_pair(word):
    bits = lax.bitcast_convert_type(word, jnp.uint32)
    hi = lax.bitcast_convert_type(bits & jnp.uint32(0xFFFF0000), F32)
    lo = lax.bitcast_convert_type(bits << 16, F32)
    return hi, lo

def _peer_body(tq, nh, x1_ref, xn_ref, i1_ref, i2_ref, g_ref, ul_ref, uh_ref, vl_ref, vh_ref, gf_ref,
               o_ref, gate_ref):
    c = pl.program_id(1)
    half_keys = N_KEYS // 2

    @pl.when(c == 0)
    def _():
        o_ref[...] = jnp.zeros_like(o_ref)
        row = lax.broadcasted_iota(jnp.int32, (N_KEYS, N_KEYS), 0)

        def build(blk, carry):
            base = pl.multiple_of(blk * 8, 8)
            i1b, i2b, gb = i1_ref[pl.ds(base, 8), :], i2_ref[pl.ds(base, 8), :], g_ref[pl.ds(base, 8), :]
            for r in range(8):
                p1 = jnp.where(i1b[r:r + 1, :] == row, gb[r:r + 1, :], 0.0).astype(BF16)
                p2 = jnp.where(i2b[r:r + 1, :] == row, 1.0, 0.0).astype(BF16)
                gt = _dot_nt(p1, p2)
                word = _pack_bf16_pair(gt[:half_keys], gt[half_keys:])
                gate_ref[pl.ds(pl.multiple_of((base + r) * G_PITCH, 8), half_keys), :] = word
            return carry

        lax.fori_loop(0, tq // 8, build, 0)

    xn = xn_ref[...]
    gates = [_unpack_bf16_pair(gate_ref[pl.ds(c * nh + ii, tq, stride=G_PITCH), :])
             for ii in range(nh)]
    contrib = []
    for half, (u_ref, v_ref) in enumerate(((ul_ref, vl_ref), (uh_ref, vh_ref))):
        a = _dot(xn, u_ref[...])
        act = 0.5 * a * (1.0 + lax.erf(a * float(np.sqrt(0.5))))
        w = jnp.concatenate([(act[:, ii * N_KEYS:(ii + 1) * N_KEYS] * gates[ii][half]).astype(BF16)
                             for ii in range(nh)], axis=1)
        contrib.append(_dot(w, v_ref[...]))
    o_ref[...] += contrib[0] + contrib[1]

    @pl.when(c == pl.num_programs(1) - 1)
    def _():
        y = x1_ref[...] + o_ref[...]
        ms = jnp.mean(y * y, axis=-1, keepdims=True)
        o_ref[...] = y * lax.rsqrt(ms + EPS) * gf_ref[...]


def _peer(x1, xn, i1, i2, g, ut_bf, v_bf, gf):
    N = x1.shape[0]
    tq = 512
    nh = 4
    ec = nh * N_KEYS
    nsteps = (N_KEYS // 2) // nh
    nsel = PEER_HEADS * TOPK
    tok = lambda w: pl.BlockSpec((tq, w), lambda i, c: (i, 0))
    return pl.pallas_call(
        functools.partial(_peer_body, tq, nh),
        grid=(N // tq, nsteps),
        in_specs=[tok(D_MODEL), tok(D_MODEL), tok(nsel), tok(nsel), tok(nsel),
                  pl.BlockSpec((D_MODEL, ec), lambda i, c: (0, c)),
                  pl.BlockSpec((D_MODEL, ec), lambda i, c: (0, c + nsteps)),
                  pl.BlockSpec((ec, D_MODEL), lambda i, c: (c, 0)),
                  pl.BlockSpec((ec, D_MODEL), lambda i, c: (c + nsteps, 0)),
                  pl.BlockSpec((1, D_MODEL), lambda i, c: (0, 0))],
        out_specs=tok(D_MODEL),
        out_shape=jax.ShapeDtypeStruct((N, D_MODEL), F32),
        scratch_shapes=[pltpu.VMEM((tq * G_PITCH, N_KEYS), jnp.int32)],
        compiler_params=_params(("parallel", "arbitrary")),
        name="peer",
    )(x1, xn, i1, i2, g, ut_bf, ut_bf, v_bf, v_bf, gf.reshape(1, D_MODEL))


def _encoder(x, w):
    B, S, _ = x.shape
    f_in, q, k, v = _proj(x, w["norm1_g"], w["w_in"])
    f_out = _fourier(f_in, w["w_fourier"])
    groups = [_attention_group(q, k, v, R) for R in DILATIONS]
    N = B * S
    flat = lambda a: a.reshape(N, a.shape[-1])
    x1, xn, qp = _mix(flat(x), flat(f_out), [flat(o) for o, _ in groups], [flat(l) for _, l in groups],
                      w["w_out"], w["norm2_g"], w["w_query"])
    i1, i2, g = _peer_topk(qp, w["sub_keys"])
    y = _peer(x1, xn, i1, i2, g, w["expert_u"], w["expert_v"], w["final_g"])
    return y.reshape(B, S, D_MODEL)


def kernel(x_prompt, x_sample, norm1_g, w_in, w_fourier, w_out, norm2_g, w_query, sub_keys, expert_u, expert_v, final_g):
    w = {
        "norm1_g": norm1_g[0], "w_in": w_in[0].astype(BF16), "w_fourier": w_fourier[0],
        "w_out": w_out[0].astype(BF16), "norm2_g": norm2_g[0], "w_query": w_query[0].astype(BF16),
        "sub_keys": sub_keys[0].reshape(2 * PEER_HEADS, N_KEYS, N_KEYS).astype(BF16),
        "expert_u": expert_u[0].astype(BF16).T, "expert_v": expert_v[0].astype(BF16), "final_g": final_g,
    }
    return (_encoder(x_prompt, w), _encoder(x_sample, w))
```

```python
import functools

import numpy as np
import jax
import jax.numpy as jnp
from jax import lax
from jax.experimental import pallas as pl
from jax.experimental.pallas import tpu as pltpu

F32 = jnp.float32
BF16 = jnp.bfloat16

D_MODEL = 1024
HEAD_DIM = 64
FOURIER_WIDTH = 512
ATTN_WIDTH = 512
IN_WIDTH = 2048
N_GROUPS = 8
DILATIONS = (1, 4, 16)
HALF_WINDOW = 64
ROPE_THETA = 500000.0
ROPE_DIM = 16
PEER_HEADS = 8
N_KEYS = 128
N_EXPERTS = N_KEYS * N_KEYS
TOPK = 16
EPS = 1e-6
NEG = -1e30

LANES = 128
DFT_INNER = 128
G_PITCH = 72
ID_NONE = 1e9
VMEM_LIMIT = 56 * 2**20


def _params(sem, vmem=None):
    return pltpu.CompilerParams(dimension_semantics=sem, vmem_limit_bytes=vmem or VMEM_LIMIT)


def _dot(a, b):
    return jnp.dot(a, b, preferred_element_type=F32)


def _dot_nt(a, b):
    return lax.dot_general(a, b, (((1,), (1,)), ((), ())), preferred_element_type=F32)


def _proj_body(tm, x_ref, g_ref, w_ref, c_ref, sa_ref, sb_ref,
               f_ref, q_ref, k_ref, v_ref, q4_ref, k4_ref, v4_ref, q16_ref, k16_ref, v16_ref, stage_ref):
    x = x_ref[...]
    ms = jnp.mean(x * x, axis=-1, keepdims=True)
    h = (x * lax.rsqrt(ms + EPS) * g_ref[...]).astype(BF16)
    p = _dot(h, w_ref[...])
    f_ref[...] = p[:, :FOURIER_WIDTH].astype(BF16)
    c, sa, sb = c_ref[...], sa_ref[...], sb_ref[...]
    for j in range(ATTN_WIDTH // LANES):
        lo = j * LANES
        qc = p[:, FOURIER_WIDTH + lo:FOURIER_WIDTH + lo + LANES]
        kc = p[:, FOURIER_WIDTH + ATTN_WIDTH + lo:FOURIER_WIDTH + ATTN_WIDTH + lo + LANES]
        vc = p[:, FOURIER_WIDTH + 2 * ATTN_WIDTH + lo:FOURIER_WIDTH + 2 * ATTN_WIDTH + lo + LANES]
        qr = (qc * c + pltpu.roll(qc, LANES - 8, 1) * sa + pltpu.roll(qc, 8, 1) * sb) * (HEAD_DIM ** -0.5)
        kr = kc * c + pltpu.roll(kc, LANES - 8, 1) * sa + pltpu.roll(kc, 8, 1) * sb
        for a, (val, nat) in enumerate(((qr, q_ref), (kr, k_ref), (vc, v_ref))):
            nat[:, lo:lo + LANES] = val.astype(BF16)
            stage_ref[a, j] = val
    for a, dil in enumerate(((q4_ref, q16_ref), (k4_ref, k16_ref), (v4_ref, v16_ref))):
        for R, d_ref in zip(DILATIONS[1:], dil):
            for j in range(ATTN_WIDTH // LANES):
                for r in range(R):
                    piece = stage_ref[a, j, pl.ds(r, tm // R, stride=R), :]
                    d_ref[:, (j * R + r) * LANES:(j * R + r + 1) * LANES] = piece.astype(BF16)


def _rope_tables(S):
    half = ROPE_DIM // 2
    inv = ROPE_THETA ** (-(jnp.arange(half, dtype=F32) * 2.0) / ROPE_DIM)
    ang = jnp.arange(S, dtype=F32)[:, None] * inv[None, :]
    cos, sin = jnp.cos(ang), jnp.sin(ang)
    one = jnp.ones((S, HEAD_DIM - ROPE_DIM), F32)
    zero = jnp.zeros((S, HEAD_DIM - ROPE_DIM), F32)
    z8 = jnp.zeros((S, half), F32)
    c = jnp.concatenate([cos, cos, one], axis=1)
    sa = jnp.concatenate([-sin, z8, zero], axis=1)
    sb = jnp.concatenate([z8, sin, zero], axis=1)
    rep = LANES // HEAD_DIM
    return jnp.tile(c, (1, rep)), jnp.tile(sa, (1, rep)), jnp.tile(sb, (1, rep))


def _proj(x, g, w_bf):
    B, S, _ = x.shape
    tm = min(512, S)
    c, sa, sb = _rope_tables(S)
    tab = pl.BlockSpec((tm, LANES), lambda b, i: (i, 0))
    out = pl.BlockSpec((None, tm, ATTN_WIDTH), lambda b, i: (b, i, 0))
    shp = jax.ShapeDtypeStruct((B, S, ATTN_WIDTH), BF16)
    dil_specs, dil_shapes = [], []
    for R in DILATIONS[1:]:
        dil_specs.append(pl.BlockSpec((None, tm // R, R * ATTN_WIDTH), lambda b, i: (b, i, 0)))
        dil_shapes.append(jax.ShapeDtypeStruct((B, S // R, R * ATTN_WIDTH), BF16))
    outs = pl.pallas_call(
        functools.partial(_proj_body, tm),
        grid=(B, S // tm),
        in_specs=[pl.BlockSpec((None, tm, D_MODEL), lambda b, i: (b, i, 0)),
                  pl.BlockSpec((1, D_MODEL), lambda b, i: (0, 0)),
                  pl.BlockSpec((D_MODEL, IN_WIDTH), lambda b, i: (0, 0)),
                  tab, tab, tab],
        out_specs=[out] * 4 + [dil_specs[0]] * 3 + [dil_specs[1]] * 3,
        out_shape=[shp] * 4 + [dil_shapes[0]] * 3 + [dil_shapes[1]] * 3,
        scratch_shapes=[pltpu.VMEM((3, ATTN_WIDTH // LANES, tm, LANES), F32)],
        compiler_params=_params(("parallel", "parallel")),
        name="proj",
    )(x, g.reshape(1, D_MODEL), w_bf, c, sa, sb)
    return outs[0], outs[1:4], outs[4:7], outs[7:10]


def _cos_sin(n_rows, n_cols, period):
    i = jnp.arange(n_rows, dtype=jnp.int32)[:, None]
    j = jnp.arange(n_cols, dtype=jnp.int32)[None, :]
    ang = ((i * j) % period).astype(F32) * (2.0 * np.pi / period)
    return jnp.cos(ang), jnp.sin(ang)


def _dft_a_body(n1, d_ref, x_ref, yr_ref, yi_ref):
    y = _dot(d_ref[...], x_ref[...])
    yr_ref[...] = y[:n1].astype(BF16)
    yi_ref[...] = y[n1:].astype(BF16)


def _dft_b_body(ct, scale, yr_ref, yi_ref, tc_ref, ts_ref, d2_ref, c64_ref, s64_ref, wf_ref, o_ref):
    rep = FOURIER_WIDTH // LANES
    for cc in range(ct):
        yr = yr_ref[cc].astype(F32)
        yi = yi_ref[cc].astype(F32)
        tc = jnp.concatenate([tc_ref[cc]] * rep, axis=1)
        ts = jnp.concatenate([ts_ref[cc]] * rep, axis=1)
        zr = yr * tc + yi * ts
        zi = yi * tc - yr * ts
        z = jnp.concatenate([zr, zi], axis=0).astype(BF16)
        xx = _dot(d2_ref[...], z)
        xr = xx[:DFT_INNER].astype(BF16)
        xi = xx[DFT_INNER:].astype(BF16)
        re = (_dot(xr, c64_ref[...]) + _dot(xi, s64_ref[...])) * scale
        o_ref[cc] = _dot(re.astype(BF16), wf_ref[...]).astype(BF16)


def _block_diag(m):
    G, n, _ = m.shape
    eye = jnp.eye(G, dtype=m.dtype)
    return (eye[:, None, :, None] * m[:, :, None, :]).reshape(G * n, G * n)


def _fourier(f_in, w_fourier):
    B, S, W = f_in.shape
    n2 = DFT_INNER
    n1 = S // n2
    cols = n2 * W
    tcw = 4096
    c1, s1 = _cos_sin(n1, n1, n1)
    d1 = jnp.concatenate([c1, -s1], axis=0).astype(BF16)
    xa = f_in.reshape(B, n1, cols)
    blk = pl.BlockSpec((None, n1, tcw), lambda b, j: (b, 0, j))
    yshape = jax.ShapeDtypeStruct((B, n1, cols), BF16)
    yr, yi = pl.pallas_call(
        functools.partial(_dft_a_body, n1),
        grid=(B, cols // tcw),
        in_specs=[pl.BlockSpec((2 * n1, n1), lambda b, j: (0, 0)), blk],
        out_specs=[blk, blk],
        out_shape=[yshape, yshape],
        compiler_params=_params(("parallel", "parallel")),
        name="dft_a",
    )(d1, xa)

    yr = yr.reshape(B, n1, n2, W)
    yi = yi.reshape(B, n1, n2, W)
    twc, tws = _cos_sin(n1, n2, S)
    twc = jnp.broadcast_to(twc[:, :, None], (n1, n2, LANES))
    tws = jnp.broadcast_to(tws[:, :, None], (n1, n2, LANES))
    c2, s2 = _cos_sin(n2, n2, n2)
    d2 = jnp.concatenate([jnp.concatenate([c2, s2], axis=1),
                          jnp.concatenate([-s2, c2], axis=1)], axis=0).astype(BF16)
    cg, sg = _cos_sin(HEAD_DIM, HEAD_DIM, HEAD_DIM)
    c64 = _block_diag(jnp.broadcast_to(cg, (N_GROUPS, HEAD_DIM, HEAD_DIM))).astype(BF16)
    s64 = _block_diag(jnp.broadcast_to(sg, (N_GROUPS, HEAD_DIM, HEAD_DIM))).astype(BF16)
    wf = _block_diag(w_fourier).astype(BF16)
    ct = 8
    scale = float(1.0 / np.sqrt(S * HEAD_DIM))
    yblk = pl.BlockSpec((None, ct, n2, W), lambda b, i: (b, i, 0, 0))
    tblk = pl.BlockSpec((ct, n2, LANES), lambda b, i: (i, 0, 0))
    full = lambda r, c: pl.BlockSpec((r, c), lambda b, i: (0, 0))
    out = pl.pallas_call(
        functools.partial(_dft_b_body, ct, scale),
        grid=(B, n1 // ct),
        in_specs=[yblk, yblk, tblk, tblk, full(2 * n2, 2 * n2), full(W, W), full(W, W), full(W, W)],
        out_specs=yblk,
        out_shape=jax.ShapeDtypeStruct((B, n1, n2, W), BF16),
        compiler_params=_params(("parallel", "parallel")),
        name="dft_b",
    )(yr, yi, twc, tws, d2, c64, s64, wf)
    return out.transpose(0, 2, 1, 3).reshape(B, S, W)


def _band_attend(qf, ks, vs, valid, lo_half):
    outs, lses = [], []
    for hh in range(2):
        mine = lo_half if hh == 0 else jnp.logical_not(lo_half)
        qm = jnp.where(mine, qf, 0.0).astype(BF16)
        s = jnp.where(valid, _dot_nt(qm, ks), NEG)
        m = jnp.max(s, axis=1, keepdims=True)
        p = jnp.exp(s - m)
        l = jnp.sum(p, axis=1, keepdims=True)
        outs.append(_dot(p.astype(BF16), vs) * (1.0 / l))
        lses.append(m + jnp.log(l))
    return jnp.where(lo_half, outs[0], outs[1]), jnp.where(lo_half, lses[0], lses[1])


def _attn_masks():
    sub, win = 128, 128 + 2 * HALF_WINDOW
    lane = lax.broadcasted_iota(jnp.int32, (sub, LANES), 1)
    r = lax.broadcasted_iota(jnp.int32, (sub, win), 0)
    c = lax.broadcasted_iota(jnp.int32, (sub, win), 1)
    return lane < HEAD_DIM, (c >= r) & (c <= r + 2 * HALF_WINDOW), c


def _attn_body(tq, L, q_ref, kp_ref, kc_ref, kn_ref, vp_ref, vc_ref, vn_ref, o_ref, lse_ref):
    i = pl.program_id(1)
    sub, win = 128, 128 + 2 * HALF_WINDOW
    lo_half, band, c = _attn_masks()
    for pair in range(ATTN_WIDTH // LANES):
        sl = slice(pair * LANES, (pair + 1) * LANES)
        k2 = jnp.concatenate([kp_ref[:, sl], kc_ref[:, sl], kn_ref[:, sl]], axis=0)
        v2 = jnp.concatenate([vp_ref[:, sl], vc_ref[:, sl], vn_ref[:, sl]], axis=0)
        for sb in range(tq // sub):
            rows = slice(sb * sub, (sb + 1) * sub)
            kpos = i * tq + sb * sub - HALF_WINDOW + c
            valid = band & (kpos >= 0) & (kpos < L)
            o, lse = _band_attend(q_ref[rows, sl].astype(F32), k2[sb * sub:sb * sub + win],
                                  v2[sb * sub:sb * sub + win], valid, lo_half)
            o_ref[rows, sl] = o
            lse_ref[rows, sl] = lse


def _attn_dil_body(R, tq, L, q_ref, kp_ref, kc_ref, kn_ref, vp_ref, vc_ref, vn_ref, o_ref, lse_ref):
    i = pl.program_id(1)
    sub, win = 128, 128 + 2 * HALF_WINDOW
    lo_half, band, c = _attn_masks()
    for r in range(R):
        sl = slice(r * LANES, (r + 1) * LANES)
        k2 = jnp.concatenate([kp_ref[:, sl], kc_ref[:, sl], kn_ref[:, sl]], axis=0)
        v2 = jnp.concatenate([vp_ref[:, sl], vc_ref[:, sl], vn_ref[:, sl]], axis=0)
        for sb in range(tq // sub):
            kpos = i * tq + sb * sub - HALF_WINDOW + c
            valid = band & (kpos >= 0) & (kpos < L)
            o, lse = _band_attend(q_ref[sb * sub:(sb + 1) * sub, sl].astype(F32), k2[sb * sub:sb * sub + win],
                                  v2[sb * sub:sb * sub + win], valid, lo_half)
            dst = pl.ds(R * sb * sub + r, sub, stride=R)
            o_ref[dst, :] = o
            lse_ref[dst, :] = lse


def _attention_natural(q, k, v):
    B, S, W = q.shape
    tq = min(256, S)
    hb = tq // HALF_WINDOW
    nhalo = S // HALF_WINDOW
    ctr = pl.BlockSpec((None, tq, W), lambda b, i: (b, i, 0))
    prev = pl.BlockSpec((None, HALF_WINDOW, W), lambda b, i: (b, jnp.maximum(i * hb - 1, 0), 0))
    nxt = pl.BlockSpec((None, HALF_WINDOW, W), lambda b, i: (b, jnp.minimum((i + 1) * hb, nhalo - 1), 0))
    shp = jax.ShapeDtypeStruct((B, S, W), F32)
    return pl.pallas_call(
        functools.partial(_attn_body, tq, S),
        grid=(B, S // tq),
        in_specs=[ctr, prev, ctr, nxt, prev, ctr, nxt],
        out_specs=[ctr, ctr],
        out_shape=[shp, shp],
        compiler_params=_params(("parallel", "parallel")),
        name="attn_d1",
    )(q, k, k, k, v, v, v)


def _attention_dilated(q, k, v, R, S):
    B, L, _ = q.shape
    tq = min(2048 // R, L)
    hb = tq // HALF_WINDOW
    nhalo = L // HALF_WINDOW
    wb = R * LANES
    ctr = pl.BlockSpec((None, tq, wb), lambda b, i, p: (b, i, p))
    prev = pl.BlockSpec((None, HALF_WINDOW, wb), lambda b, i, p: (b, jnp.maximum(i * hb - 1, 0), p))
    nxt = pl.BlockSpec((None, HALF_WINDOW, wb), lambda b, i, p: (b, jnp.minimum((i + 1) * hb, nhalo - 1), p))
    out = pl.BlockSpec((None, R * tq, LANES), lambda b, i, p: (b, i, p))
    shp = jax.ShapeDtypeStruct((B, S, ATTN_WIDTH), F32)
    return pl.pallas_call(
        functools.partial(_attn_dil_body, R, tq, L),
        grid=(B, L // tq, ATTN_WIDTH // LANES),
        in_specs=[ctr, prev, ctr, nxt, prev, ctr, nxt],
        out_specs=[out, out],
        out_shape=[shp, shp],
        compiler_params=_params(("parallel", "parallel", "parallel")),
        name=f"attn_d{R}",
    )(q, k, k, k, v, v, v)


def _mix_body(x_ref, f_ref, o1_ref, o2_ref, o3_ref, l1_ref, l2_ref, l3_ref, wf_ref, wa_ref, g_ref, wq_ref,
              x1_ref, xn_ref, qp_ref):
    l1, l2, l3 = l1_ref[...], l2_ref[...], l3_ref[...]
    mx = jnp.maximum(jnp.maximum(l1, l2), l3)
    e1, e2, e3 = jnp.exp(l1 - mx), jnp.exp(l2 - mx), jnp.exp(l3 - mx)
    a = (e1 * o1_ref[...] + e2 * o2_ref[...] + e3 * o3_ref[...]) * (1.0 / (e1 + e2 + e3))
    x1 = x_ref[...] + _dot(f_ref[...], wf_ref[...]) + _dot(a.astype(BF16), wa_ref[...])
    x1_ref[...] = x1
    ms = jnp.mean(x1 * x1, axis=-1, keepdims=True)
    xn = (x1 * lax.rsqrt(ms + EPS) * g_ref[...]).astype(BF16)
    xn_ref[...] = xn
    qp_ref[...] = _dot(xn, wq_ref[...])


def _mix(x, f_out, os_, lses, w_out_bf, g2, wq_bf):
    N = x.shape[0]
    tm = 256
    row = lambda w: pl.BlockSpec((tm, w), lambda i: (i, 0))
    full = lambda r, c: pl.BlockSpec((r, c), lambda i: (0, 0))
    qw = wq_bf.shape[1]
    return pl.pallas_call(
        _mix_body,
        grid=(N // tm,),
        in_specs=[row(D_MODEL), row(FOURIER_WIDTH)] + [row(ATTN_WIDTH)] * 6
                 + [full(FOURIER_WIDTH, D_MODEL), full(ATTN_WIDTH, D_MODEL), full(1, D_MODEL), full(D_MODEL, qw)],
        out_specs=[row(D_MODEL), row(D_MODEL), row(qw)],
        out_shape=[jax.ShapeDtypeStruct((N, D_MODEL), F32), jax.ShapeDtypeStruct((N, D_MODEL), BF16),
                   jax.ShapeDtypeStruct((N, qw), F32)],
        compiler_params=_params(("parallel",)),
        name="mix",
    )(x, f_out, *os_, *lses, w_out_bf[:FOURIER_WIDTH], w_out_bf[FOURIER_WIDTH:], g2.reshape(1, D_MODEL), wq_bf)


def _extract_topk(s, ids):
    tq = s.shape[1]
    rank = lax.broadcasted_iota(jnp.int32, (TOPK, tq), 0)
    vals = jnp.zeros((TOPK, tq), F32)
    idxs = jnp.zeros((TOPK, tq), F32)
    for kk in range(TOPK):
        m = jnp.max(s, axis=0, keepdims=True)
        first = jnp.min(jnp.where(s == m, ids, ID_NONE), axis=0, keepdims=True)
        s = jnp.where(ids == first, -jnp.inf, s)
        vals = jnp.where(rank == kk, m, vals)
        idxs = jnp.where(rank == kk, first, idxs)
    return vals, idxs


def _select_rows(sel, table):
    out = jnp.zeros_like(sel)
    for a in range(TOPK):
        out = jnp.where(sel == float(a), table[a:a + 1, :], out)
    return out


def _candidate_ids():
    ids = [a * TOPK for a in range(TOPK)]
    for b in range(1, 8):
        ids += [a * TOPK + b if (a + 1) * (b + 1) <= TOPK else ID_NONE for a in range(8)]
    ids += list(range(8, TOPK))
    return np.asarray(ids, np.float32)


def _topk_body(qp_ref, keys_ref, ids_ref, i1_ref, i2_ref, g_ref):
    tq = qp_ref.shape[0]
    key_ids = lax.broadcasted_iota(jnp.int32, (N_KEYS, tq), 0).astype(F32)
    cand_ids = ids_ref[...]
    i1s, i2s, gs = [], [], []
    for h in range(PEER_HEADS):
        halves = []
        for p in range(2):
            j = h * 2 + p
            qc = qp_ref[:, j * N_KEYS:(j + 1) * N_KEYS].astype(BF16)
            st = _dot_nt(keys_ref[j], qc)
            halves.append(_extract_topk(st, key_ids))
        (v1, x1), (v2, x2) = halves
        cand = jnp.concatenate([v1 + v2[0:1, :]] + [v1[0:8, :] + v2[b:b + 1, :] for b in range(1, 8)]
                               + [v1[0:1, :] + v2[8:TOPK, :]], axis=0)
        cand = jnp.where(cand_ids < ID_NONE, cand, -jnp.inf)
        top_s, flat = _extract_topk(cand, cand_ids)
        a_sel = jnp.floor(flat * (1.0 / TOPK))
        b_sel = flat - a_sel * TOPK
        i1s.append(_select_rows(a_sel, x1))
        i2s.append(_select_rows(b_sel, x2))
        e = jnp.exp(top_s - top_s[0:1, :])
        gs.append(e * (1.0 / jnp.sum(e, axis=0, keepdims=True)))
    i1_ref[...] = jnp.concatenate(i1s, axis=0).T.astype(jnp.int32)
    i2_ref[...] = jnp.concatenate(i2s, axis=0).T.astype(jnp.int32)
    g_ref[...] = jnp.concatenate(gs, axis=0).T


def _peer_topk(qp, keys_bf):
    N, qw = qp.shape
    tq = 256
    nsel = PEER_HEADS * TOPK
    ids = _candidate_ids()
    ids = jnp.asarray(np.broadcast_to(ids[:, None], (ids.shape[0], tq)))
    out = pl.BlockSpec((tq, nsel), lambda i: (i, 0))
    return pl.pallas_call(
        _topk_body,
        grid=(N // tq,),
        in_specs=[pl.BlockSpec((tq, qw), lambda i: (i, 0)),
                  pl.BlockSpec((2 * PEER_HEADS, N_KEYS, N_KEYS), lambda i: (0, 0, 0)),
                  pl.BlockSpec(ids.shape, lambda i: (0, 0))],
        out_specs=[out, out, out],
        out_shape=[jax.ShapeDtypeStruct((N, nsel), jnp.int32), jax.ShapeDtypeStruct((N, nsel), jnp.int32),
                   jax.ShapeDtypeStruct((N, nsel), F32)],
        compiler_params=_params(("parallel",)),
        name="peer_topk",
    )(qp, keys_bf, ids)


def _pack_bf16_pair(a, b):
    abits = lax.bitcast_convert_type(a.astype(BF16).astype(F32), jnp.uint32)
    bbits = lax.bitcast_convert_type(b.astype(BF16).astype(F32), jnp.uint32)
    return lax.bitcast_convert_type(abits | (bbits >> 16), jnp.int32)


def _unpack_bf16_pair(word):
    bits = lax.bitcast_convert_type(word, jnp.uint32)
    hi = lax.bitcast_convert_type(bits & jnp.uint32(0xFFFF0000), F32)
    lo = lax.bitcast_convert_type(bits << 16, F32)
    return hi, lo

def _peer_body(tq, nh, n_chunks, x1_ref, xn_ref, i1_ref, i2_ref, g_ref, ul_ref, uh_ref, vl_ref, vh_ref, gf_ref,
               o_ref, gate_ref, w_ref):
    c = pl.program_id(1)
    half_keys = N_KEYS // 2

    @pl.when(c == 0)
    def _():
        o_ref[...] = jnp.zeros_like(o_ref)
        row = lax.broadcasted_iota(jnp.int32, (N_KEYS, N_KEYS), 0)

        def build(blk, carry):
            base = pl.multiple_of(blk * 8, 8)
            i1b, i2b, gb = i1_ref[pl.ds(base, 8), :], i2_ref[pl.ds(base, 8), :], g_ref[pl.ds(base, 8), :]
            for r in range(8):
                p1 = jnp.where(i1b[r:r + 1, :] == row, gb[r:r + 1, :], 0.0).astype(BF16)
                p2 = jnp.where(i2b[r:r + 1, :] == row, 1.0, 0.0).astype(BF16)
                gt = _dot_nt(p1, p2)
                word = _pack_bf16_pair(gt[:half_keys], gt[half_keys:])
                gate_ref[pl.ds(pl.multiple_of((base + r) * G_PITCH, 8), half_keys), :] = word
            return carry

        lax.fori_loop(0, tq // 8, build, 0)

    def activate(slot):
        xn = xn_ref[...]
        gates = [_unpack_bf16_pair(gate_ref[pl.ds(c * nh + ii, tq, stride=G_PITCH), :])
                 for ii in range(nh)]
        for half, u_ref in enumerate((ul_ref, uh_ref)):
            a = _dot(xn, u_ref[...])
            act = 0.5 * a * (1.0 + lax.erf(a * float(np.sqrt(0.5))))
            for ii in range(nh):
                cols = slice(ii * N_KEYS, (ii + 1) * N_KEYS)
                w_ref[slot, half, :, cols] = (act[:, cols] * gates[ii][half]).astype(BF16)

    def accumulate(slot):
        o_ref[...] += _dot(w_ref[slot, 0], vl_ref[...]) + _dot(w_ref[slot, 1], vh_ref[...])

    last = pl.num_programs(1) - 1
    odd = lax.rem(c, 2) == 1

    @pl.when(c == 0)
    def _():
        activate(0)

    @pl.when(jnp.logical_and(odd, c < last))
    def _():
        activate(1)
        accumulate(0)

    @pl.when(jnp.logical_and(jnp.logical_not(odd), jnp.logical_and(c > 0, c < last)))
    def _():
        activate(0)
        accumulate(1)

    @pl.when(c == last)
    def _():
        accumulate((n_chunks - 1) % 2)
        y = x1_ref[...] + o_ref[...]
        ms = jnp.mean(y * y, axis=-1, keepdims=True)
        o_ref[...] = y * lax.rsqrt(ms + EPS) * gf_ref[...]


def _peer(x1, xn, i1, i2, g, ut_bf, v_bf, gf):
    N = x1.shape[0]
    tq = 512
    nh = 4
    ec = nh * N_KEYS
    n_chunks = (N_KEYS // 2) // nh
    nsel = PEER_HEADS * TOPK
    tok = lambda w: pl.BlockSpec((tq, w), lambda i, c: (i, 0))
    u_chunk = lambda c: jnp.minimum(c, n_chunks - 1)
    v_chunk = lambda c: jnp.maximum(c - 1, 0)
    return pl.pallas_call(
        functools.partial(_peer_body, tq, nh, n_chunks),
        grid=(N // tq, n_chunks + 1),
        in_specs=[tok(D_MODEL), tok(D_MODEL), tok(nsel), tok(nsel), tok(nsel),
                  pl.BlockSpec((D_MODEL, ec), lambda i, c: (0, u_chunk(c))),
                  pl.BlockSpec((D_MODEL, ec), lambda i, c: (0, u_chunk(c) + n_chunks)),
                  pl.BlockSpec((ec, D_MODEL), lambda i, c: (v_chunk(c), 0)),
                  pl.BlockSpec((ec, D_MODEL), lambda i, c: (v_chunk(c) + n_chunks, 0)),
                  pl.BlockSpec((1, D_MODEL), lambda i, c: (0, 0))],
        out_specs=tok(D_MODEL),
        out_shape=jax.ShapeDtypeStruct((N, D_MODEL), F32),
        scratch_shapes=[pltpu.VMEM((tq * G_PITCH, N_KEYS), jnp.int32), pltpu.VMEM((2, 2, tq, ec), BF16)],
        compiler_params=_params(("parallel", "arbitrary")),
        name="peer",
    )(x1, xn, i1, i2, g, ut_bf, ut_bf, v_bf, v_bf, gf.reshape(1, D_MODEL))


def _encoder(x, w):
    B, S, _ = x.shape
    f_in, qkv1, qkv4, qkv16 = _proj(x, w["norm1_g"], w["w_in"])
    f_out = _fourier(f_in, w["w_fourier"])
    groups = [_attention_natural(*qkv1), _attention_dilated(*qkv4, DILATIONS[1], S),
              _attention_dilated(*qkv16, DILATIONS[2], S)]
    N = B * S
    flat = lambda a: a.reshape(N, a.shape[-1])
    x1, xn, qp = _mix(flat(x), flat(f_out), [flat(o) for o, _ in groups], [flat(l) for _, l in groups],
                      w["w_out"], w["norm2_g"], w["w_query"])
    i1, i2, g = _peer_topk(qp, w["sub_keys"])
    y = _peer(x1, xn, i1, i2, g, w["expert_u"], w["expert_v"], w["final_g"])
    return y.reshape(B, S, D_MODEL)


def kernel(x_prompt, x_sample, norm1_g, w_in, w_fourier, w_out, norm2_g, w_query, sub_keys, expert_u, expert_v, final_g):
    w = {
        "norm1_g": norm1_g[0], "w_in": w_in[0].astype(BF16), "w_fourier": w_fourier[0],
        "w_out": w_out[0].astype(BF16), "norm2_g": norm2_g[0], "w_query": w_query[0].astype(BF16),
        "sub_keys": sub_keys[0].reshape(2 * PEER_HEADS, N_KEYS, N_KEYS).astype(BF16),
        "expert_u": expert_u[0].astype(BF16).T, "expert_v": expert_v[0].astype(BF16), "final_g": final_g,
    }
    return (_encoder(x_prompt, w), _encoder(x_sample, w))
```

```python
import functools

import numpy as np
import jax
import jax.numpy as jnp
from jax import lax
from jax.experimental import pallas as pl
from jax.experimental.pallas import tpu as pltpu

F32 = jnp.float32
BF16 = jnp.bfloat16

D_MODEL = 1024
HEAD_DIM = 64
FOURIER_WIDTH = 512
ATTN_WIDTH = 512
IN_WIDTH = 2048
N_GROUPS = 8
DILATIONS = (1, 4, 16)
HALF_WINDOW = 64
ROPE_THETA = 500000.0
ROPE_DIM = 16
PEER_HEADS = 8
N_KEYS = 128
N_EXPERTS = N_KEYS * N_KEYS
TOPK = 16
EPS = 1e-6
NEG = -1e30

LANES = 128
DFT_INNER = 128
G_PITCH = 72
ID_NONE = 1e9
VMEM_LIMIT = 56 * 2**20


def _params(sem, vmem=None):
    return pltpu.CompilerParams(dimension_semantics=sem, vmem_limit_bytes=vmem or VMEM_LIMIT)


def _dot(a, b):
    return jnp.dot(a, b, preferred_element_type=F32)


def _dot_nt(a, b):
    return lax.dot_general(a, b, (((1,), (1,)), ((), ())), preferred_element_type=F32)


def _proj_body(tm, x_ref, g_ref, w_ref, c_ref, sa_ref, sb_ref,
               f_ref, q_ref, k_ref, v_ref, q4_ref, k4_ref, v4_ref, q16_ref, k16_ref, v16_ref, stage_ref):
    x = x_ref[...]
    ms = jnp.mean(x * x, axis=-1, keepdims=True)
    h = (x * lax.rsqrt(ms + EPS) * g_ref[...]).astype(BF16)
    p = _dot(h, w_ref[...])
    f_ref[...] = p[:, :FOURIER_WIDTH].astype(BF16)
    c, sa, sb = c_ref[...], sa_ref[...], sb_ref[...]
    for j in range(ATTN_WIDTH // LANES):
        lo = j * LANES
        qc = p[:, FOURIER_WIDTH + lo:FOURIER_WIDTH + lo + LANES]
        kc = p[:, FOURIER_WIDTH + ATTN_WIDTH + lo:FOURIER_WIDTH + ATTN_WIDTH + lo + LANES]
        vc = p[:, FOURIER_WIDTH + 2 * ATTN_WIDTH + lo:FOURIER_WIDTH + 2 * ATTN_WIDTH + lo + LANES]
        qr = (qc * c + pltpu.roll(qc, LANES - 8, 1) * sa + pltpu.roll(qc, 8, 1) * sb) * (HEAD_DIM ** -0.5)
        kr = kc * c + pltpu.roll(kc, LANES - 8, 1) * sa + pltpu.roll(kc, 8, 1) * sb
        for a, (val, nat) in enumerate(((qr, q_ref), (kr, k_ref), (vc, v_ref))):
            nat[:, lo:lo + LANES] = val.astype(BF16)
            stage_ref[a, j] = val
    for a, dil in enumerate(((q4_ref, q16_ref), (k4_ref, k16_ref), (v4_ref, v16_ref))):
        for R, d_ref in zip(DILATIONS[1:], dil):
            for j in range(ATTN_WIDTH // LANES):
                for r in range(R):
                    piece = stage_ref[a, j, pl.ds(r, tm // R, stride=R), :]
                    d_ref[:, (j * R + r) * LANES:(j * R + r + 1) * LANES] = piece.astype(BF16)


def _rope_tables(S):
    half = ROPE_DIM // 2
    inv = ROPE_THETA ** (-(jnp.arange(half, dtype=F32) * 2.0) / ROPE_DIM)
    ang = jnp.arange(S, dtype=F32)[:, None] * inv[None, :]
    cos, sin = jnp.cos(ang), jnp.sin(ang)
    one = jnp.ones((S, HEAD_DIM - ROPE_DIM), F32)
    zero = jnp.zeros((S, HEAD_DIM - ROPE_DIM), F32)
    z8 = jnp.zeros((S, half), F32)
    c = jnp.concatenate([cos, cos, one], axis=1)
    sa = jnp.concatenate([-sin, z8, zero], axis=1)
    sb = jnp.concatenate([z8, sin, zero], axis=1)
    rep = LANES // HEAD_DIM
    return jnp.tile(c, (1, rep)), jnp.tile(sa, (1, rep)), jnp.tile(sb, (1, rep))


def _proj(x, g, w_bf):
    B, S, _ = x.shape
    tm = min(512, S)
    c, sa, sb = _rope_tables(S)
    tab = pl.BlockSpec((tm, LANES), lambda b, i: (i, 0))
    out = pl.BlockSpec((None, tm, ATTN_WIDTH), lambda b, i: (b, i, 0))
    shp = jax.ShapeDtypeStruct((B, S, ATTN_WIDTH), BF16)
    dil_specs, dil_shapes = [], []
    for R in DILATIONS[1:]:
        dil_specs.append(pl.BlockSpec((None, tm // R, R * ATTN_WIDTH), lambda b, i: (b, i, 0)))
        dil_shapes.append(jax.ShapeDtypeStruct((B, S // R, R * ATTN_WIDTH), BF16))
    outs = pl.pallas_call(
        functools.partial(_proj_body, tm),
        grid=(B, S // tm),
        in_specs=[pl.BlockSpec((None, tm, D_MODEL), lambda b, i: (b, i, 0)),
                  pl.BlockSpec((1, D_MODEL), lambda b, i: (0, 0)),
                  pl.BlockSpec((D_MODEL, IN_WIDTH), lambda b, i: (0, 0)),
                  tab, tab, tab],
        out_specs=[out] * 4 + [dil_specs[0]] * 3 + [dil_specs[1]] * 3,
        out_shape=[shp] * 4 + [dil_shapes[0]] * 3 + [dil_shapes[1]] * 3,
        scratch_shapes=[pltpu.VMEM((3, ATTN_WIDTH // LANES, tm, LANES), F32)],
        compiler_params=_params(("parallel", "parallel")),
        name="proj",
    )(x, g.reshape(1, D_MODEL), w_bf, c, sa, sb)
    return outs[0], outs[1:4], outs[4:7], outs[7:10]


def _cos_sin(n_rows, n_cols, period):
    i = jnp.arange(n_rows, dtype=jnp.int32)[:, None]
    j = jnp.arange(n_cols, dtype=jnp.int32)[None, :]
    ang = ((i * j) % period).astype(F32) * (2.0 * np.pi / period)
    return jnp.cos(ang), jnp.sin(ang)


def _dft_a_body(n1, d_ref, x_ref, yr_ref, yi_ref):
    y = _dot(d_ref[...], x_ref[...])
    yr_ref[...] = y[:n1].astype(BF16)
    yi_ref[...] = y[n1:].astype(BF16)


def _dft_b_body(ct, scale, yr_ref, yi_ref, tc_ref, ts_ref, d2_ref, c64_ref, s64_ref, wf_ref, o_ref):
    rep = FOURIER_WIDTH // LANES
    for cc in range(ct):
        yr = yr_ref[cc].astype(F32)
        yi = yi_ref[cc].astype(F32)
        tc = jnp.concatenate([tc_ref[cc]] * rep, axis=1)
        ts = jnp.concatenate([ts_ref[cc]] * rep, axis=1)
        zr = yr * tc + yi * ts
        zi = yi * tc - yr * ts
        z = jnp.concatenate([zr, zi], axis=0).astype(BF16)
        xx = _dot(d2_ref[...], z)
        xr = xx[:DFT_INNER].astype(BF16)
        xi = xx[DFT_INNER:].astype(BF16)
        re = (_dot(xr, c64_ref[...]) + _dot(xi, s64_ref[...])) * scale
        o_ref[cc] = _dot(re.astype(BF16), wf_ref[...]).astype(BF16)


def _block_diag(m):
    G, n, _ = m.shape
    eye = jnp.eye(G, dtype=m.dtype)
    return (eye[:, None, :, None] * m[:, :, None, :]).reshape(G * n, G * n)


def _fourier(f_in, w_fourier):
    B, S, W = f_in.shape
    n2 = DFT_INNER
    n1 = S // n2
    cols = n2 * W
    tcw = 4096
    c1, s1 = _cos_sin(n1, n1, n1)
    d1 = jnp.concatenate([c1, -s1], axis=0).astype(BF16)
    xa = f_in.reshape(B, n1, cols)
    blk = pl.BlockSpec((None, n1, tcw), lambda b, j: (b, 0, j))
    yshape = jax.ShapeDtypeStruct((B, n1, cols), BF16)
    yr, yi = pl.pallas_call(
        functools.partial(_dft_a_body, n1),
        grid=(B, cols // tcw),
        in_specs=[pl.BlockSpec((2 * n1, n1), lambda b, j: (0, 0)), blk],
        out_specs=[blk, blk],
        out_shape=[yshape, yshape],
        compiler_params=_params(("parallel", "parallel")),
        name="dft_a",
    )(d1, xa)

    yr = yr.reshape(B, n1, n2, W)
    yi = yi.reshape(B, n1, n2, W)
    twc, tws = _cos_sin(n1, n2, S)
    twc = jnp.broadcast_to(twc[:, :, None], (n1, n2, LANES))
    tws = jnp.broadcast_to(tws[:, :, None], (n1, n2, LANES))
    c2, s2 = _cos_sin(n2, n2, n2)
    d2 = jnp.concatenate([jnp.concatenate([c2, s2], axis=1),
                          jnp.concatenate([-s2, c2], axis=1)], axis=0).astype(BF16)
    cg, sg = _cos_sin(HEAD_DIM, HEAD_DIM, HEAD_DIM)
    c64 = _block_diag(jnp.broadcast_to(cg, (N_GROUPS, HEAD_DIM, HEAD_DIM))).astype(BF16)
    s64 = _block_diag(jnp.broadcast_to(sg, (N_GROUPS, HEAD_DIM, HEAD_DIM))).astype(BF16)
    wf = _block_diag(w_fourier).astype(BF16)
    ct = 8
    scale = float(1.0 / np.sqrt(S * HEAD_DIM))
    yblk = pl.BlockSpec((None, ct, n2, W), lambda b, i: (b, i, 0, 0))
    tblk = pl.BlockSpec((ct, n2, LANES), lambda b, i: (i, 0, 0))
    full = lambda r, c: pl.BlockSpec((r, c), lambda b, i: (0, 0))
    out = pl.pallas_call(
        functools.partial(_dft_b_body, ct, scale),
        grid=(B, n1 // ct),
        in_specs=[yblk, yblk, tblk, tblk, full(2 * n2, 2 * n2), full(W, W), full(W, W), full(W, W)],
        out_specs=yblk,
        out_shape=jax.ShapeDtypeStruct((B, n1, n2, W), BF16),
        compiler_params=_params(("parallel", "parallel")),
        name="dft_b",
    )(yr, yi, twc, tws, d2, c64, s64, wf)
    return out.transpose(0, 2, 1, 3).reshape(B, S, W)


def _band_attend(qf, ks, vs, valid, lo_half):
    outs, lses = [], []
    for hh in range(2):
        mine = lo_half if hh == 0 else jnp.logical_not(lo_half)
        qm = jnp.where(mine, qf, 0.0).astype(BF16)
        s = jnp.where(valid, _dot_nt(qm, ks), NEG)
        m = jnp.max(s, axis=1, keepdims=True)
        p = jnp.exp(s - m)
        l = jnp.sum(p, axis=1, keepdims=True)
        outs.append(_dot(p.astype(BF16), vs) * (1.0 / l))
        lses.append(m + jnp.log(l))
    return jnp.where(lo_half, outs[0], outs[1]), jnp.where(lo_half, lses[0], lses[1])


def _attn_masks():
    sub, win = 128, 128 + 2 * HALF_WINDOW
    lane = lax.broadcasted_iota(jnp.int32, (sub, LANES), 1)
    r = lax.broadcasted_iota(jnp.int32, (sub, win), 0)
    c = lax.broadcasted_iota(jnp.int32, (sub, win), 1)
    return lane < HEAD_DIM, (c >= r) & (c <= r + 2 * HALF_WINDOW), c


def _attn_body(tq, L, q_ref, kp_ref, kc_ref, kn_ref, vp_ref, vc_ref, vn_ref, o_ref, lse_ref):
    i = pl.program_id(1)
    sub, win = 128, 128 + 2 * HALF_WINDOW
    lo_half, band, c = _attn_masks()
    for pair in range(ATTN_WIDTH // LANES):
        sl = slice(pair * LANES, (pair + 1) * LANES)
        k2 = jnp.concatenate([kp_ref[:, sl], kc_ref[:, sl], kn_ref[:, sl]], axis=0)
        v2 = jnp.concatenate([vp_ref[:, sl], vc_ref[:, sl], vn_ref[:, sl]], axis=0)
        for sb in range(tq // sub):
            rows = slice(sb * sub, (sb + 1) * sub)
            kpos = i * tq + sb * sub - HALF_WINDOW + c
            valid = band & (kpos >= 0) & (kpos < L)
            o, lse = _band_attend(q_ref[rows, sl].astype(F32), k2[sb * sub:sb * sub + win],
                                  v2[sb * sub:sb * sub + win], valid, lo_half)
            o_ref[rows, sl] = o
            lse_ref[rows, sl] = lse


def _attn_dil_body(R, tq, L, q_ref, kp_ref, kc_ref, kn_ref, vp_ref, vc_ref, vn_ref, o_ref, lse_ref):
    i = pl.program_id(1)
    sub, win = 128, 128 + 2 * HALF_WINDOW
    lo_half, band, c = _attn_masks()
    for r in range(R):
        sl = slice(r * LANES, (r + 1) * LANES)
        k2 = jnp.concatenate([kp_ref[:, sl], kc_ref[:, sl], kn_ref[:, sl]], axis=0)
        v2 = jnp.concatenate([vp_ref[:, sl], vc_ref[:, sl], vn_ref[:, sl]], axis=0)
        for sb in range(tq // sub):
            kpos = i * tq + sb * sub - HALF_WINDOW + c
            valid = band & (kpos >= 0) & (kpos < L)
            o, lse = _band_attend(q_ref[sb * sub:(sb + 1) * sub, sl].astype(F32), k2[sb * sub:sb * sub + win],
                                  v2[sb * sub:sb * sub + win], valid, lo_half)
            dst = pl.ds(R * sb * sub + r, sub, stride=R)
            o_ref[dst, :] = o
            lse_ref[dst, :] = lse


def _attention_natural(q, k, v):
    B, S, W = q.shape
    tq = min(256, S)
    hb = tq // HALF_WINDOW
    nhalo = S // HALF_WINDOW
    ctr = pl.BlockSpec((None, tq, W), lambda b, i: (b, i, 0))
    prev = pl.BlockSpec((None, HALF_WINDOW, W), lambda b, i: (b, jnp.maximum(i * hb - 1, 0), 0))
    nxt = pl.BlockSpec((None, HALF_WINDOW, W), lambda b, i: (b, jnp.minimum((i + 1) * hb, nhalo - 1), 0))
    shp = jax.ShapeDtypeStruct((B, S, W), F32)
    return pl.pallas_call(
        functools.partial(_attn_body, tq, S),
        grid=(B, S // tq),
        in_specs=[ctr, prev, ctr, nxt, prev, ctr, nxt],
        out_specs=[ctr, ctr],
        out_shape=[shp, shp],
        compiler_params=_params(("parallel", "parallel")),
        name="attn_d1",
    )(q, k, k, k, v, v, v)


def _attention_dilated(q, k, v, R, S):
    B, L, _ = q.shape
    tq = min(2048 // R, L)
    hb = tq // HALF_WINDOW
    nhalo = L // HALF_WINDOW
    wb = R * LANES
    ctr = pl.BlockSpec((None, tq, wb), lambda b, i, p: (b, i, p))
    prev = pl.BlockSpec((None, HALF_WINDOW, wb), lambda b, i, p: (b, jnp.maximum(i * hb - 1, 0), p))
    nxt = pl.BlockSpec((None, HALF_WINDOW, wb), lambda b, i, p: (b, jnp.minimum((i + 1) * hb, nhalo - 1), p))
    out = pl.BlockSpec((None, R * tq, LANES), lambda b, i, p: (b, i, p))
    shp = jax.ShapeDtypeStruct((B, S, ATTN_WIDTH), F32)
    return pl.pallas_call(
        functools.partial(_attn_dil_body, R, tq, L),
        grid=(B, L // tq, ATTN_WIDTH // LANES),
        in_specs=[ctr, prev, ctr, nxt, prev, ctr, nxt],
        out_specs=[out, out],
        out_shape=[shp, shp],
        compiler_params=_params(("parallel", "parallel", "parallel")),
        name=f"attn_d{R}",
    )(q, k, k, k, v, v, v)


def _mix_body(x_ref, f_ref, o1_ref, o2_ref, o3_ref, l1_ref, l2_ref, l3_ref, wf_ref, wa_ref, g_ref, wq_ref,
              x1_ref, xn_ref, qp_ref):
    l1, l2, l3 = l1_ref[...], l2_ref[...], l3_ref[...]
    mx = jnp.maximum(jnp.maximum(l1, l2), l3)
    e1, e2, e3 = jnp.exp(l1 - mx), jnp.exp(l2 - mx), jnp.exp(l3 - mx)
    a = (e1 * o1_ref[...] + e2 * o2_ref[...] + e3 * o3_ref[...]) * (1.0 / (e1 + e2 + e3))
    x1 = x_ref[...] + _dot(f_ref[...], wf_ref[...]) + _dot(a.astype(BF16), wa_ref[...])
    x1_ref[...] = x1
    ms = jnp.mean(x1 * x1, axis=-1, keepdims=True)
    xn = (x1 * lax.rsqrt(ms + EPS) * g_ref[...]).astype(BF16)
    xn_ref[...] = xn
    qp_ref[...] = _dot(xn, wq_ref[...]).astype(BF16)


def _mix(x, f_out, os_, lses, w_out_bf, g2, wq_bf):
    N = x.shape[0]
    tm = 256
    row = lambda w: pl.BlockSpec((tm, w), lambda i: (i, 0))
    full = lambda r, c: pl.BlockSpec((r, c), lambda i: (0, 0))
    qw = wq_bf.shape[1]
    return pl.pallas_call(
        _mix_body,
        grid=(N // tm,),
        in_specs=[row(D_MODEL), row(FOURIER_WIDTH)] + [row(ATTN_WIDTH)] * 6
                 + [full(FOURIER_WIDTH, D_MODEL), full(ATTN_WIDTH, D_MODEL), full(1, D_MODEL), full(D_MODEL, qw)],
        out_specs=[row(D_MODEL), row(D_MODEL), row(qw)],
        out_shape=[jax.ShapeDtypeStruct((N, D_MODEL), F32), jax.ShapeDtypeStruct((N, D_MODEL), BF16),
                   jax.ShapeDtypeStruct((N, qw), BF16)],
        compiler_params=_params(("parallel",)),
        name="mix",
    )(x, f_out, *os_, *lses, w_out_bf[:FOURIER_WIDTH], w_out_bf[FOURIER_WIDTH:], g2.reshape(1, D_MODEL), wq_bf)


def _argmax_rows(parts, ids_ascending):
    while len(parts) > 1:
        nxt = []
        for k in range(0, len(parts) - 1, 2):
            (va, ia), (vb, ib) = parts[k], parts[k + 1]
            take_b = (vb > va) if ids_ascending else ((vb > va) | ((vb == va) & (ib < ia)))
            nxt.append((jnp.where(take_b, vb, va), jnp.where(take_b, ib, ia)))
        if len(parts) % 2:
            nxt.append(parts[-1])
        parts = nxt
    v, i = parts[0]
    m = jnp.max(v, axis=0, keepdims=True)
    return m, jnp.min(jnp.where(v == m, i, ID_NONE), axis=0, keepdims=True)


def _extract_topk(s, id_parts, ids_ascending):
    tq = s.shape[1]
    parts = [(s[8 * k:8 * k + 8], ids) for k, ids in enumerate(id_parts)]
    rank = lax.broadcasted_iota(jnp.int32, (TOPK, tq), 0)
    vals = jnp.zeros((TOPK, tq), F32)
    idxs = jnp.zeros((TOPK, tq), F32)
    for kk in range(TOPK):
        m, first = _argmax_rows(parts, ids_ascending)
        parts = [(jnp.where(ids == first, -jnp.inf, v), ids) for v, ids in parts]
        vals = jnp.where(rank == kk, m, vals)
        idxs = jnp.where(rank == kk, first, idxs)
    return vals, idxs


def _select_rows(sel, table):
    out = jnp.zeros_like(sel)
    for a in range(TOPK):
        out = jnp.where(sel == float(a), table[a:a + 1, :], out)
    return out


def _candidate_ids():
    ids = [a * TOPK for a in range(TOPK)]
    for b in range(1, 8):
        ids += [a * TOPK + b if (a + 1) * (b + 1) <= TOPK else ID_NONE for a in range(8)]
    ids += list(range(8, TOPK))
    return np.asarray(ids, np.float32)


def _topk_body(qp_ref, keys_ref, ids_ref, i1_ref, i2_ref, g_ref):
    tq = qp_ref.shape[0]
    sub = lax.broadcasted_iota(jnp.int32, (8, tq), 0).astype(F32)
    key_ids = [sub + float(r) for r in range(0, N_KEYS, 8)]
    cand_ids = ids_ref[...]
    cand_id_parts = [cand_ids[r:r + 8] for r in range(0, cand_ids.shape[0], 8)]
    i1s, i2s, gs = [], [], []
    for h in range(PEER_HEADS):
        halves = []
        for p in range(2):
            j = h * 2 + p
            st = _dot_nt(keys_ref[j], qp_ref[:, j * N_KEYS:(j + 1) * N_KEYS])
            halves.append(_extract_topk(st, key_ids, True))
        (v1, x1), (v2, x2) = halves
        cand = jnp.concatenate([v1 + v2[0:1, :]] + [v1[0:8, :] + v2[b:b + 1, :] for b in range(1, 8)]
                               + [v1[0:1, :] + v2[8:TOPK, :]], axis=0)
        cand = jnp.where(cand_ids < ID_NONE, cand, -jnp.inf)
        top_s, flat = _extract_topk(cand, cand_id_parts, False)
        a_sel = jnp.floor(flat * (1.0 / TOPK))
        b_sel = flat - a_sel * TOPK
        i1s.append(_select_rows(a_sel, x1))
        i2s.append(_select_rows(b_sel, x2))
        e = jnp.exp(top_s - top_s[0:1, :])
        gs.append(e * (1.0 / jnp.sum(e, axis=0, keepdims=True)))
    i1_ref[...] = jnp.concatenate(i1s, axis=0).T.astype(jnp.int32)
    i2_ref[...] = jnp.concatenate(i2s, axis=0).T.astype(jnp.int32)
    g_ref[...] = jnp.concatenate(gs, axis=0).T


def _peer_topk(qp, keys_bf):
    N, qw = qp.shape
    tq = 256
    nsel = PEER_HEADS * TOPK
    ids = _candidate_ids()
    ids = jnp.asarray(np.broadcast_to(ids[:, None], (ids.shape[0], tq)))
    out = pl.BlockSpec((tq, nsel), lambda i: (i, 0))
    return pl.pallas_call(
        _topk_body,
        grid=(N // tq,),
        in_specs=[pl.BlockSpec((tq, qw), lambda i: (i, 0)),
                  pl.BlockSpec((2 * PEER_HEADS, N_KEYS, N_KEYS), lambda i: (0, 0, 0)),
                  pl.BlockSpec(ids.shape, lambda i: (0, 0))],
        out_specs=[out, out, out],
        out_shape=[jax.ShapeDtypeStruct((N, nsel), jnp.int32), jax.ShapeDtypeStruct((N, nsel), jnp.int32),
                   jax.ShapeDtypeStruct((N, nsel), F32)],
        compiler_params=_params(("parallel",)),
        name="peer_topk",
    )(qp, keys_bf, ids)


def _pack_bf16_pair(a, b):
    abits = lax.bitcast_convert_type(a.astype(BF16).astype(F32), jnp.uint32)
    bbits = lax.bitcast_convert_type(b.astype(BF16).astype(F32), jnp.uint32)
    return lax.bitcast_convert_type(abits | (bbits >> 16), jnp.int32)


def _unpack_bf16_pair(word):
    bits = lax.bitcast_convert_type(word, jnp.uint32)
    hi = lax.bitcast_convert_type(bits & jnp.uint32(0xFFFF0000), F32)
    lo = lax.bitcast_convert_type(bits << 16, F32)
    return hi, lo

def _peer_body(tq, nh, n_chunks, x1_ref, xn_ref, i1_ref, i2_ref, g_ref, ul_ref, uh_ref, vl_ref, vh_ref, gf_ref,
               o_ref, gate_ref, w_ref):
    c = pl.program_id(1)
    half_keys = N_KEYS // 2

    @pl.when(c == 0)
    def _():
        o_ref[...] = jnp.zeros_like(o_ref)
        row = lax.broadcasted_iota(jnp.int32, (N_KEYS, N_KEYS), 0)

        def build(blk, carry):
            base = pl.multiple_of(blk * 8, 8)
            i1b, i2b, gb = i1_ref[pl.ds(base, 8), :], i2_ref[pl.ds(base, 8), :], g_ref[pl.ds(base, 8), :]
            for r in range(8):
                p1 = jnp.where(i1b[r:r + 1, :] == row, gb[r:r + 1, :], 0.0).astype(BF16)
                p2 = jnp.where(i2b[r:r + 1, :] == row, 1.0, 0.0).astype(BF16)
                gt = _dot_nt(p1, p2)
                word = _pack_bf16_pair(gt[:half_keys], gt[half_keys:])
                gate_ref[pl.ds(pl.multiple_of((base + r) * G_PITCH, 8), half_keys), :] = word
            return carry

        lax.fori_loop(0, tq // 8, build, 0)

    def activate(slot):
        xn = xn_ref[...]
        gates = [_unpack_bf16_pair(gate_ref[pl.ds(c * nh + ii, tq, stride=G_PITCH), :])
                 for ii in range(nh)]
        for half, u_ref in enumerate((ul_ref, uh_ref)):
            a = _dot(xn, u_ref[...])
            act = 0.5 * a * (1.0 + lax.erf(a * float(np.sqrt(0.5))))
            for ii in range(nh):
                cols = slice(ii * N_KEYS, (ii + 1) * N_KEYS)
                w_ref[slot, half, :, cols] = (act[:, cols] * gates[ii][half]).astype(BF16)

    def accumulate(slot):
        o_ref[...] += _dot(w_ref[slot, 0], vl_ref[...]) + _dot(w_ref[slot, 1], vh_ref[...])

    last = pl.num_programs(1) - 1
    slot = lax.rem(c, 2)

    @pl.when(c == 0)
    def _():
        activate(0)

    @pl.when(jnp.logical_and(c > 0, c < last))
    def _():
        activate(slot)
        accumulate(1 - slot)

    @pl.when(c == last)
    def _():
        accumulate((n_chunks - 1) % 2)
        y = x1_ref[...] + o_ref[...]
        ms = jnp.mean(y * y, axis=-1, keepdims=True)
        o_ref[...] = y * lax.rsqrt(ms + EPS) * gf_ref[...]


def _peer(x1, xn, i1, i2, g, ut_bf, v_bf, gf):
    N = x1.shape[0]
    tq = 512
    nh = 4
    ec = nh * N_KEYS
    n_chunks = (N_KEYS // 2) // nh
    nsel = PEER_HEADS * TOPK
    tok = lambda w: pl.BlockSpec((tq, w), lambda i, c: (i, 0))
    u_chunk = lambda c: jnp.minimum(c, n_chunks - 1)
    v_chunk = lambda c: jnp.maximum(c - 1, 0)
    return pl.pallas_call(
        functools.partial(_peer_body, tq, nh, n_chunks),
        grid=(N // tq, n_chunks + 1),
        in_specs=[tok(D_MODEL), tok(D_MODEL), tok(nsel), tok(nsel), tok(nsel),
                  pl.BlockSpec((D_MODEL, ec), lambda i, c: (0, u_chunk(c))),
                  pl.BlockSpec((D_MODEL, ec), lambda i, c: (0, u_chunk(c) + n_chunks)),
                  pl.BlockSpec((ec, D_MODEL), lambda i, c: (v_chunk(c), 0)),
                  pl.BlockSpec((ec, D_MODEL), lambda i, c: (v_chunk(c) + n_chunks, 0)),
                  pl.BlockSpec((1, D_MODEL), lambda i, c: (0, 0))],
        out_specs=tok(D_MODEL),
        out_shape=jax.ShapeDtypeStruct((N, D_MODEL), F32),
        scratch_shapes=[pltpu.VMEM((tq * G_PITCH, N_KEYS), jnp.int32), pltpu.VMEM((2, 2, tq, ec), BF16)],
        compiler_params=_params(("parallel", "arbitrary")),
        name="peer",
    )(x1, xn, i1, i2, g, ut_bf, ut_bf, v_bf, v_bf, gf.reshape(1, D_MODEL))


def _encoder(x, w):
    B, S, _ = x.shape
    f_in, qkv1, qkv4, qkv16 = _proj(x, w["norm1_g"], w["w_in"])
    f_out = _fourier(f_in, w["w_fourier"])
    groups = [_attention_natural(*qkv1), _attention_dilated(*qkv4, DILATIONS[1], S),
              _attention_dilated(*qkv16, DILATIONS[2], S)]
    N = B * S
    flat = lambda a: a.reshape(N, a.shape[-1])
    x1, xn, qp = _mix(flat(x), flat(f_out), [flat(o) for o, _ in groups], [flat(l) for _, l in groups],
                      w["w_out"], w["norm2_g"], w["w_query"])
    i1, i2, g = _peer_topk(qp, w["sub_keys"])
    y = _peer(x1, xn, i1, i2, g, w["expert_u"], w["expert_v"], w["final_g"])
    return y.reshape(B, S, D_MODEL)


def kernel(x_prompt, x_sample, norm1_g, w_in, w_fourier, w_out, norm2_g, w_query, sub_keys, expert_u, expert_v, final_g):
    w = {
        "norm1_g": norm1_g[0], "w_in": w_in[0].astype(BF16), "w_fourier": w_fourier[0],
        "w_out": w_out[0].astype(BF16), "norm2_g": norm2_g[0], "w_query": w_query[0].astype(BF16),
        "sub_keys": sub_keys[0].reshape(2 * PEER_HEADS, N_KEYS, N_KEYS).astype(BF16),
        "expert_u": expert_u[0].astype(BF16).T, "expert_v": expert_v[0].astype(BF16), "final_g": final_g,
    }
    return (_encoder(x_prompt, w), _encoder(x_sample, w))
```

```python
import functools

import numpy as np
import jax
import jax.numpy as jnp
from jax import lax
from jax.experimental import pallas as pl
from jax.experimental.pallas import tpu as pltpu

F32 = jnp.float32
BF16 = jnp.bfloat16

D_MODEL = 1024
HEAD_DIM = 64
FOURIER_WIDTH = 512
ATTN_WIDTH = 512
IN_WIDTH = 2048
N_GROUPS = 8
DILATIONS = (1, 4, 16)
HALF_WINDOW = 64
ROPE_THETA = 500000.0
ROPE_DIM = 16
PEER_HEADS = 8
N_KEYS = 128
N_EXPERTS = N_KEYS * N_KEYS
TOPK = 16
EPS = 1e-6
NEG = -1e30

LANES = 128
DFT_INNER = 128
G_PITCH = 72
ID_NONE = 1e9
VMEM_LIMIT = 56 * 2**20
PEER_NH = 8


def _params(sem, vmem=None):
    return pltpu.CompilerParams(dimension_semantics=sem, vmem_limit_bytes=vmem or VMEM_LIMIT)


def _dot(a, b):
    return jnp.dot(a, b, preferred_element_type=F32)


def _dot_nt(a, b):
    return lax.dot_general(a, b, (((1,), (1,)), ((), ())), preferred_element_type=F32)


def _proj_body(tm, x_ref, g_ref, w_ref, c_ref, sa_ref, sb_ref,
               f_ref, q_ref, k_ref, v_ref, q4_ref, k4_ref, v4_ref, q16_ref, k16_ref, v16_ref, stage_ref):
    x = x_ref[...]
    ms = jnp.mean(x * x, axis=-1, keepdims=True)
    h = (x * lax.rsqrt(ms + EPS) * g_ref[...]).astype(BF16)
    p = _dot(h, w_ref[...])
    f_ref[...] = p[:, :FOURIER_WIDTH].astype(BF16)
    c, sa, sb = c_ref[...], sa_ref[...], sb_ref[...]
    for j in range(ATTN_WIDTH // LANES):
        lo = j * LANES
        qc = p[:, FOURIER_WIDTH + lo:FOURIER_WIDTH + lo + LANES]
        kc = p[:, FOURIER_WIDTH + ATTN_WIDTH + lo:FOURIER_WIDTH + ATTN_WIDTH + lo + LANES]
        vc = p[:, FOURIER_WIDTH + 2 * ATTN_WIDTH + lo:FOURIER_WIDTH + 2 * ATTN_WIDTH + lo + LANES]
        qr = (qc * c + pltpu.roll(qc, LANES - 8, 1) * sa + pltpu.roll(qc, 8, 1) * sb) * (HEAD_DIM ** -0.5)
        kr = kc * c + pltpu.roll(kc, LANES - 8, 1) * sa + pltpu.roll(kc, 8, 1) * sb
        for a, (val, nat) in enumerate(((qr, q_ref), (kr, k_ref), (vc, v_ref))):
            nat[:, lo:lo + LANES] = val.astype(BF16)
            stage_ref[a, j] = val
    for a, dil in enumerate(((q4_ref, q16_ref), (k4_ref, k16_ref), (v4_ref, v16_ref))):
        for R, d_ref in zip(DILATIONS[1:], dil):
            for j in range(ATTN_WIDTH // LANES):
                for r in range(R):
                    piece = stage_ref[a, j, pl.ds(r, tm // R, stride=R), :]
                    d_ref[:, (j * R + r) * LANES:(j * R + r + 1) * LANES] = piece.astype(BF16)


def _rope_tables(S):
    half = ROPE_DIM // 2
    inv = ROPE_THETA ** (-(jnp.arange(half, dtype=F32) * 2.0) / ROPE_DIM)
    ang = jnp.arange(S, dtype=F32)[:, None] * inv[None, :]
    cos, sin = jnp.cos(ang), jnp.sin(ang)
    one = jnp.ones((S, HEAD_DIM - ROPE_DIM), F32)
    zero = jnp.zeros((S, HEAD_DIM - ROPE_DIM), F32)
    z8 = jnp.zeros((S, half), F32)
    c = jnp.concatenate([cos, cos, one], axis=1)
    sa = jnp.concatenate([-sin, z8, zero], axis=1)
    sb = jnp.concatenate([z8, sin, zero], axis=1)
    rep = LANES // HEAD_DIM
    return jnp.tile(c, (1, rep)), jnp.tile(sa, (1, rep)), jnp.tile(sb, (1, rep))


def _proj(x, g, w_bf):
    B, S, _ = x.shape
    tm = min(512, S)
    c, sa, sb = _rope_tables(S)
    tab = pl.BlockSpec((tm, LANES), lambda b, i: (i, 0))
    out = pl.BlockSpec((None, tm, ATTN_WIDTH), lambda b, i: (b, i, 0))
    shp = jax.ShapeDtypeStruct((B, S, ATTN_WIDTH), BF16)
    dil_specs, dil_shapes = [], []
    for R in DILATIONS[1:]:
        dil_specs.append(pl.BlockSpec((None, tm // R, R * ATTN_WIDTH), lambda b, i: (b, i, 0)))
        dil_shapes.append(jax.ShapeDtypeStruct((B, S // R, R * ATTN_WIDTH), BF16))
    outs = pl.pallas_call(
        functools.partial(_proj_body, tm),
        grid=(B, S // tm),
        in_specs=[pl.BlockSpec((None, tm, D_MODEL), lambda b, i: (b, i, 0)),
                  pl.BlockSpec((1, D_MODEL), lambda b, i: (0, 0)),
                  pl.BlockSpec((D_MODEL, IN_WIDTH), lambda b, i: (0, 0)),
                  tab, tab, tab],
        out_specs=[out] * 4 + [dil_specs[0]] * 3 + [dil_specs[1]] * 3,
        out_shape=[shp] * 4 + [dil_shapes[0]] * 3 + [dil_shapes[1]] * 3,
        scratch_shapes=[pltpu.VMEM((3, ATTN_WIDTH // LANES, tm, LANES), F32)],
        compiler_params=_params(("parallel", "parallel")),
        name="proj",
    )(x, g.reshape(1, D_MODEL), w_bf, c, sa, sb)
    return outs[0], outs[1:4], outs[4:7], outs[7:10]


def _cos_sin(n_rows, n_cols, period):
    i = jnp.arange(n_rows, dtype=jnp.int32)[:, None]
    j = jnp.arange(n_cols, dtype=jnp.int32)[None, :]
    ang = ((i * j) % period).astype(F32) * (2.0 * np.pi / period)
    return jnp.cos(ang), jnp.sin(ang)


def _dft_a_body(n1, d_ref, x_ref, yr_ref, yi_ref):
    y = _dot(d_ref[...], x_ref[...])
    yr_ref[...] = y[:n1].astype(BF16)
    yi_ref[...] = y[n1:].astype(BF16)


def _dft_b_body(ct, scale, yr_ref, yi_ref, tc_ref, ts_ref, d2_ref, c64_ref, s64_ref, wf_ref, o_ref):
    rep = FOURIER_WIDTH // LANES
    for cc in range(ct):
        yr = yr_ref[cc].astype(F32)
        yi = yi_ref[cc].astype(F32)
        tc = jnp.concatenate([tc_ref[cc]] * rep, axis=1)
        ts = jnp.concatenate([ts_ref[cc]] * rep, axis=1)
        zr = yr * tc + yi * ts
        zi = yi * tc - yr * ts
        z = jnp.concatenate([zr, zi], axis=0).astype(BF16)
        xx = _dot(d2_ref[...], z)
        xr = xx[:DFT_INNER].astype(BF16)
        xi = xx[DFT_INNER:].astype(BF16)
        re = (_dot(xr, c64_ref[...]) + _dot(xi, s64_ref[...])) * scale
        o_ref[cc] = _dot(re.astype(BF16), wf_ref[...]).astype(BF16)


def _block_diag(m):
    G, n, _ = m.shape
    eye = jnp.eye(G, dtype=m.dtype)
    return (eye[:, None, :, None] * m[:, :, None, :]).reshape(G * n, G * n)


def _fourier(f_in, w_fourier):
    B, S, W = f_in.shape
    n2 = DFT_INNER
    n1 = S // n2
    cols = n2 * W
    tcw = 4096
    c1, s1 = _cos_sin(n1, n1, n1)
    d1 = jnp.concatenate([c1, -s1], axis=0).astype(BF16)
    xa = f_in.reshape(B, n1, cols)
    blk = pl.BlockSpec((None, n1, tcw), lambda b, j: (b, 0, j))
    yshape = jax.ShapeDtypeStruct((B, n1, cols), BF16)
    yr, yi = pl.pallas_call(
        functools.partial(_dft_a_body, n1),
        grid=(B, cols // tcw),
        in_specs=[pl.BlockSpec((2 * n1, n1), lambda b, j: (0, 0)), blk],
        out_specs=[blk, blk],
        out_shape=[yshape, yshape],
        compiler_params=_params(("parallel", "parallel")),
        name="dft_a",
    )(d1, xa)

    yr = yr.reshape(B, n1, n2, W)
    yi = yi.reshape(B, n1, n2, W)
    twc, tws = _cos_sin(n1, n2, S)
    twc = jnp.broadcast_to(twc[:, :, None], (n1, n2, LANES))
    tws = jnp.broadcast_to(tws[:, :, None], (n1, n2, LANES))
    c2, s2 = _cos_sin(n2, n2, n2)
    d2 = jnp.concatenate([jnp.concatenate([c2, s2], axis=1),
                          jnp.concatenate([-s2, c2], axis=1)], axis=0).astype(BF16)
    cg, sg = _cos_sin(HEAD_DIM, HEAD_DIM, HEAD_DIM)
    c64 = _block_diag(jnp.broadcast_to(cg, (N_GROUPS, HEAD_DIM, HEAD_DIM))).astype(BF16)
    s64 = _block_diag(jnp.broadcast_to(sg, (N_GROUPS, HEAD_DIM, HEAD_DIM))).astype(BF16)
    wf = _block_diag(w_fourier).astype(BF16)
    ct = 8
    scale = float(1.0 / np.sqrt(S * HEAD_DIM))
    yblk = pl.BlockSpec((None, ct, n2, W), lambda b, i: (b, i, 0, 0))
    tblk = pl.BlockSpec((ct, n2, LANES), lambda b, i: (i, 0, 0))
    full = lambda r, c: pl.BlockSpec((r, c), lambda b, i: (0, 0))
    out = pl.pallas_call(
        functools.partial(_dft_b_body, ct, scale),
        grid=(B, n1 // ct),
        in_specs=[yblk, yblk, tblk, tblk, full(2 * n2, 2 * n2), full(W, W), full(W, W), full(W, W)],
        out_specs=yblk,
        out_shape=jax.ShapeDtypeStruct((B, n1, n2, W), BF16),
        compiler_params=_params(("parallel", "parallel")),
        name="dft_b",
    )(yr, yi, twc, tws, d2, c64, s64, wf)
    return out.transpose(0, 2, 1, 3).reshape(B, S, W)


def _band_attend(qf, ks, vs, valid, lo_half):
    outs, lses = [], []
    for hh in range(2):
        mine = lo_half if hh == 0 else jnp.logical_not(lo_half)
        qm = jnp.where(mine, qf, 0.0).astype(BF16)
        s = jnp.where(valid, _dot_nt(qm, ks), NEG)
        m = jnp.max(s, axis=1, keepdims=True)
        p = jnp.exp(s - m)
        l = jnp.sum(p, axis=1, keepdims=True)
        outs.append(_dot(p.astype(BF16), vs) * (1.0 / l))
        lses.append(m + jnp.log(l))
    return jnp.where(lo_half, outs[0], outs[1]), jnp.where(lo_half, lses[0], lses[1])


def _attn_masks():
    sub, win = 128, 128 + 2 * HALF_WINDOW
    lane = lax.broadcasted_iota(jnp.int32, (sub, LANES), 1)
    r = lax.broadcasted_iota(jnp.int32, (sub, win), 0)
    c = lax.broadcasted_iota(jnp.int32, (sub, win), 1)
    return lane < HEAD_DIM, (c >= r) & (c <= r + 2 * HALF_WINDOW), c


def _attn_body(tq, L, q_ref, kp_ref, kc_ref, kn_ref, vp_ref, vc_ref, vn_ref, o_ref, lse_ref):
    i = pl.program_id(1)
    sub, win = 128, 128 + 2 * HALF_WINDOW
    lo_half, band, c = _attn_masks()
    for pair in range(ATTN_WIDTH // LANES):
        sl = slice(pair * LANES, (pair + 1) * LANES)
        k2 = jnp.concatenate([kp_ref[:, sl], kc_ref[:, sl], kn_ref[:, sl]], axis=0)
        v2 = jnp.concatenate([vp_ref[:, sl], vc_ref[:, sl], vn_ref[:, sl]], axis=0)
        for sb in range(tq // sub):
            rows = slice(sb * sub, (sb + 1) * sub)
            kpos = i * tq + sb * sub - HALF_WINDOW + c
            valid = band & (kpos >= 0) & (kpos < L)
            o, lse = _band_attend(q_ref[rows, sl].astype(F32), k2[sb * sub:sb * sub + win],
                                  v2[sb * sub:sb * sub + win], valid, lo_half)
            o_ref[rows, sl] = o
            lse_ref[rows, sl] = lse


def _attn_dil_body(R, tq, L, q_ref, kp_ref, kc_ref, kn_ref, vp_ref, vc_ref, vn_ref, o_ref, lse_ref):
    i = pl.program_id(1)
    sub, win = 128, 128 + 2 * HALF_WINDOW
    lo_half, band, c = _attn_masks()
    for r in range(R):
        sl = slice(r * LANES, (r + 1) * LANES)
        k2 = jnp.concatenate([kp_ref[:, sl], kc_ref[:, sl], kn_ref[:, sl]], axis=0)
        v2 = jnp.concatenate([vp_ref[:, sl], vc_ref[:, sl], vn_ref[:, sl]], axis=0)
        for sb in range(tq // sub):
            kpos = i * tq + sb * sub - HALF_WINDOW + c
            valid = band & (kpos >= 0) & (kpos < L)
            o, lse = _band_attend(q_ref[sb * sub:(sb + 1) * sub, sl].astype(F32), k2[sb * sub:sb * sub + win],
                                  v2[sb * sub:sb * sub + win], valid, lo_half)
            dst = pl.ds(R * sb * sub + r, sub, stride=R)
            o_ref[dst, :] = o
            lse_ref[dst, :] = lse


def _attention_natural(q, k, v):
    B, S, W = q.shape
    tq = min(256, S)
    hb = tq // HALF_WINDOW
    nhalo = S // HALF_WINDOW
    ctr = pl.BlockSpec((None, tq, W), lambda b, i: (b, i, 0))
    prev = pl.BlockSpec((None, HALF_WINDOW, W), lambda b, i: (b, jnp.maximum(i * hb - 1, 0), 0))
    nxt = pl.BlockSpec((None, HALF_WINDOW, W), lambda b, i: (b, jnp.minimum((i + 1) * hb, nhalo - 1), 0))
    shp = jax.ShapeDtypeStruct((B, S, W), F32)
    return pl.pallas_call(
        functools.partial(_attn_body, tq, S),
        grid=(B, S // tq),
        in_specs=[ctr, prev, ctr, nxt, prev, ctr, nxt],
        out_specs=[ctr, ctr],
        out_shape=[shp, shp],
        compiler_params=_params(("parallel", "parallel")),
        name="attn_d1",
    )(q, k, k, k, v, v, v)


def _attention_dilated(q, k, v, R, S):
    B, L, _ = q.shape
    tq = min(2048 // R, L)
    hb = tq // HALF_WINDOW
    nhalo = L // HALF_WINDOW
    wb = R * LANES
    ctr = pl.BlockSpec((None, tq, wb), lambda b, i, p: (b, i, p))
    prev = pl.BlockSpec((None, HALF_WINDOW, wb), lambda b, i, p: (b, jnp.maximum(i * hb - 1, 0), p))
    nxt = pl.BlockSpec((None, HALF_WINDOW, wb), lambda b, i, p: (b, jnp.minimum((i + 1) * hb, nhalo - 1), p))
    out = pl.BlockSpec((None, R * tq, LANES), lambda b, i, p: (b, i, p))
    shp = jax.ShapeDtypeStruct((B, S, ATTN_WIDTH), F32)
    return pl.pallas_call(
        functools.partial(_attn_dil_body, R, tq, L),
        grid=(B, L // tq, ATTN_WIDTH // LANES),
        in_specs=[ctr, prev, ctr, nxt, prev, ctr, nxt],
        out_specs=[out, out],
        out_shape=[shp, shp],
        compiler_params=_params(("parallel", "parallel", "parallel")),
        name=f"attn_d{R}",
    )(q, k, k, k, v, v, v)


def _mix_body(x_ref, f_ref, o1_ref, o2_ref, o3_ref, l1_ref, l2_ref, l3_ref, wf_ref, wa_ref, g_ref, wq_ref,
              x1_ref, xn_ref, qp_ref):
    l1, l2, l3 = l1_ref[...], l2_ref[...], l3_ref[...]
    mx = jnp.maximum(jnp.maximum(l1, l2), l3)
    e1, e2, e3 = jnp.exp(l1 - mx), jnp.exp(l2 - mx), jnp.exp(l3 - mx)
    a = (e1 * o1_ref[...] + e2 * o2_ref[...] + e3 * o3_ref[...]) * (1.0 / (e1 + e2 + e3))
    x1 = x_ref[...] + _dot(f_ref[...], wf_ref[...]) + _dot(a.astype(BF16), wa_ref[...])
    x1_ref[...] = x1
    ms = jnp.mean(x1 * x1, axis=-1, keepdims=True)
    xn = (x1 * lax.rsqrt(ms + EPS) * g_ref[...]).astype(BF16)
    xn_ref[...] = xn
    qp_ref[...] = _dot(xn, wq_ref[...]).astype(BF16)


def _mix(x, f_out, os_, lses, w_out_bf, g2, wq_bf):
    N = x.shape[0]
    tm = 256
    row = lambda w: pl.BlockSpec((tm, w), lambda i: (i, 0))
    full = lambda r, c: pl.BlockSpec((r, c), lambda i: (0, 0))
    qw = wq_bf.shape[1]
    return pl.pallas_call(
        _mix_body,
        grid=(N // tm,),
        in_specs=[row(D_MODEL), row(FOURIER_WIDTH)] + [row(ATTN_WIDTH)] * 6
                 + [full(FOURIER_WIDTH, D_MODEL), full(ATTN_WIDTH, D_MODEL), full(1, D_MODEL), full(D_MODEL, qw)],
        out_specs=[row(D_MODEL), row(D_MODEL), row(qw)],
        out_shape=[jax.ShapeDtypeStruct((N, D_MODEL), F32), jax.ShapeDtypeStruct((N, D_MODEL), BF16),
                   jax.ShapeDtypeStruct((N, qw), BF16)],
        compiler_params=_params(("parallel",)),
        name="mix",
    )(x, f_out, *os_, *lses, w_out_bf[:FOURIER_WIDTH], w_out_bf[FOURIER_WIDTH:], g2.reshape(1, D_MODEL), wq_bf)


def _argmax_rows(parts, ids_ascending):
    while len(parts) > 1:
        nxt = []
        for k in range(0, len(parts) - 1, 2):
            (va, ia), (vb, ib) = parts[k], parts[k + 1]
            take_b = (vb > va) if ids_ascending else ((vb > va) | ((vb == va) & (ib < ia)))
            nxt.append((jnp.where(take_b, vb, va), jnp.where(take_b, ib, ia)))
        if len(parts) % 2:
            nxt.append(parts[-1])
        parts = nxt
    v, i = parts[0]
    m = jnp.max(v, axis=0, keepdims=True)
    return m, jnp.min(jnp.where(v == m, i, ID_NONE), axis=0, keepdims=True)


def _extract_topk(s, id_parts, ids_ascending):
    tq = s.shape[1]
    parts = [(s[8 * k:8 * k + 8], ids) for k, ids in enumerate(id_parts)]
    rank = lax.broadcasted_iota(jnp.int32, (TOPK, tq), 0)
    vals = jnp.zeros((TOPK, tq), F32)
    idxs = jnp.zeros((TOPK, tq), F32)
    for kk in range(TOPK):
        m, first = _argmax_rows(parts, ids_ascending)
        parts = [(jnp.where(ids == first, -jnp.inf, v), ids) for v, ids in parts]
        vals = jnp.where(rank == kk, m, vals)
        idxs = jnp.where(rank == kk, first, idxs)
    return vals, idxs


def _select_rows(sel, table):
    out = jnp.zeros_like(sel)
    for a in range(TOPK):
        out = jnp.where(sel == float(a), table[a:a + 1, :], out)
    return out


def _candidate_ids():
    ids = [a * TOPK for a in range(TOPK)]
    for b in range(1, 8):
        ids += [a * TOPK + b if (a + 1) * (b + 1) <= TOPK else ID_NONE for a in range(8)]
    ids += list(range(8, TOPK))
    return np.asarray(ids, np.float32)


def _topk_body(qp_ref, keys_ref, ids_ref, i1_ref, i2_ref, g_ref):
    tq = qp_ref.shape[0]
    sub = lax.broadcasted_iota(jnp.int32, (8, tq), 0).astype(F32)
    key_ids = [sub + float(r) for r in range(0, N_KEYS, 8)]
    cand_ids = ids_ref[...]
    cand_id_parts = [cand_ids[r:r + 8] for r in range(0, cand_ids.shape[0], 8)]
    i1s, i2s, gs = [], [], []
    for h in range(PEER_HEADS):
        halves = []
        for p in range(2):
            j = h * 2 + p
            st = _dot_nt(keys_ref[j], qp_ref[:, j * N_KEYS:(j + 1) * N_KEYS])
            halves.append(_extract_topk(st, key_ids, True))
        (v1, x1), (v2, x2) = halves
        cand = jnp.concatenate([v1 + v2[0:1, :]] + [v1[0:8, :] + v2[b:b + 1, :] for b in range(1, 8)]
                               + [v1[0:1, :] + v2[8:TOPK, :]], axis=0)
        cand = jnp.where(cand_ids < ID_NONE, cand, -jnp.inf)
        top_s, flat = _extract_topk(cand, cand_id_parts, False)
        a_sel = jnp.floor(flat * (1.0 / TOPK))
        b_sel = flat - a_sel * TOPK
        i1s.append(_select_rows(a_sel, x1))
        i2s.append(_select_rows(b_sel, x2))
        e = jnp.exp(top_s - top_s[0:1, :])
        gs.append(e * (1.0 / jnp.sum(e, axis=0, keepdims=True)))
    i1_ref[...] = jnp.concatenate(i1s, axis=0).T.astype(jnp.int32)
    i2_ref[...] = jnp.concatenate(i2s, axis=0).T.astype(jnp.int32)
    g_ref[...] = jnp.concatenate(gs, axis=0).T


def _peer_topk(qp, keys_bf):
    N, qw = qp.shape
    tq = 256
    nsel = PEER_HEADS * TOPK
    ids = _candidate_ids()
    ids = jnp.asarray(np.broadcast_to(ids[:, None], (ids.shape[0], tq)))
    out = pl.BlockSpec((tq, nsel), lambda i: (i, 0))
    return pl.pallas_call(
        _topk_body,
        grid=(N // tq,),
        in_specs=[pl.BlockSpec((tq, qw), lambda i: (i, 0)),
                  pl.BlockSpec((2 * PEER_HEADS, N_KEYS, N_KEYS), lambda i: (0, 0, 0)),
                  pl.BlockSpec(ids.shape, lambda i: (0, 0))],
        out_specs=[out, out, out],
        out_shape=[jax.ShapeDtypeStruct((N, nsel), jnp.int32), jax.ShapeDtypeStruct((N, nsel), jnp.int32),
                   jax.ShapeDtypeStruct((N, nsel), F32)],
        compiler_params=_params(("parallel",)),
        name="peer_topk",
    )(qp, keys_bf, ids)


def _pack_bf16_pair(a, b):
    abits = lax.bitcast_convert_type(a.astype(BF16).astype(F32), jnp.uint32)
    bbits = lax.bitcast_convert_type(b.astype(BF16).astype(F32), jnp.uint32)
    return lax.bitcast_convert_type(abits | (bbits >> 16), jnp.int32)


def _unpack_bf16_pair(word):
    bits = lax.bitcast_convert_type(word, jnp.uint32)
    hi = lax.bitcast_convert_type(bits & jnp.uint32(0xFFFF0000), F32)
    lo = lax.bitcast_convert_type(bits << 16, F32)
    return hi, lo

def _peer_body(tq, nh, x1_ref, xn_ref, i1_ref, i2_ref, g_ref, ul_ref, uh_ref, vl_ref, vh_ref, gf_ref,
               o_ref, gate_ref):
    c = pl.program_id(1)
    half_keys = N_KEYS // 2

    @pl.when(c == 0)
    def _():
        o_ref[...] = jnp.zeros_like(o_ref)
        row = lax.broadcasted_iota(jnp.int32, (N_KEYS, N_KEYS), 0)

        def build(blk, carry):
            base = pl.multiple_of(blk * 8, 8)
            i1b, i2b, gb = i1_ref[pl.ds(base, 8), :], i2_ref[pl.ds(base, 8), :], g_ref[pl.ds(base, 8), :]
            for r in range(8):
                p1 = jnp.where(i1b[r:r + 1, :] == row, gb[r:r + 1, :], 0.0).astype(BF16)
                p2 = jnp.where(i2b[r:r + 1, :] == row, 1.0, 0.0).astype(BF16)
                gt = _dot_nt(p1, p2)
                word = _pack_bf16_pair(gt[:half_keys], gt[half_keys:])
                gate_ref[pl.ds(pl.multiple_of((base + r) * G_PITCH, 8), half_keys), :] = word
            return carry

        lax.fori_loop(0, tq // 8, build, 0)

    xn = xn_ref[...]
    gates = [_unpack_bf16_pair(gate_ref[pl.ds(c * nh + ii, tq, stride=G_PITCH), :])
             for ii in range(nh)]
    contrib = []
    for half, (u_ref, v_ref) in enumerate(((ul_ref, vl_ref), (uh_ref, vh_ref))):
        a = _dot(xn, u_ref[...])
        act = 0.5 * a * (1.0 + lax.erf(a * float(np.sqrt(0.5))))
        w = jnp.concatenate([(act[:, ii * N_KEYS:(ii + 1) * N_KEYS] * gates[ii][half]).astype(BF16)
                             for ii in range(nh)], axis=1)
        contrib.append(_dot(w, v_ref[...]))
    o_ref[...] += contrib[0] + contrib[1]

    @pl.when(c == pl.num_programs(1) - 1)
    def _():
        y = x1_ref[...] + o_ref[...]
        ms = jnp.mean(y * y, axis=-1, keepdims=True)
        o_ref[...] = y * lax.rsqrt(ms + EPS) * gf_ref[...]


def _peer(x1, xn, i1, i2, g, ut_bf, v_bf, gf):
    N = x1.shape[0]
    tq = 512
    nh = PEER_NH
    ec = nh * N_KEYS
    n_chunks = (N_KEYS // 2) // nh
    nsel = PEER_HEADS * TOPK
    tok = lambda w: pl.BlockSpec((tq, w), lambda i, c: (i, 0), pipeline_mode=pl.Buffered(1))
    return pl.pallas_call(
        functools.partial(_peer_body, tq, nh),
        grid=(N // tq, n_chunks),
        in_specs=[tok(D_MODEL), tok(D_MODEL), tok(nsel), tok(nsel), tok(nsel),
                  pl.BlockSpec((D_MODEL, ec), lambda i, c: (0, c)),
                  pl.BlockSpec((D_MODEL, ec), lambda i, c: (0, c + n_chunks)),
                  pl.BlockSpec((ec, D_MODEL), lambda i, c: (c, 0)),
                  pl.BlockSpec((ec, D_MODEL), lambda i, c: (c + n_chunks, 0)),
                  pl.BlockSpec((1, D_MODEL), lambda i, c: (0, 0))],
        out_specs=pl.BlockSpec((tq, D_MODEL), lambda i, c: (i, 0)),
        out_shape=jax.ShapeDtypeStruct((N, D_MODEL), F32),
        scratch_shapes=[pltpu.VMEM((tq * G_PITCH, N_KEYS), jnp.int32)],
        compiler_params=_params(("parallel", "arbitrary")),
        name="peer",
    )(x1, xn, i1, i2, g, ut_bf, ut_bf, v_bf, v_bf, gf.reshape(1, D_MODEL))


def _encoder(x, w):
    B, S, _ = x.shape
    f_in, qkv1, qkv4, qkv16 = _proj(x, w["norm1_g"], w["w_in"])
    f_out = _fourier(f_in, w["w_fourier"])
    groups = [_attention_natural(*qkv1), _attention_dilated(*qkv4, DILATIONS[1], S),
              _attention_dilated(*qkv16, DILATIONS[2], S)]
    N = B * S
    flat = lambda a: a.reshape(N, a.shape[-1])
    x1, xn, qp = _mix(flat(x), flat(f_out), [flat(o) for o, _ in groups], [flat(l) for _, l in groups],
                      w["w_out"], w["norm2_g"], w["w_query"])
    i1, i2, g = _peer_topk(qp, w["sub_keys"])
    y = _peer(x1, xn, i1, i2, g, w["expert_u"], w["expert_v"], w["final_g"])
    return y.reshape(B, S, D_MODEL)


def kernel(x_prompt, x_sample, norm1_g, w_in, w_fourier, w_out, norm2_g, w_query, sub_keys, expert_u, expert_v, final_g):
    w = {
        "norm1_g": norm1_g[0], "w_in": w_in[0].astype(BF16), "w_fourier": w_fourier[0],
        "w_out": w_out[0].astype(BF16), "norm2_g": norm2_g[0], "w_query": w_query[0].astype(BF16),
        "sub_keys": sub_keys[0].reshape(2 * PEER_HEADS, N_KEYS, N_KEYS).astype(BF16),
        "expert_u": expert_u[0].astype(BF16).T, "expert_v": expert_v[0].astype(BF16), "final_g": final_g,
    }
    return (_encoder(x_prompt, w), _encoder(x_sample, w))
```

```python
import functools

import numpy as np
import jax
import jax.numpy as jnp
from jax import lax
from jax.experimental import pallas as pl
from jax.experimental.pallas import tpu as pltpu

F32 = jnp.float32
BF16 = jnp.bfloat16

D_MODEL = 1024
HEAD_DIM = 64
FOURIER_WIDTH = 512
ATTN_WIDTH = 512
IN_WIDTH = 2048
N_GROUPS = 8
DILATIONS = (1, 4, 16)
HALF_WINDOW = 64
ROPE_THETA = 500000.0
ROPE_DIM = 16
PEER_HEADS = 8
N_KEYS = 128
N_EXPERTS = N_KEYS * N_KEYS
TOPK = 16
EPS = 1e-6
NEG = -1e30

LANES = 128
DFT_INNER = 128
G_PITCH = 72
ID_NONE = 1e9
VMEM_LIMIT = 56 * 2**20
PEER_NH = 8


def _params(sem, vmem=None):
    return pltpu.CompilerParams(dimension_semantics=sem, vmem_limit_bytes=vmem or VMEM_LIMIT)


def _dot(a, b):
    return jnp.dot(a, b, preferred_element_type=F32)


def _dot_nt(a, b):
    return lax.dot_general(a, b, (((1,), (1,)), ((), ())), preferred_element_type=F32)


def _proj_body(tm, x_ref, g_ref, w_ref, c_ref, sa_ref, sb_ref,
               f_ref, q_ref, k_ref, v_ref, q4_ref, k4_ref, v4_ref, q16_ref, k16_ref, v16_ref, stage_ref):
    x = x_ref[...]
    ms = jnp.mean(x * x, axis=-1, keepdims=True)
    h = (x * lax.rsqrt(ms + EPS) * g_ref[...]).astype(BF16)
    p = _dot(h, w_ref[...])
    f_ref[...] = p[:, :FOURIER_WIDTH].astype(BF16)
    c, sa, sb = c_ref[...], sa_ref[...], sb_ref[...]
    for j in range(ATTN_WIDTH // LANES):
        lo = j * LANES
        qc = p[:, FOURIER_WIDTH + lo:FOURIER_WIDTH + lo + LANES]
        kc = p[:, FOURIER_WIDTH + ATTN_WIDTH + lo:FOURIER_WIDTH + ATTN_WIDTH + lo + LANES]
        vc = p[:, FOURIER_WIDTH + 2 * ATTN_WIDTH + lo:FOURIER_WIDTH + 2 * ATTN_WIDTH + lo + LANES]
        qr = (qc * c + pltpu.roll(qc, LANES - 8, 1) * sa + pltpu.roll(qc, 8, 1) * sb) * (HEAD_DIM ** -0.5)
        kr = kc * c + pltpu.roll(kc, LANES - 8, 1) * sa + pltpu.roll(kc, 8, 1) * sb
        for a, (val, nat) in enumerate(((qr, q_ref), (kr, k_ref), (vc, v_ref))):
            nat[:, lo:lo + LANES] = val.astype(BF16)
            stage_ref[a, j] = val
    for a, dil in enumerate(((q4_ref, q16_ref), (k4_ref, k16_ref), (v4_ref, v16_ref))):
        for R, d_ref in zip(DILATIONS[1:], dil):
            for j in range(ATTN_WIDTH // LANES):
                for r in range(R):
                    piece = stage_ref[a, j, pl.ds(r, tm // R, stride=R), :]
                    d_ref[:, (j * R + r) * LANES:(j * R + r + 1) * LANES] = piece.astype(BF16)


def _rope_tables(S):
    half = ROPE_DIM // 2
    inv = ROPE_THETA ** (-(jnp.arange(half, dtype=F32) * 2.0) / ROPE_DIM)
    ang = jnp.arange(S, dtype=F32)[:, None] * inv[None, :]
    cos, sin = jnp.cos(ang), jnp.sin(ang)
    one = jnp.ones((S, HEAD_DIM - ROPE_DIM), F32)
    zero = jnp.zeros((S, HEAD_DIM - ROPE_DIM), F32)
    z8 = jnp.zeros((S, half), F32)
    c = jnp.concatenate([cos, cos, one], axis=1)
    sa = jnp.concatenate([-sin, z8, zero], axis=1)
    sb = jnp.concatenate([z8, sin, zero], axis=1)
    rep = LANES // HEAD_DIM
    return jnp.tile(c, (1, rep)), jnp.tile(sa, (1, rep)), jnp.tile(sb, (1, rep))


def _proj(x, g, w_bf):
    B, S, _ = x.shape
    tm = min(512, S)
    c, sa, sb = _rope_tables(S)
    tab = pl.BlockSpec((tm, LANES), lambda b, i: (i, 0))
    out = pl.BlockSpec((None, tm, ATTN_WIDTH), lambda b, i: (b, i, 0))
    shp = jax.ShapeDtypeStruct((B, S, ATTN_WIDTH), BF16)
    dil_specs, dil_shapes = [], []
    for R in DILATIONS[1:]:
        dil_specs.append(pl.BlockSpec((None, tm // R, R * ATTN_WIDTH), lambda b, i: (b, i, 0)))
        dil_shapes.append(jax.ShapeDtypeStruct((B, S // R, R * ATTN_WIDTH), BF16))
    outs = pl.pallas_call(
        functools.partial(_proj_body, tm),
        grid=(B, S // tm),
        in_specs=[pl.BlockSpec((None, tm, D_MODEL), lambda b, i: (b, i, 0)),
                  pl.BlockSpec((1, D_MODEL), lambda b, i: (0, 0)),
                  pl.BlockSpec((D_MODEL, IN_WIDTH), lambda b, i: (0, 0)),
                  tab, tab, tab],
        out_specs=[out] * 4 + [dil_specs[0]] * 3 + [dil_specs[1]] * 3,
        out_shape=[shp] * 4 + [dil_shapes[0]] * 3 + [dil_shapes[1]] * 3,
        scratch_shapes=[pltpu.VMEM((3, ATTN_WIDTH // LANES, tm, LANES), F32)],
        compiler_params=_params(("parallel", "parallel")),
        name="proj",
    )(x, g.reshape(1, D_MODEL), w_bf, c, sa, sb)
    return outs[0], outs[1:4], outs[4:7], outs[7:10]


def _cos_sin(n_rows, n_cols, period):
    i = jnp.arange(n_rows, dtype=jnp.int32)[:, None]
    j = jnp.arange(n_cols, dtype=jnp.int32)[None, :]
    ang = ((i * j) % period).astype(F32) * (2.0 * np.pi / period)
    return jnp.cos(ang), jnp.sin(ang)


def _dft_a_body(n1, d_ref, x_ref, yr_ref, yi_ref):
    y = _dot(d_ref[...], x_ref[...])
    yr_ref[...] = y[:n1].astype(BF16)
    yi_ref[...] = y[n1:].astype(BF16)


def _dft_b_body(ct, scale, yr_ref, yi_ref, tc_ref, ts_ref, d2_ref, c64_ref, s64_ref, wf_ref, o_ref):
    rep = FOURIER_WIDTH // LANES
    for cc in range(ct):
        yr = yr_ref[cc].astype(F32)
        yi = yi_ref[cc].astype(F32)
        tc = jnp.concatenate([tc_ref[cc]] * rep, axis=1)
        ts = jnp.concatenate([ts_ref[cc]] * rep, axis=1)
        zr = yr * tc + yi * ts
        zi = yi * tc - yr * ts
        z = jnp.concatenate([zr, zi], axis=0).astype(BF16)
        xx = _dot(d2_ref[...], z)
        xr = xx[:DFT_INNER].astype(BF16)
        xi = xx[DFT_INNER:].astype(BF16)
        re = (_dot(xr, c64_ref[...]) + _dot(xi, s64_ref[...])) * scale
        o_ref[cc] = _dot(re.astype(BF16), wf_ref[...]).astype(BF16)


def _block_diag(m):
    G, n, _ = m.shape
    eye = jnp.eye(G, dtype=m.dtype)
    return (eye[:, None, :, None] * m[:, :, None, :]).reshape(G * n, G * n)


def _fourier(f_in, w_fourier):
    B, S, W = f_in.shape
    n2 = DFT_INNER
    n1 = S // n2
    cols = n2 * W
    tcw = 4096
    c1, s1 = _cos_sin(n1, n1, n1)
    d1 = jnp.concatenate([c1, -s1], axis=0).astype(BF16)
    xa = f_in.reshape(B, n1, cols)
    blk = pl.BlockSpec((None, n1, tcw), lambda b, j: (b, 0, j))
    yshape = jax.ShapeDtypeStruct((B, n1, cols), BF16)
    yr, yi = pl.pallas_call(
        functools.partial(_dft_a_body, n1),
        grid=(B, cols // tcw),
        in_specs=[pl.BlockSpec((2 * n1, n1), lambda b, j: (0, 0)), blk],
        out_specs=[blk, blk],
        out_shape=[yshape, yshape],
        compiler_params=_params(("parallel", "parallel")),
        name="dft_a",
    )(d1, xa)

    yr = yr.reshape(B, n1, n2, W)
    yi = yi.reshape(B, n1, n2, W)
    twc, tws = _cos_sin(n1, n2, S)
    twc = jnp.broadcast_to(twc[:, :, None], (n1, n2, LANES))
    tws = jnp.broadcast_to(tws[:, :, None], (n1, n2, LANES))
    c2, s2 = _cos_sin(n2, n2, n2)
    d2 = jnp.concatenate([jnp.concatenate([c2, s2], axis=1),
                          jnp.concatenate([-s2, c2], axis=1)], axis=0).astype(BF16)
    cg, sg = _cos_sin(HEAD_DIM, HEAD_DIM, HEAD_DIM)
    c64 = _block_diag(jnp.broadcast_to(cg, (N_GROUPS, HEAD_DIM, HEAD_DIM))).astype(BF16)
    s64 = _block_diag(jnp.broadcast_to(sg, (N_GROUPS, HEAD_DIM, HEAD_DIM))).astype(BF16)
    wf = _block_diag(w_fourier).astype(BF16)
    ct = 8
    scale = float(1.0 / np.sqrt(S * HEAD_DIM))
    yblk = pl.BlockSpec((None, ct, n2, W), lambda b, i: (b, i, 0, 0))
    tblk = pl.BlockSpec((ct, n2, LANES), lambda b, i: (i, 0, 0))
    full = lambda r, c: pl.BlockSpec((r, c), lambda b, i: (0, 0))
    out = pl.pallas_call(
        functools.partial(_dft_b_body, ct, scale),
        grid=(B, n1 // ct),
        in_specs=[yblk, yblk, tblk, tblk, full(2 * n2, 2 * n2), full(W, W), full(W, W), full(W, W)],
        out_specs=yblk,
        out_shape=jax.ShapeDtypeStruct((B, n1, n2, W), BF16),
        compiler_params=_params(("parallel", "parallel")),
        name="dft_b",
    )(yr, yi, twc, tws, d2, c64, s64, wf)
    return out.transpose(0, 2, 1, 3).reshape(B, S, W)


def _band_attend(qf, ks, vs, valid, lo_half):
    outs, lses = [], []
    for hh in range(2):
        mine = lo_half if hh == 0 else jnp.logical_not(lo_half)
        qm = jnp.where(mine, qf, 0.0).astype(BF16)
        s = jnp.where(valid, _dot_nt(qm, ks), NEG)
        m = jnp.max(s, axis=1, keepdims=True)
        p = jnp.exp(s - m)
        l = jnp.sum(p, axis=1, keepdims=True)
        outs.append(_dot(p.astype(BF16), vs) * (1.0 / l))
        lses.append(m + jnp.log(l))
    return jnp.where(lo_half, outs[0], outs[1]), jnp.where(lo_half, lses[0], lses[1])


def _attn_masks():
    sub, win = 128, 128 + 2 * HALF_WINDOW
    lane = lax.broadcasted_iota(jnp.int32, (sub, LANES), 1)
    r = lax.broadcasted_iota(jnp.int32, (sub, win), 0)
    c = lax.broadcasted_iota(jnp.int32, (sub, win), 1)
    return lane < HEAD_DIM, (c >= r) & (c <= r + 2 * HALF_WINDOW), c


def _attn_body(tq, L, q_ref, kp_ref, kc_ref, kn_ref, vp_ref, vc_ref, vn_ref, o_ref, lse_ref):
    i = pl.program_id(1)
    sub, win = 128, 128 + 2 * HALF_WINDOW
    lo_half, band, c = _attn_masks()
    for pair in range(ATTN_WIDTH // LANES):
        sl = slice(pair * LANES, (pair + 1) * LANES)
        k2 = jnp.concatenate([kp_ref[:, sl], kc_ref[:, sl], kn_ref[:, sl]], axis=0)
        v2 = jnp.concatenate([vp_ref[:, sl], vc_ref[:, sl], vn_ref[:, sl]], axis=0)
        for sb in range(tq // sub):
            rows = slice(sb * sub, (sb + 1) * sub)
            kpos = i * tq + sb * sub - HALF_WINDOW + c
            valid = band & (kpos >= 0) & (kpos < L)
            o, lse = _band_attend(q_ref[rows, sl].astype(F32), k2[sb * sub:sb * sub + win],
                                  v2[sb * sub:sb * sub + win], valid, lo_half)
            o_ref[rows, sl] = o
            lse_ref[rows, sl] = lse


def _attn_dil_body(R, tq, L, q_ref, kp_ref, kc_ref, kn_ref, vp_ref, vc_ref, vn_ref, o_ref, lse_ref):
    i = pl.program_id(1)
    sub, win = 128, 128 + 2 * HALF_WINDOW
    lo_half, band, c = _attn_masks()
    for r in range(R):
        sl = slice(r * LANES, (r + 1) * LANES)
        k2 = jnp.concatenate([kp_ref[:, sl], kc_ref[:, sl], kn_ref[:, sl]], axis=0)
        v2 = jnp.concatenate([vp_ref[:, sl], vc_ref[:, sl], vn_ref[:, sl]], axis=0)
        for sb in range(tq // sub):
            kpos = i * tq + sb * sub - HALF_WINDOW + c
            valid = band & (kpos >= 0) & (kpos < L)
            o, lse = _band_attend(q_ref[sb * sub:(sb + 1) * sub, sl].astype(F32), k2[sb * sub:sb * sub + win],
                                  v2[sb * sub:sb * sub + win], valid, lo_half)
            dst = pl.ds(R * sb * sub + r, sub, stride=R)
            o_ref[dst, :] = o
            lse_ref[dst, :] = lse


def _attention_natural(q, k, v):
    B, S, W = q.shape
    tq = min(256, S)
    hb = tq // HALF_WINDOW
    nhalo = S // HALF_WINDOW
    ctr = pl.BlockSpec((None, tq, W), lambda b, i: (b, i, 0))
    prev = pl.BlockSpec((None, HALF_WINDOW, W), lambda b, i: (b, jnp.maximum(i * hb - 1, 0), 0))
    nxt = pl.BlockSpec((None, HALF_WINDOW, W), lambda b, i: (b, jnp.minimum((i + 1) * hb, nhalo - 1), 0))
    shp = jax.ShapeDtypeStruct((B, S, W), F32)
    return pl.pallas_call(
        functools.partial(_attn_body, tq, S),
        grid=(B, S // tq),
        in_specs=[ctr, prev, ctr, nxt, prev, ctr, nxt],
        out_specs=[ctr, ctr],
        out_shape=[shp, shp],
        compiler_params=_params(("parallel", "parallel")),
        name="attn_d1",
    )(q, k, k, k, v, v, v)


def _attention_dilated(q, k, v, R, S):
    B, L, _ = q.shape
    tq = min(2048 // R, L)
    hb = tq // HALF_WINDOW
    nhalo = L // HALF_WINDOW
    wb = R * LANES
    ctr = pl.BlockSpec((None, tq, wb), lambda b, i, p: (b, i, p))
    prev = pl.BlockSpec((None, HALF_WINDOW, wb), lambda b, i, p: (b, jnp.maximum(i * hb - 1, 0), p))
    nxt = pl.BlockSpec((None, HALF_WINDOW, wb), lambda b, i, p: (b, jnp.minimum((i + 1) * hb, nhalo - 1), p))
    out = pl.BlockSpec((None, R * tq, LANES), lambda b, i, p: (b, i, p))
    shp = jax.ShapeDtypeStruct((B, S, ATTN_WIDTH), F32)
    return pl.pallas_call(
        functools.partial(_attn_dil_body, R, tq, L),
        grid=(B, L // tq, ATTN_WIDTH // LANES),
        in_specs=[ctr, prev, ctr, nxt, prev, ctr, nxt],
        out_specs=[out, out],
        out_shape=[shp, shp],
        compiler_params=_params(("parallel", "parallel", "parallel")),
        name=f"attn_d{R}",
    )(q, k, k, k, v, v, v)


def _mix_body(x_ref, f_ref, o1_ref, o2_ref, o3_ref, l1_ref, l2_ref, l3_ref, wf_ref, wa_ref, g_ref, wq_ref,
              x1_ref, xn_ref, qp_ref):
    l1, l2, l3 = l1_ref[...], l2_ref[...], l3_ref[...]
    mx = jnp.maximum(jnp.maximum(l1, l2), l3)
    e1, e2, e3 = jnp.exp(l1 - mx), jnp.exp(l2 - mx), jnp.exp(l3 - mx)
    a = (e1 * o1_ref[...] + e2 * o2_ref[...] + e3 * o3_ref[...]) * (1.0 / (e1 + e2 + e3))
    x1 = x_ref[...] + _dot(f_ref[...], wf_ref[...]) + _dot(a.astype(BF16), wa_ref[...])
    x1_ref[...] = x1
    ms = jnp.mean(x1 * x1, axis=-1, keepdims=True)
    xn = (x1 * lax.rsqrt(ms + EPS) * g_ref[...]).astype(BF16)
    xn_ref[...] = xn
    qp_ref[...] = _dot(xn, wq_ref[...]).astype(BF16)


def _mix(x, f_out, os_, lses, w_out_bf, g2, wq_bf):
    N = x.shape[0]
    tm = 256
    row = lambda w: pl.BlockSpec((tm, w), lambda i: (i, 0))
    full = lambda r, c: pl.BlockSpec((r, c), lambda i: (0, 0))
    qw = wq_bf.shape[1]
    return pl.pallas_call(
        _mix_body,
        grid=(N // tm,),
        in_specs=[row(D_MODEL), row(FOURIER_WIDTH)] + [row(ATTN_WIDTH)] * 6
                 + [full(FOURIER_WIDTH, D_MODEL), full(ATTN_WIDTH, D_MODEL), full(1, D_MODEL), full(D_MODEL, qw)],
        out_specs=[row(D_MODEL), row(D_MODEL), row(qw)],
        out_shape=[jax.ShapeDtypeStruct((N, D_MODEL), F32), jax.ShapeDtypeStruct((N, D_MODEL), BF16),
                   jax.ShapeDtypeStruct((N, qw), BF16)],
        compiler_params=_params(("parallel",)),
        name="mix",
    )(x, f_out, *os_, *lses, w_out_bf[:FOURIER_WIDTH], w_out_bf[FOURIER_WIDTH:], g2.reshape(1, D_MODEL), wq_bf)


def _argmax_rows(parts, ids_ascending):
    while len(parts) > 1:
        nxt = []
        for k in range(0, len(parts) - 1, 2):
            (va, ia), (vb, ib) = parts[k], parts[k + 1]
            take_b = (vb > va) if ids_ascending else ((vb > va) | ((vb == va) & (ib < ia)))
            nxt.append((jnp.where(take_b, vb, va), jnp.where(take_b, ib, ia)))
        if len(parts) % 2:
            nxt.append(parts[-1])
        parts = nxt
    v, i = parts[0]
    m = jnp.max(v, axis=0, keepdims=True)
    return m, jnp.min(jnp.where(v == m, i, ID_NONE), axis=0, keepdims=True)


def _extract_topk(s, id_parts, ids_ascending):
    tq = s.shape[1]
    parts = [(s[8 * k:8 * k + 8], ids) for k, ids in enumerate(id_parts)]
    rank = lax.broadcasted_iota(jnp.int32, (TOPK, tq), 0)
    vals = jnp.zeros((TOPK, tq), F32)
    idxs = jnp.zeros((TOPK, tq), F32)
    for kk in range(TOPK):
        m, first = _argmax_rows(parts, ids_ascending)
        parts = [(jnp.where(ids == first, -jnp.inf, v), ids) for v, ids in parts]
        vals = jnp.where(rank == kk, m, vals)
        idxs = jnp.where(rank == kk, first, idxs)
    return vals, idxs


def _select_rows(sel, table):
    out = jnp.zeros_like(sel)
    for a in range(TOPK):
        out = jnp.where(sel == float(a), table[a:a + 1, :], out)
    return out


def _candidate_ids():
    ids = [a * TOPK for a in range(TOPK)]
    for b in range(1, 8):
        ids += [a * TOPK + b if (a + 1) * (b + 1) <= TOPK else ID_NONE for a in range(8)]
    ids += list(range(8, TOPK))
    return np.asarray(ids, np.float32)


def _topk_body(qp_ref, keys_ref, ids_ref, i1_ref, i2_ref, g_ref):
    tq = qp_ref.shape[0]
    sub = lax.broadcasted_iota(jnp.int32, (8, tq), 0).astype(F32)
    key_ids = [sub + float(r) for r in range(0, N_KEYS, 8)]
    cand_ids = ids_ref[...]
    cand_id_parts = [cand_ids[r:r + 8] for r in range(0, cand_ids.shape[0], 8)]
    i1s, i2s, gs = [], [], []
    for h in range(PEER_HEADS):
        halves = []
        for p in range(2):
            j = h * 2 + p
            st = _dot_nt(keys_ref[j], qp_ref[:, j * N_KEYS:(j + 1) * N_KEYS])
            halves.append(_extract_topk(st, key_ids, True))
        (v1, x1), (v2, x2) = halves
        cand = jnp.concatenate([v1 + v2[0:1, :]] + [v1[0:8, :] + v2[b:b + 1, :] for b in range(1, 8)]
                               + [v1[0:1, :] + v2[8:TOPK, :]], axis=0)
        cand = jnp.where(cand_ids < ID_NONE, cand, -jnp.inf)
        top_s, flat = _extract_topk(cand, cand_id_parts, False)
        a_sel = jnp.floor(flat * (1.0 / TOPK))
        b_sel = flat - a_sel * TOPK
        i1s.append(_select_rows(a_sel, x1))
        i2s.append(_select_rows(b_sel, x2))
        e = jnp.exp(top_s - top_s[0:1, :])
        gs.append(e * (1.0 / jnp.sum(e, axis=0, keepdims=True)))
    i1_ref[...] = jnp.concatenate(i1s, axis=0).T.astype(jnp.int32)
    i2_ref[...] = jnp.concatenate(i2s, axis=0).T.astype(jnp.int32)
    g_ref[...] = jnp.concatenate(gs, axis=0).T


def _peer_topk(qp, keys_bf):
    N, qw = qp.shape
    tq = 256
    nsel = PEER_HEADS * TOPK
    ids = _candidate_ids()
    ids = jnp.asarray(np.broadcast_to(ids[:, None], (ids.shape[0], tq)))
    out = pl.BlockSpec((tq, nsel), lambda i: (i, 0))
    return pl.pallas_call(
        _topk_body,
        grid=(N // tq,),
        in_specs=[pl.BlockSpec((tq, qw), lambda i: (i, 0)),
                  pl.BlockSpec((2 * PEER_HEADS, N_KEYS, N_KEYS), lambda i: (0, 0, 0)),
                  pl.BlockSpec(ids.shape, lambda i: (0, 0))],
        out_specs=[out, out, out],
        out_shape=[jax.ShapeDtypeStruct((N, nsel), jnp.int32), jax.ShapeDtypeStruct((N, nsel), jnp.int32),
                   jax.ShapeDtypeStruct((N, nsel), F32)],
        compiler_params=_params(("parallel",)),
        name="peer_topk",
    )(qp, keys_bf, ids)


def _pack_bf16_pair(a, b):
    abits = lax.bitcast_convert_type(a.astype(BF16).astype(F32), jnp.uint32)
    bbits = lax.bitcast_convert_type(b.astype(BF16).astype(F32), jnp.uint32)
    return lax.bitcast_convert_type(abits | (bbits >> 16), jnp.int32)


def _unpack_bf16_pair(word):
    bits = lax.bitcast_convert_type(word, jnp.uint32)
    hi = lax.bitcast_convert_type(bits & jnp.uint32(0xFFFF0000), F32)
    lo = lax.bitcast_convert_type(bits << 16, F32)
    return hi, lo

def _peer_body(tq, nh, x1_ref, xn_ref, i1_ref, i2_ref, g_ref, ul_ref, uh_ref, vl_ref, vh_ref, gf_ref,
               o_ref, gate_ref):
    c = pl.program_id(1)
    half_keys = N_KEYS // 2

    @pl.when(c == 0)
    def _():
        o_ref[...] = jnp.zeros_like(o_ref)
        row = lax.broadcasted_iota(jnp.int32, (N_KEYS, N_KEYS), 0)

        def build(blk, carry):
            base = pl.multiple_of(blk * 8, 8)
            i1b, i2b, gb = i1_ref[pl.ds(base, 8), :], i2_ref[pl.ds(base, 8), :], g_ref[pl.ds(base, 8), :]
            for r in range(8):
                p1 = jnp.where(i1b[r:r + 1, :] == row, gb[r:r + 1, :], 0.0).astype(BF16)
                p2 = jnp.where(i2b[r:r + 1, :] == row, 1.0, 0.0).astype(BF16)
                gt = _dot_nt(p1, p2)
                word = _pack_bf16_pair(gt[:half_keys], gt[half_keys:])
                gate_ref[pl.ds(pl.multiple_of((base + r) * G_PITCH, 8), half_keys), :] = word
            return carry

        lax.fori_loop(0, tq // 8, build, 0)

    xn = xn_ref[...]
    gates = [_unpack_bf16_pair(gate_ref[pl.ds(c * nh + ii, tq, stride=G_PITCH), :])
             for ii in range(nh)]
    contrib = []
    for half, (u_ref, v_ref) in enumerate(((ul_ref, vl_ref), (uh_ref, vh_ref))):
        a = _dot(xn, u_ref[...])
        act = 0.5 * a * (1.0 + lax.erf(a * float(np.sqrt(0.5))))
        w = jnp.concatenate([(act[:, ii * N_KEYS:(ii + 1) * N_KEYS] * gates[ii][half]).astype(BF16)
                             for ii in range(nh)], axis=1)
        contrib.append(_dot(w, v_ref[...]))
    o_ref[...] += contrib[0] + contrib[1]

    @pl.when(c == pl.num_programs(1) - 1)
    def _():
        y = x1_ref[...] + o_ref[...]
        ms = jnp.mean(y * y, axis=-1, keepdims=True)
        o_ref[...] = y * lax.rsqrt(ms + EPS) * gf_ref[...]


def _peer_u_blocks(u):
    ec = PEER_NH * N_KEYS
    return u.astype(BF16).reshape(N_EXPERTS // ec, ec, D_MODEL).transpose(0, 2, 1)


def _peer(x1, xn, i1, i2, g, u_blk, v_bf, gf):
    N = x1.shape[0]
    tq = 512
    nh = PEER_NH
    ec = nh * N_KEYS
    n_chunks = (N_KEYS // 2) // nh
    nsel = PEER_HEADS * TOPK
    tok = lambda w: pl.BlockSpec((tq, w), lambda i, c: (i, 0), pipeline_mode=pl.Buffered(1))
    return pl.pallas_call(
        functools.partial(_peer_body, tq, nh),
        grid=(N // tq, n_chunks),
        in_specs=[tok(D_MODEL), tok(D_MODEL), tok(nsel), tok(nsel), tok(nsel),
                  pl.BlockSpec((None, D_MODEL, ec), lambda i, c: (c, 0, 0)),
                  pl.BlockSpec((None, D_MODEL, ec), lambda i, c: (c + n_chunks, 0, 0)),
                  pl.BlockSpec((ec, D_MODEL), lambda i, c: (c, 0)),
                  pl.BlockSpec((ec, D_MODEL), lambda i, c: (c + n_chunks, 0)),
                  pl.BlockSpec((1, D_MODEL), lambda i, c: (0, 0))],
        out_specs=pl.BlockSpec((tq, D_MODEL), lambda i, c: (i, 0)),
        out_shape=jax.ShapeDtypeStruct((N, D_MODEL), F32),
        scratch_shapes=[pltpu.VMEM((tq * G_PITCH, N_KEYS), jnp.int32)],
        compiler_params=_params(("parallel", "arbitrary")),
        name="peer",
    )(x1, xn, i1, i2, g, u_blk, u_blk, v_bf, v_bf, gf.reshape(1, D_MODEL))


def _encoder(x, w):
    B, S, _ = x.shape
    f_in, qkv1, qkv4, qkv16 = _proj(x, w["norm1_g"], w["w_in"])
    f_out = _fourier(f_in, w["w_fourier"])
    groups = [_attention_natural(*qkv1), _attention_dilated(*qkv4, DILATIONS[1], S),
              _attention_dilated(*qkv16, DILATIONS[2], S)]
    N = B * S
    flat = lambda a: a.reshape(N, a.shape[-1])
    x1, xn, qp = _mix(flat(x), flat(f_out), [flat(o) for o, _ in groups], [flat(l) for _, l in groups],
                      w["w_out"], w["norm2_g"], w["w_query"])
    i1, i2, g = _peer_topk(qp, w["sub_keys"])
    y = _peer(x1, xn, i1, i2, g, w["expert_u"], w["expert_v"], w["final_g"])
    return y.reshape(B, S, D_MODEL)


def kernel(x_prompt, x_sample, norm1_g, w_in, w_fourier, w_out, norm2_g, w_query, sub_keys, expert_u, expert_v, final_g):
    w = {
        "norm1_g": norm1_g[0], "w_in": w_in[0].astype(BF16), "w_fourier": w_fourier[0],
        "w_out": w_out[0].astype(BF16), "norm2_g": norm2_g[0], "w_query": w_query[0].astype(BF16),
        "sub_keys": sub_keys[0].reshape(2 * PEER_HEADS, N_KEYS, N_KEYS).astype(BF16),
        "expert_u": _peer_u_blocks(expert_u[0]), "expert_v": expert_v[0].astype(BF16), "final_g": final_g,
    }
    return (_encoder(x_prompt, w), _encoder(x_sample, w))
```

```python
import functools

import numpy as np
import jax
import jax.numpy as jnp
from jax import lax
from jax.experimental import pallas as pl
from jax.experimental.pallas import tpu as pltpu

F32 = jnp.float32
BF16 = jnp.bfloat16

D_MODEL = 1024
HEAD_DIM = 64
FOURIER_WIDTH = 512
ATTN_WIDTH = 512
IN_WIDTH = 2048
N_GROUPS = 8
DILATIONS = (1, 4, 16)
HALF_WINDOW = 64
ROPE_THETA = 500000.0
ROPE_DIM = 16
PEER_HEADS = 8
N_KEYS = 128
N_EXPERTS = N_KEYS * N_KEYS
TOPK = 16
EPS = 1e-6
NEG = -1e30

LANES = 128
DFT_INNER = 128
G_PITCH = 72
ID_NONE = 1e9
VMEM_LIMIT = 56 * 2**20
PEER_NH = 8


def _params(sem, vmem=None):
    return pltpu.CompilerParams(dimension_semantics=sem, vmem_limit_bytes=vmem or VMEM_LIMIT)


def _dot(a, b):
    return jnp.dot(a, b, preferred_element_type=F32)


def _dot_nt(a, b):
    return lax.dot_general(a, b, (((1,), (1,)), ((), ())), preferred_element_type=F32)


def _proj_body(tm, x_ref, g_ref, w_ref, c_ref, sa_ref, sb_ref,
               f_ref, q_ref, k_ref, v_ref, q4_ref, k4_ref, v4_ref, q16_ref, k16_ref, v16_ref, stage_ref):
    x = x_ref[...]
    ms = jnp.mean(x * x, axis=-1, keepdims=True)
    h = (x * lax.rsqrt(ms + EPS) * g_ref[...]).astype(BF16)
    p = _dot(h, w_ref[...])
    f_ref[...] = p[:, :FOURIER_WIDTH].astype(BF16)
    c, sa, sb = c_ref[...], sa_ref[...], sb_ref[...]
    for j in range(ATTN_WIDTH // LANES):
        lo = j * LANES
        qc = p[:, FOURIER_WIDTH + lo:FOURIER_WIDTH + lo + LANES]
        kc = p[:, FOURIER_WIDTH + ATTN_WIDTH + lo:FOURIER_WIDTH + ATTN_WIDTH + lo + LANES]
        vc = p[:, FOURIER_WIDTH + 2 * ATTN_WIDTH + lo:FOURIER_WIDTH + 2 * ATTN_WIDTH + lo + LANES]
        qr = (qc * c + pltpu.roll(qc, LANES - 8, 1) * sa + pltpu.roll(qc, 8, 1) * sb) * (HEAD_DIM ** -0.5)
        kr = kc * c + pltpu.roll(kc, LANES - 8, 1) * sa + pltpu.roll(kc, 8, 1) * sb
        for a, (val, nat) in enumerate(((qr, q_ref), (kr, k_ref), (vc, v_ref))):
            nat[:, lo:lo + LANES] = val.astype(BF16)
            stage_ref[a, j] = val
    for a, dil in enumerate(((q4_ref, q16_ref), (k4_ref, k16_ref), (v4_ref, v16_ref))):
        for R, d_ref in zip(DILATIONS[1:], dil):
            for j in range(ATTN_WIDTH // LANES):
                for r in range(R):
                    piece = stage_ref[a, j, pl.ds(r, tm // R, stride=R), :]
                    d_ref[:, (j * R + r) * LANES:(j * R + r + 1) * LANES] = piece.astype(BF16)


def _rope_tables(S):
    half = ROPE_DIM // 2
    inv = ROPE_THETA ** (-(jnp.arange(half, dtype=F32) * 2.0) / ROPE_DIM)
    ang = jnp.arange(S, dtype=F32)[:, None] * inv[None, :]
    cos, sin = jnp.cos(ang), jnp.sin(ang)
    one = jnp.ones((S, HEAD_DIM - ROPE_DIM), F32)
    zero = jnp.zeros((S, HEAD_DIM - ROPE_DIM), F32)
    z8 = jnp.zeros((S, half), F32)
    c = jnp.concatenate([cos, cos, one], axis=1)
    sa = jnp.concatenate([-sin, z8, zero], axis=1)
    sb = jnp.concatenate([z8, sin, zero], axis=1)
    rep = LANES // HEAD_DIM
    return jnp.tile(c, (1, rep)), jnp.tile(sa, (1, rep)), jnp.tile(sb, (1, rep))


def _proj(x, g, w_bf):
    B, S, _ = x.shape
    tm = min(512, S)
    c, sa, sb = _rope_tables(S)
    tab = pl.BlockSpec((tm, LANES), lambda b, i: (i, 0))
    out = pl.BlockSpec((None, tm, ATTN_WIDTH), lambda b, i: (b, i, 0))
    shp = jax.ShapeDtypeStruct((B, S, ATTN_WIDTH), BF16)
    dil_specs, dil_shapes = [], []
    for R in DILATIONS[1:]:
        dil_specs.append(pl.BlockSpec((None, tm // R, R * ATTN_WIDTH), lambda b, i: (b, i, 0)))
        dil_shapes.append(jax.ShapeDtypeStruct((B, S // R, R * ATTN_WIDTH), BF16))
    outs = pl.pallas_call(
        functools.partial(_proj_body, tm),
        grid=(B, S // tm),
        in_specs=[pl.BlockSpec((None, tm, D_MODEL), lambda b, i: (b, i, 0)),
                  pl.BlockSpec((1, D_MODEL), lambda b, i: (0, 0)),
                  pl.BlockSpec((D_MODEL, IN_WIDTH), lambda b, i: (0, 0)),
                  tab, tab, tab],
        out_specs=[out] * 4 + [dil_specs[0]] * 3 + [dil_specs[1]] * 3,
        out_shape=[shp] * 4 + [dil_shapes[0]] * 3 + [dil_shapes[1]] * 3,
        scratch_shapes=[pltpu.VMEM((3, ATTN_WIDTH // LANES, tm, LANES), F32)],
        compiler_params=_params(("parallel", "parallel")),
        name="proj",
    )(x, g.reshape(1, D_MODEL), w_bf, c, sa, sb)
    return outs[0], outs[1:4], outs[4:7], outs[7:10]


def _cos_sin(n_rows, n_cols, period):
    i = jnp.arange(n_rows, dtype=jnp.int32)[:, None]
    j = jnp.arange(n_cols, dtype=jnp.int32)[None, :]
    ang = ((i * j) % period).astype(F32) * (2.0 * np.pi / period)
    return jnp.cos(ang), jnp.sin(ang)


def _dft_a_body(n1, d_ref, x_ref, yr_ref, yi_ref):
    y = _dot(d_ref[...], x_ref[...])
    yr_ref[...] = y[:n1].astype(BF16)
    yi_ref[...] = y[n1:].astype(BF16)


def _dft_b_body(ct, scale, yr_ref, yi_ref, tc_ref, ts_ref, d2_ref, c64_ref, s64_ref, wf_ref, o_ref):
    rep = FOURIER_WIDTH // LANES
    for cc in range(ct):
        yr = yr_ref[cc].astype(F32)
        yi = yi_ref[cc].astype(F32)
        tc = jnp.concatenate([tc_ref[cc]] * rep, axis=1)
        ts = jnp.concatenate([ts_ref[cc]] * rep, axis=1)
        zr = yr * tc + yi * ts
        zi = yi * tc - yr * ts
        z = jnp.concatenate([zr, zi], axis=0).astype(BF16)
        xx = _dot(d2_ref[...], z)
        xr = xx[:DFT_INNER].astype(BF16)
        xi = xx[DFT_INNER:].astype(BF16)
        re = (_dot(xr, c64_ref[...]) + _dot(xi, s64_ref[...])) * scale
        o_ref[cc] = _dot(re.astype(BF16), wf_ref[...]).astype(BF16)


def _block_diag(m):
    G, n, _ = m.shape
    eye = jnp.eye(G, dtype=m.dtype)
    return (eye[:, None, :, None] * m[:, :, None, :]).reshape(G * n, G * n)


def _fourier(f_in, w_fourier):
    B, S, W = f_in.shape
    n2 = DFT_INNER
    n1 = S // n2
    cols = n2 * W
    tcw = 4096
    c1, s1 = _cos_sin(n1, n1, n1)
    d1 = jnp.concatenate([c1, -s1], axis=0).astype(BF16)
    xa = f_in.reshape(B, n1, cols)
    blk = pl.BlockSpec((None, n1, tcw), lambda b, j: (b, 0, j))
    yshape = jax.ShapeDtypeStruct((B, n1, cols), BF16)
    yr, yi = pl.pallas_call(
        functools.partial(_dft_a_body, n1),
        grid=(B, cols // tcw),
        in_specs=[pl.BlockSpec((2 * n1, n1), lambda b, j: (0, 0)), blk],
        out_specs=[blk, blk],
        out_shape=[yshape, yshape],
        compiler_params=_params(("parallel", "parallel")),
        name="dft_a",
    )(d1, xa)

    yr = yr.reshape(B, n1, n2, W)
    yi = yi.reshape(B, n1, n2, W)
    twc, tws = _cos_sin(n1, n2, S)
    twc = jnp.broadcast_to(twc[:, :, None], (n1, n2, LANES))
    tws = jnp.broadcast_to(tws[:, :, None], (n1, n2, LANES))
    c2, s2 = _cos_sin(n2, n2, n2)
    d2 = jnp.concatenate([jnp.concatenate([c2, s2], axis=1),
                          jnp.concatenate([-s2, c2], axis=1)], axis=0).astype(BF16)
    cg, sg = _cos_sin(HEAD_DIM, HEAD_DIM, HEAD_DIM)
    c64 = _block_diag(jnp.broadcast_to(cg, (N_GROUPS, HEAD_DIM, HEAD_DIM))).astype(BF16)
    s64 = _block_diag(jnp.broadcast_to(sg, (N_GROUPS, HEAD_DIM, HEAD_DIM))).astype(BF16)
    wf = _block_diag(w_fourier).astype(BF16)
    ct = 8
    scale = float(1.0 / np.sqrt(S * HEAD_DIM))
    yblk = pl.BlockSpec((None, ct, n2, W), lambda b, i: (b, i, 0, 0))
    tblk = pl.BlockSpec((ct, n2, LANES), lambda b, i: (i, 0, 0))
    full = lambda r, c: pl.BlockSpec((r, c), lambda b, i: (0, 0))
    out = pl.pallas_call(
        functools.partial(_dft_b_body, ct, scale),
        grid=(B, n1 // ct),
        in_specs=[yblk, yblk, tblk, tblk, full(2 * n2, 2 * n2), full(W, W), full(W, W), full(W, W)],
        out_specs=yblk,
        out_shape=jax.ShapeDtypeStruct((B, n1, n2, W), BF16),
        compiler_params=_params(("parallel", "parallel")),
        name="dft_b",
    )(yr, yi, twc, tws, d2, c64, s64, wf)
    return out.transpose(0, 2, 1, 3).reshape(B, S, W)


def _band_attend(qf, ks, vs, valid, lo_half):
    outs, lses = [], []
    for hh in range(2):
        mine = lo_half if hh == 0 else jnp.logical_not(lo_half)
        qm = jnp.where(mine, qf, 0.0).astype(BF16)
        s = jnp.where(valid, _dot_nt(qm, ks), NEG)
        m = jnp.max(s, axis=1, keepdims=True)
        p = jnp.exp(s - m)
        l = jnp.sum(p, axis=1, keepdims=True)
        outs.append(_dot(p.astype(BF16), vs) * (1.0 / l))
        lses.append(m + jnp.log(l))
    return jnp.where(lo_half, outs[0], outs[1]), jnp.where(lo_half, lses[0], lses[1])


def _attn_masks():
    sub, win = 128, 128 + 2 * HALF_WINDOW
    lane = lax.broadcasted_iota(jnp.int32, (sub, LANES), 1)
    r = lax.broadcasted_iota(jnp.int32, (sub, win), 0)
    c = lax.broadcasted_iota(jnp.int32, (sub, win), 1)
    return lane < HEAD_DIM, (c >= r) & (c <= r + 2 * HALF_WINDOW), c


def _attn_body(tq, L, q_ref, kp_ref, kc_ref, kn_ref, vp_ref, vc_ref, vn_ref, o_ref, lse_ref):
    i = pl.program_id(1)
    sub, win = 128, 128 + 2 * HALF_WINDOW
    lo_half, band, c = _attn_masks()
    for pair in range(ATTN_WIDTH // LANES):
        sl = slice(pair * LANES, (pair + 1) * LANES)
        k2 = jnp.concatenate([kp_ref[:, sl], kc_ref[:, sl], kn_ref[:, sl]], axis=0)
        v2 = jnp.concatenate([vp_ref[:, sl], vc_ref[:, sl], vn_ref[:, sl]], axis=0)
        for sb in range(tq // sub):
            rows = slice(sb * sub, (sb + 1) * sub)
            kpos = i * tq + sb * sub - HALF_WINDOW + c
            valid = band & (kpos >= 0) & (kpos < L)
            o, lse = _band_attend(q_ref[rows, sl].astype(F32), k2[sb * sub:sb * sub + win],
                                  v2[sb * sub:sb * sub + win], valid, lo_half)
            o_ref[rows, sl] = o
            lse_ref[rows, sl] = lse


def _attn_dil_body(R, tq, L, q_ref, kp_ref, kc_ref, kn_ref, vp_ref, vc_ref, vn_ref, o_ref, lse_ref):
    i = pl.program_id(1)
    sub, win = 128, 128 + 2 * HALF_WINDOW
    lo_half, band, c = _attn_masks()
    for r in range(R):
        sl = slice(r * LANES, (r + 1) * LANES)
        k2 = jnp.concatenate([kp_ref[:, sl], kc_ref[:, sl], kn_ref[:, sl]], axis=0)
        v2 = jnp.concatenate([vp_ref[:, sl], vc_ref[:, sl], vn_ref[:, sl]], axis=0)
        for sb in range(tq // sub):
            kpos = i * tq + sb * sub - HALF_WINDOW + c
            valid = band & (kpos >= 0) & (kpos < L)
            o, lse = _band_attend(q_ref[sb * sub:(sb + 1) * sub, sl].astype(F32), k2[sb * sub:sb * sub + win],
                                  v2[sb * sub:sb * sub + win], valid, lo_half)
            dst = pl.ds(R * sb * sub + r, sub, stride=R)
            o_ref[dst, :] = o
            lse_ref[dst, :] = lse


def _attention_natural(q, k, v):
    B, S, W = q.shape
    tq = min(256, S)
    hb = tq // HALF_WINDOW
    nhalo = S // HALF_WINDOW
    ctr = pl.BlockSpec((None, tq, W), lambda b, i: (b, i, 0))
    prev = pl.BlockSpec((None, HALF_WINDOW, W), lambda b, i: (b, jnp.maximum(i * hb - 1, 0), 0))
    nxt = pl.BlockSpec((None, HALF_WINDOW, W), lambda b, i: (b, jnp.minimum((i + 1) * hb, nhalo - 1), 0))
    shp = jax.ShapeDtypeStruct((B, S, W), F32)
    return pl.pallas_call(
        functools.partial(_attn_body, tq, S),
        grid=(B, S // tq),
        in_specs=[ctr, prev, ctr, nxt, prev, ctr, nxt],
        out_specs=[ctr, ctr],
        out_shape=[shp, shp],
        compiler_params=_params(("parallel", "parallel")),
        name="attn_d1",
    )(q, k, k, k, v, v, v)


def _attention_dilated(q, k, v, R, S):
    B, L, _ = q.shape
    tq = min(2048 // R, L)
    hb = tq // HALF_WINDOW
    nhalo = L // HALF_WINDOW
    wb = R * LANES
    ctr = pl.BlockSpec((None, tq, wb), lambda b, i, p: (b, i, p))
    prev = pl.BlockSpec((None, HALF_WINDOW, wb), lambda b, i, p: (b, jnp.maximum(i * hb - 1, 0), p))
    nxt = pl.BlockSpec((None, HALF_WINDOW, wb), lambda b, i, p: (b, jnp.minimum((i + 1) * hb, nhalo - 1), p))
    out = pl.BlockSpec((None, R * tq, LANES), lambda b, i, p: (b, i, p))
    shp = jax.ShapeDtypeStruct((B, S, ATTN_WIDTH), F32)
    return pl.pallas_call(
        functools.partial(_attn_dil_body, R, tq, L),
        grid=(B, L // tq, ATTN_WIDTH // LANES),
        in_specs=[ctr, prev, ctr, nxt, prev, ctr, nxt],
        out_specs=[out, out],
        out_shape=[shp, shp],
        compiler_params=_params(("parallel", "parallel", "parallel")),
        name=f"attn_d{R}",
    )(q, k, k, k, v, v, v)


def _mix_body(x_ref, f_ref, o1_ref, o2_ref, o3_ref, l1_ref, l2_ref, l3_ref, wf_ref, wa_ref, g_ref, wq_ref,
              x1_ref, xn_ref, qp_ref):
    l1, l2, l3 = l1_ref[...], l2_ref[...], l3_ref[...]
    mx = jnp.maximum(jnp.maximum(l1, l2), l3)
    e1, e2, e3 = jnp.exp(l1 - mx), jnp.exp(l2 - mx), jnp.exp(l3 - mx)
    a = (e1 * o1_ref[...] + e2 * o2_ref[...] + e3 * o3_ref[...]) * (1.0 / (e1 + e2 + e3))
    x1 = x_ref[...] + _dot(f_ref[...], wf_ref[...]) + _dot(a.astype(BF16), wa_ref[...])
    x1_ref[...] = x1
    ms = jnp.mean(x1 * x1, axis=-1, keepdims=True)
    xn = (x1 * lax.rsqrt(ms + EPS) * g_ref[...]).astype(BF16)
    xn_ref[...] = xn
    qp_ref[...] = _dot(xn, wq_ref[...]).astype(BF16)


def _mix(x, f_out, os_, lses, w_out_bf, g2, wq_bf):
    N = x.shape[0]
    tm = 256
    row = lambda w: pl.BlockSpec((tm, w), lambda i: (i, 0))
    full = lambda r, c: pl.BlockSpec((r, c), lambda i: (0, 0))
    qw = wq_bf.shape[1]
    return pl.pallas_call(
        _mix_body,
        grid=(N // tm,),
        in_specs=[row(D_MODEL), row(FOURIER_WIDTH)] + [row(ATTN_WIDTH)] * 6
                 + [full(FOURIER_WIDTH, D_MODEL), full(ATTN_WIDTH, D_MODEL), full(1, D_MODEL), full(D_MODEL, qw)],
        out_specs=[row(D_MODEL), row(D_MODEL), row(qw)],
        out_shape=[jax.ShapeDtypeStruct((N, D_MODEL), F32), jax.ShapeDtypeStruct((N, D_MODEL), BF16),
                   jax.ShapeDtypeStruct((N, qw), BF16)],
        compiler_params=_params(("parallel",)),
        name="mix",
    )(x, f_out, *os_, *lses, w_out_bf[:FOURIER_WIDTH], w_out_bf[FOURIER_WIDTH:], g2.reshape(1, D_MODEL), wq_bf)


def _argmax_rows(parts, ids_ascending):
    while len(parts) > 1:
        nxt = []
        for k in range(0, len(parts) - 1, 2):
            (va, ia), (vb, ib) = parts[k], parts[k + 1]
            take_b = (vb > va) if ids_ascending else ((vb > va) | ((vb == va) & (ib < ia)))
            nxt.append((jnp.where(take_b, vb, va), jnp.where(take_b, ib, ia)))
        if len(parts) % 2:
            nxt.append(parts[-1])
        parts = nxt
    v, i = parts[0]
    m = jnp.max(v, axis=0, keepdims=True)
    return m, jnp.min(jnp.where(v == m, i, ID_NONE), axis=0, keepdims=True)


def _extract_topk(s, id_parts, ids_ascending):
    tq = s.shape[1]
    parts = [(s[8 * k:8 * k + 8], ids) for k, ids in enumerate(id_parts)]
    rank = lax.broadcasted_iota(jnp.int32, (TOPK, tq), 0)
    vals = jnp.zeros((TOPK, tq), F32)
    idxs = jnp.zeros((TOPK, tq), F32)
    for kk in range(TOPK):
        m, first = _argmax_rows(parts, ids_ascending)
        parts = [(jnp.where(ids == first, -jnp.inf, v), ids) for v, ids in parts]
        vals = jnp.where(rank == kk, m, vals)
        idxs = jnp.where(rank == kk, first, idxs)
    return vals, idxs


def _select_rows(sel, table):
    out = jnp.zeros_like(sel)
    for a in range(TOPK):
        out = jnp.where(sel == float(a), table[a:a + 1, :], out)
    return out


def _candidate_ids():
    ids = [a * TOPK for a in range(TOPK)]
    for b in range(1, 8):
        ids += [a * TOPK + b if (a + 1) * (b + 1) <= TOPK else ID_NONE for a in range(8)]
    ids += list(range(8, TOPK))
    return np.asarray(ids, np.float32)


def _topk_body(qp_ref, keys_ref, ids_ref, i1_ref, i2t_ref, g_ref):
    tq = qp_ref.shape[0]
    sub = lax.broadcasted_iota(jnp.int32, (8, tq), 0).astype(F32)
    key_ids = [sub + float(r) for r in range(0, N_KEYS, 8)]
    cand_ids = ids_ref[...]
    cand_id_parts = [cand_ids[r:r + 8] for r in range(0, cand_ids.shape[0], 8)]
    i1s, i2s, gs = [], [], []
    for h in range(PEER_HEADS):
        halves = []
        for p in range(2):
            j = h * 2 + p
            st = _dot_nt(keys_ref[j], qp_ref[:, j * N_KEYS:(j + 1) * N_KEYS])
            halves.append(_extract_topk(st, key_ids, True))
        (v1, x1), (v2, x2) = halves
        cand = jnp.concatenate([v1 + v2[0:1, :]] + [v1[0:8, :] + v2[b:b + 1, :] for b in range(1, 8)]
                               + [v1[0:1, :] + v2[8:TOPK, :]], axis=0)
        cand = jnp.where(cand_ids < ID_NONE, cand, -jnp.inf)
        top_s, flat = _extract_topk(cand, cand_id_parts, False)
        a_sel = jnp.floor(flat * (1.0 / TOPK))
        b_sel = flat - a_sel * TOPK
        i1s.append(_select_rows(a_sel, x1))
        i2s.append(_select_rows(b_sel, x2))
        e = jnp.exp(top_s - top_s[0:1, :])
        gs.append(e * (1.0 / jnp.sum(e, axis=0, keepdims=True)))
    i1_ref[...] = jnp.concatenate(i1s, axis=0).T.astype(jnp.int32)
    i2 = jnp.concatenate(i2s, axis=0).astype(jnp.int32)
    for grp in range(tq // LANES):
        i2t_ref[grp] = i2[:, grp * LANES:(grp + 1) * LANES]
    g_ref[...] = jnp.concatenate(gs, axis=0).T


def _peer_topk(qp, keys_bf):
    N, qw = qp.shape
    tq = 256
    nsel = PEER_HEADS * TOPK
    ids = _candidate_ids()
    ids = jnp.asarray(np.broadcast_to(ids[:, None], (ids.shape[0], tq)))
    out = pl.BlockSpec((tq, nsel), lambda i: (i, 0))
    return pl.pallas_call(
        _topk_body,
        grid=(N // tq,),
        in_specs=[pl.BlockSpec((tq, qw), lambda i: (i, 0)),
                  pl.BlockSpec((2 * PEER_HEADS, N_KEYS, N_KEYS), lambda i: (0, 0, 0)),
                  pl.BlockSpec(ids.shape, lambda i: (0, 0))],
        out_specs=[out, pl.BlockSpec((tq // LANES, nsel, LANES), lambda i: (i, 0, 0)), out],
        out_shape=[jax.ShapeDtypeStruct((N, nsel), jnp.int32),
                   jax.ShapeDtypeStruct((N // LANES, nsel, LANES), jnp.int32),
                   jax.ShapeDtypeStruct((N, nsel), F32)],
        compiler_params=_params(("parallel",)),
        name="peer_topk",
    )(qp, keys_bf, ids)


def _pack_bf16_pair(a, b):
    abits = lax.bitcast_convert_type(a.astype(BF16).astype(F32), jnp.uint32)
    bbits = lax.bitcast_convert_type(b.astype(BF16).astype(F32), jnp.uint32)
    return lax.bitcast_convert_type(abits | (bbits >> 16), jnp.int32)


def _unpack_bf16_pair(word):
    bits = lax.bitcast_convert_type(word, jnp.uint32)
    hi = lax.bitcast_convert_type(bits & jnp.uint32(0xFFFF0000), F32)
    lo = lax.bitcast_convert_type(bits << 16, F32)
    return hi, lo

def _peer_body(tq, nh, x1_ref, xn_ref, i1_ref, i2t_ref, g_ref, ul_ref, uh_ref, vl_ref, vh_ref, gf_ref,
               o_ref, gate_ref):
    c = pl.program_id(1)
    half_keys = N_KEYS // 2

    @pl.when(c == 0)
    def _():
        o_ref[...] = jnp.zeros_like(o_ref)
        row = lax.broadcasted_iota(jnp.int32, (N_KEYS, N_KEYS), 0)
        lane = lax.broadcasted_iota(jnp.int32, (N_KEYS, N_KEYS), 1)

        def build(grp, carry):
            base = pl.multiple_of(grp * LANES, LANES)
            i1g, gg = i1_ref[pl.ds(base, LANES), :], g_ref[pl.ds(base, LANES), :]
            i2t = i2t_ref[grp]
            for r in range(LANES):
                p1 = jnp.where(i1g[r:r + 1, :] == row, gg[r:r + 1, :], 0.0).astype(BF16)
                p2 = jnp.where(i2t[:, r:r + 1] == lane, 1.0, 0.0).astype(BF16)
                gt = _dot(p1, p2)
                word = _pack_bf16_pair(gt[:half_keys], gt[half_keys:])
                gate_ref[pl.ds(pl.multiple_of((base + r) * G_PITCH, 8), half_keys), :] = word
            return carry

        lax.fori_loop(0, tq // LANES, build, 0)

    xn = xn_ref[...]
    gates = [_unpack_bf16_pair(gate_ref[pl.ds(c * nh + ii, tq, stride=G_PITCH), :])
             for ii in range(nh)]
    contrib = []
    for half, (u_ref, v_ref) in enumerate(((ul_ref, vl_ref), (uh_ref, vh_ref))):
        a = _dot(xn, u_ref[...])
        act = 0.5 * a * (1.0 + lax.erf(a * float(np.sqrt(0.5))))
        w = jnp.concatenate([(act[:, ii * N_KEYS:(ii + 1) * N_KEYS] * gates[ii][half]).astype(BF16)
                             for ii in range(nh)], axis=1)
        contrib.append(_dot(w, v_ref[...]))
    o_ref[...] += contrib[0] + contrib[1]

    @pl.when(c == pl.num_programs(1) - 1)
    def _():
        y = x1_ref[...] + o_ref[...]
        ms = jnp.mean(y * y, axis=-1, keepdims=True)
        o_ref[...] = y * lax.rsqrt(ms + EPS) * gf_ref[...]


def _peer_u_blocks(u):
    ec = PEER_NH * N_KEYS
    return u.astype(BF16).reshape(N_EXPERTS // ec, ec, D_MODEL).transpose(0, 2, 1)


def _peer(x1, xn, i1, i2t, g, u_blk, v_bf, gf):
    N = x1.shape[0]
    tq = 512
    nh = PEER_NH
    ec = nh * N_KEYS
    n_chunks = (N_KEYS // 2) // nh
    nsel = PEER_HEADS * TOPK
    tok = lambda w: pl.BlockSpec((tq, w), lambda i, c: (i, 0), pipeline_mode=pl.Buffered(1))
    return pl.pallas_call(
        functools.partial(_peer_body, tq, nh),
        grid=(N // tq, n_chunks),
        in_specs=[tok(D_MODEL), tok(D_MODEL), tok(nsel),
                  pl.BlockSpec((tq // LANES, nsel, LANES), lambda i, c: (i, 0, 0), pipeline_mode=pl.Buffered(1)),
                  tok(nsel),
                  pl.BlockSpec((None, D_MODEL, ec), lambda i, c: (c, 0, 0)),
                  pl.BlockSpec((None, D_MODEL, ec), lambda i, c: (c + n_chunks, 0, 0)),
                  pl.BlockSpec((ec, D_MODEL), lambda i, c: (c, 0)),
                  pl.BlockSpec((ec, D_MODEL), lambda i, c: (c + n_chunks, 0)),
                  pl.BlockSpec((1, D_MODEL), lambda i, c: (0, 0))],
        out_specs=pl.BlockSpec((tq, D_MODEL), lambda i, c: (i, 0)),
        out_shape=jax.ShapeDtypeStruct((N, D_MODEL), F32),
        scratch_shapes=[pltpu.VMEM((tq * G_PITCH, N_KEYS), jnp.int32)],
        compiler_params=_params(("parallel", "arbitrary")),
        name="peer",
    )(x1, xn, i1, i2t, g, u_blk, u_blk, v_bf, v_bf, gf.reshape(1, D_MODEL))


def _encoder(x, w):
    B, S, _ = x.shape
    f_in, qkv1, qkv4, qkv16 = _proj(x, w["norm1_g"], w["w_in"])
    f_out = _fourier(f_in, w["w_fourier"])
    groups = [_attention_natural(*qkv1), _attention_dilated(*qkv4, DILATIONS[1], S),
              _attention_dilated(*qkv16, DILATIONS[2], S)]
    N = B * S
    flat = lambda a: a.reshape(N, a.shape[-1])
    x1, xn, qp = _mix(flat(x), flat(f_out), [flat(o) for o, _ in groups], [flat(l) for _, l in groups],
                      w["w_out"], w["norm2_g"], w["w_query"])
    i1, i2t, g = _peer_topk(qp, w["sub_keys"])
    y = _peer(x1, xn, i1, i2t, g, w["expert_u"], w["expert_v"], w["final_g"])
    return y.reshape(B, S, D_MODEL)


def kernel(x_prompt, x_sample, norm1_g, w_in, w_fourier, w_out, norm2_g, w_query, sub_keys, expert_u, expert_v, final_g):
    w = {
        "norm1_g": norm1_g[0], "w_in": w_in[0].astype(BF16), "w_fourier": w_fourier[0],
        "w_out": w_out[0].astype(BF16), "norm2_g": norm2_g[0], "w_query": w_query[0].astype(BF16),
        "sub_keys": sub_keys[0].reshape(2 * PEER_HEADS, N_KEYS, N_KEYS).astype(BF16),
        "expert_u": _peer_u_blocks(expert_u[0]), "expert_v": expert_v[0].astype(BF16), "final_g": final_g,
    }
    return (_encoder(x_prompt, w), _encoder(x_sample, w))
```

```python
import functools

import numpy as np
import jax
import jax.numpy as jnp
from jax import lax
from jax.experimental import pallas as pl
from jax.experimental.pallas import tpu as pltpu

F32 = jnp.float32
BF16 = jnp.bfloat16

D_MODEL = 1024
HEAD_DIM = 64
FOURIER_WIDTH = 512
ATTN_WIDTH = 512
IN_WIDTH = 2048
N_GROUPS = 8
DILATIONS = (1, 4, 16)
HALF_WINDOW = 64
ROPE_THETA = 500000.0
ROPE_DIM = 16
PEER_HEADS = 8
N_KEYS = 128
N_EXPERTS = N_KEYS * N_KEYS
TOPK = 16
EPS = 1e-6
NEG = -1e30

LANES = 128
DFT_INNER = 128
G_PITCH = 72
ID_NONE = 1e9
VMEM_LIMIT = 56 * 2**20
PEER_NH = 8


def _params(sem, vmem=None):
    return pltpu.CompilerParams(dimension_semantics=sem, vmem_limit_bytes=vmem or VMEM_LIMIT)


def _dot(a, b):
    return jnp.dot(a, b, preferred_element_type=F32)


def _dot_nt(a, b):
    return lax.dot_general(a, b, (((1,), (1,)), ((), ())), preferred_element_type=F32)


def _proj_body(tm, x_ref, g_ref, w_ref, c_ref, sa_ref, sb_ref,
               f_ref, q_ref, k_ref, v_ref, q4_ref, k4_ref, v4_ref, q16_ref, k16_ref, v16_ref, stage_ref):
    x = x_ref[...]
    ms = jnp.mean(x * x, axis=-1, keepdims=True)
    h = (x * lax.rsqrt(ms + EPS) * g_ref[...]).astype(BF16)
    p = _dot(h, w_ref[...])
    f_ref[...] = p[:, :FOURIER_WIDTH].astype(BF16)
    c, sa, sb = c_ref[...], sa_ref[...], sb_ref[...]
    for j in range(ATTN_WIDTH // LANES):
        lo = j * LANES
        qc = p[:, FOURIER_WIDTH + lo:FOURIER_WIDTH + lo + LANES]
        kc = p[:, FOURIER_WIDTH + ATTN_WIDTH + lo:FOURIER_WIDTH + ATTN_WIDTH + lo + LANES]
        vc = p[:, FOURIER_WIDTH + 2 * ATTN_WIDTH + lo:FOURIER_WIDTH + 2 * ATTN_WIDTH + lo + LANES]
        qr = (qc * c + pltpu.roll(qc, LANES - 8, 1) * sa + pltpu.roll(qc, 8, 1) * sb) * (HEAD_DIM ** -0.5)
        kr = kc * c + pltpu.roll(kc, LANES - 8, 1) * sa + pltpu.roll(kc, 8, 1) * sb
        for a, (val, nat) in enumerate(((qr, q_ref), (kr, k_ref), (vc, v_ref))):
            nat[:, lo:lo + LANES] = val.astype(BF16)
            stage_ref[a, j] = val
    for a, dil in enumerate(((q4_ref, q16_ref), (k4_ref, k16_ref), (v4_ref, v16_ref))):
        for R, d_ref in zip(DILATIONS[1:], dil):
            for j in range(ATTN_WIDTH // LANES):
                for r in range(R):
                    piece = stage_ref[a, j, pl.ds(r, tm // R, stride=R), :]
                    d_ref[:, (j * R + r) * LANES:(j * R + r + 1) * LANES] = piece.astype(BF16)


def _rope_tables(S):
    half = ROPE_DIM // 2
    inv = ROPE_THETA ** (-(jnp.arange(half, dtype=F32) * 2.0) / ROPE_DIM)
    ang = jnp.arange(S, dtype=F32)[:, None] * inv[None, :]
    cos, sin = jnp.cos(ang), jnp.sin(ang)
    one = jnp.ones((S, HEAD_DIM - ROPE_DIM), F32)
    zero = jnp.zeros((S, HEAD_DIM - ROPE_DIM), F32)
    z8 = jnp.zeros((S, half), F32)
    c = jnp.concatenate([cos, cos, one], axis=1)
    sa = jnp.concatenate([-sin, z8, zero], axis=1)
    sb = jnp.concatenate([z8, sin, zero], axis=1)
    rep = LANES // HEAD_DIM
    return jnp.tile(c, (1, rep)), jnp.tile(sa, (1, rep)), jnp.tile(sb, (1, rep))


def _proj(x, g, w_bf):
    B, S, _ = x.shape
    tm = min(512, S)
    c, sa, sb = _rope_tables(S)
    tab = pl.BlockSpec((tm, LANES), lambda b, i: (i, 0))
    out = pl.BlockSpec((None, tm, ATTN_WIDTH), lambda b, i: (b, i, 0))
    shp = jax.ShapeDtypeStruct((B, S, ATTN_WIDTH), BF16)
    dil_specs, dil_shapes = [], []
    for R in DILATIONS[1:]:
        dil_specs.append(pl.BlockSpec((None, tm // R, R * ATTN_WIDTH), lambda b, i: (b, i, 0)))
        dil_shapes.append(jax.ShapeDtypeStruct((B, S // R, R * ATTN_WIDTH), BF16))
    outs = pl.pallas_call(
        functools.partial(_proj_body, tm),
        grid=(B, S // tm),
        in_specs=[pl.BlockSpec((None, tm, D_MODEL), lambda b, i: (b, i, 0)),
                  pl.BlockSpec((1, D_MODEL), lambda b, i: (0, 0)),
                  pl.BlockSpec((D_MODEL, IN_WIDTH), lambda b, i: (0, 0)),
                  tab, tab, tab],
        out_specs=[out] * 4 + [dil_specs[0]] * 3 + [dil_specs[1]] * 3,
        out_shape=[shp] * 4 + [dil_shapes[0]] * 3 + [dil_shapes[1]] * 3,
        scratch_shapes=[pltpu.VMEM((3, ATTN_WIDTH // LANES, tm, LANES), F32)],
        compiler_params=_params(("parallel", "parallel")),
        name="proj",
    )(x, g.reshape(1, D_MODEL), w_bf, c, sa, sb)
    return outs[0], outs[1:4], outs[4:7], outs[7:10]


def _cos_sin(n_rows, n_cols, period):
    i = jnp.arange(n_rows, dtype=jnp.int32)[:, None]
    j = jnp.arange(n_cols, dtype=jnp.int32)[None, :]
    ang = ((i * j) % period).astype(F32) * (2.0 * np.pi / period)
    return jnp.cos(ang), jnp.sin(ang)


def _dft_a_body(n1, d_ref, x_ref, yr_ref, yi_ref):
    y = _dot(d_ref[...], x_ref[...])
    yr_ref[...] = y[:n1].astype(BF16)
    yi_ref[...] = y[n1:].astype(BF16)


def _dft_b_body(ct, scale, yr_ref, yi_ref, tc_ref, ts_ref, d2_ref, c64_ref, s64_ref, wf_ref, o_ref):
    rep = FOURIER_WIDTH // LANES
    for cc in range(ct):
        yr = yr_ref[cc].astype(F32)
        yi = yi_ref[cc].astype(F32)
        tc = jnp.concatenate([tc_ref[cc]] * rep, axis=1)
        ts = jnp.concatenate([ts_ref[cc]] * rep, axis=1)
        zr = yr * tc + yi * ts
        zi = yi * tc - yr * ts
        z = jnp.concatenate([zr, zi], axis=0).astype(BF16)
        xx = _dot(d2_ref[...], z)
        xr = xx[:DFT_INNER].astype(BF16)
        xi = xx[DFT_INNER:].astype(BF16)
        re = (_dot(xr, c64_ref[...]) + _dot(xi, s64_ref[...])) * scale
        o_ref[cc] = _dot(re.astype(BF16), wf_ref[...]).astype(BF16)


def _block_diag(m):
    G, n, _ = m.shape
    eye = jnp.eye(G, dtype=m.dtype)
    return (eye[:, None, :, None] * m[:, :, None, :]).reshape(G * n, G * n)


def _fourier(f_in, w_fourier):
    B, S, W = f_in.shape
    n2 = DFT_INNER
    n1 = S // n2
    cols = n2 * W
    tcw = 4096
    c1, s1 = _cos_sin(n1, n1, n1)
    d1 = jnp.concatenate([c1, -s1], axis=0).astype(BF16)
    xa = f_in.reshape(B, n1, cols)
    blk = pl.BlockSpec((None, n1, tcw), lambda b, j: (b, 0, j))
    yshape = jax.ShapeDtypeStruct((B, n1, cols), BF16)
    yr, yi = pl.pallas_call(
        functools.partial(_dft_a_body, n1),
        grid=(B, cols // tcw),
        in_specs=[pl.BlockSpec((2 * n1, n1), lambda b, j: (0, 0)), blk],
        out_specs=[blk, blk],
        out_shape=[yshape, yshape],
        compiler_params=_params(("parallel", "parallel")),
        name="dft_a",
    )(d1, xa)

    yr = yr.reshape(B, n1, n2, W)
    yi = yi.reshape(B, n1, n2, W)
    twc, tws = _cos_sin(n1, n2, S)
    twc = jnp.broadcast_to(twc[:, :, None], (n1, n2, LANES))
    tws = jnp.broadcast_to(tws[:, :, None], (n1, n2, LANES))
    c2, s2 = _cos_sin(n2, n2, n2)
    d2 = jnp.concatenate([jnp.concatenate([c2, s2], axis=1),
                          jnp.concatenate([-s2, c2], axis=1)], axis=0).astype(BF16)
    cg, sg = _cos_sin(HEAD_DIM, HEAD_DIM, HEAD_DIM)
    c64 = _block_diag(jnp.broadcast_to(cg, (N_GROUPS, HEAD_DIM, HEAD_DIM))).astype(BF16)
    s64 = _block_diag(jnp.broadcast_to(sg, (N_GROUPS, HEAD_DIM, HEAD_DIM))).astype(BF16)
    wf = _block_diag(w_fourier).astype(BF16)
    ct = 8
    scale = float(1.0 / np.sqrt(S * HEAD_DIM))
    yblk = pl.BlockSpec((None, ct, n2, W), lambda b, i: (b, i, 0, 0))
    tblk = pl.BlockSpec((ct, n2, LANES), lambda b, i: (i, 0, 0))
    full = lambda r, c: pl.BlockSpec((r, c), lambda b, i: (0, 0))
    out = pl.pallas_call(
        functools.partial(_dft_b_body, ct, scale),
        grid=(B, n1 // ct),
        in_specs=[yblk, yblk, tblk, tblk, full(2 * n2, 2 * n2), full(W, W), full(W, W), full(W, W)],
        out_specs=yblk,
        out_shape=jax.ShapeDtypeStruct((B, n1, n2, W), BF16),
        compiler_params=_params(("parallel", "parallel")),
        name="dft_b",
    )(yr, yi, twc, tws, d2, c64, s64, wf)
    return out.transpose(0, 2, 1, 3).reshape(B, S, W)


def _band_attend(qf, ks, vs, valid, lo_half):
    outs, lses = [], []
    for hh in range(2):
        mine = lo_half if hh == 0 else jnp.logical_not(lo_half)
        qm = jnp.where(mine, qf, 0.0).astype(BF16)
        s = jnp.where(valid, _dot_nt(qm, ks), NEG)
        m = jnp.max(s, axis=1, keepdims=True)
        p = jnp.exp(s - m)
        l = jnp.sum(p, axis=1, keepdims=True)
        outs.append(_dot(p.astype(BF16), vs) * (1.0 / l))
        lses.append(m + jnp.log(l))
    return jnp.where(lo_half, outs[0], outs[1]), jnp.where(lo_half, lses[0], lses[1])


def _attn_masks():
    sub, win = 128, 128 + 2 * HALF_WINDOW
    lane = lax.broadcasted_iota(jnp.int32, (sub, LANES), 1)
    r = lax.broadcasted_iota(jnp.int32, (sub, win), 0)
    c = lax.broadcasted_iota(jnp.int32, (sub, win), 1)
    return lane < HEAD_DIM, (c >= r) & (c <= r + 2 * HALF_WINDOW), c


def _attn_body(tq, L, q_ref, kp_ref, kc_ref, kn_ref, vp_ref, vc_ref, vn_ref, o_ref, lse_ref):
    i = pl.program_id(1)
    sub, win = 128, 128 + 2 * HALF_WINDOW
    lo_half, band, c = _attn_masks()
    for pair in range(ATTN_WIDTH // LANES):
        sl = slice(pair * LANES, (pair + 1) * LANES)
        k2 = jnp.concatenate([kp_ref[:, sl], kc_ref[:, sl], kn_ref[:, sl]], axis=0)
        v2 = jnp.concatenate([vp_ref[:, sl], vc_ref[:, sl], vn_ref[:, sl]], axis=0)
        for sb in range(tq // sub):
            rows = slice(sb * sub, (sb + 1) * sub)
            kpos = i * tq + sb * sub - HALF_WINDOW + c
            valid = band & (kpos >= 0) & (kpos < L)
            o, lse = _band_attend(q_ref[rows, sl].astype(F32), k2[sb * sub:sb * sub + win],
                                  v2[sb * sub:sb * sub + win], valid, lo_half)
            o_ref[rows, sl] = o
            lse_ref[rows, sl] = lse


def _attn_dil_body(R, tq, L, q_ref, kp_ref, kc_ref, kn_ref, vp_ref, vc_ref, vn_ref, o_ref, lse_ref):
    i = pl.program_id(1)
    sub, win = 128, 128 + 2 * HALF_WINDOW
    lo_half, band, c = _attn_masks()
    for r in range(R):
        sl = slice(r * LANES, (r + 1) * LANES)
        k2 = jnp.concatenate([kp_ref[:, sl], kc_ref[:, sl], kn_ref[:, sl]], axis=0)
        v2 = jnp.concatenate([vp_ref[:, sl], vc_ref[:, sl], vn_ref[:, sl]], axis=0)
        for sb in range(tq // sub):
            kpos = i * tq + sb * sub - HALF_WINDOW + c
            valid = band & (kpos >= 0) & (kpos < L)
            o, lse = _band_attend(q_ref[sb * sub:(sb + 1) * sub, sl].astype(F32), k2[sb * sub:sb * sub + win],
                                  v2[sb * sub:sb * sub + win], valid, lo_half)
            dst = pl.ds(R * sb * sub + r, sub, stride=R)
            o_ref[dst, :] = o
            lse_ref[dst, :] = lse


def _attention_natural(q, k, v):
    B, S, W = q.shape
    tq = min(256, S)
    hb = tq // HALF_WINDOW
    nhalo = S // HALF_WINDOW
    ctr = pl.BlockSpec((None, tq, W), lambda b, i: (b, i, 0))
    prev = pl.BlockSpec((None, HALF_WINDOW, W), lambda b, i: (b, jnp.maximum(i * hb - 1, 0), 0))
    nxt = pl.BlockSpec((None, HALF_WINDOW, W), lambda b, i: (b, jnp.minimum((i + 1) * hb, nhalo - 1), 0))
    shp = jax.ShapeDtypeStruct((B, S, W), F32)
    return pl.pallas_call(
        functools.partial(_attn_body, tq, S),
        grid=(B, S // tq),
        in_specs=[ctr, prev, ctr, nxt, prev, ctr, nxt],
        out_specs=[ctr, ctr],
        out_shape=[shp, shp],
        compiler_params=_params(("parallel", "parallel")),
        name="attn_d1",
    )(q, k, k, k, v, v, v)


def _attention_dilated(q, k, v, R, S):
    B, L, _ = q.shape
    tq = min(2048 // R, L)
    hb = tq // HALF_WINDOW
    nhalo = L // HALF_WINDOW
    wb = R * LANES
    ctr = pl.BlockSpec((None, tq, wb), lambda b, i, p: (b, i, p))
    prev = pl.BlockSpec((None, HALF_WINDOW, wb), lambda b, i, p: (b, jnp.maximum(i * hb - 1, 0), p))
    nxt = pl.BlockSpec((None, HALF_WINDOW, wb), lambda b, i, p: (b, jnp.minimum((i + 1) * hb, nhalo - 1), p))
    out = pl.BlockSpec((None, R * tq, LANES), lambda b, i, p: (b, i, p))
    shp = jax.ShapeDtypeStruct((B, S, ATTN_WIDTH), F32)
    return pl.pallas_call(
        functools.partial(_attn_dil_body, R, tq, L),
        grid=(B, L // tq, ATTN_WIDTH // LANES),
        in_specs=[ctr, prev, ctr, nxt, prev, ctr, nxt],
        out_specs=[out, out],
        out_shape=[shp, shp],
        compiler_params=_params(("parallel", "parallel", "parallel")),
        name=f"attn_d{R}",
    )(q, k, k, k, v, v, v)


def _mix_body(x_ref, f_ref, o1_ref, o2_ref, o3_ref, l1_ref, l2_ref, l3_ref, wf_ref, wa_ref, g_ref, wq_ref,
              x1_ref, xn_ref, qp_ref):
    l1, l2, l3 = l1_ref[...], l2_ref[...], l3_ref[...]
    mx = jnp.maximum(jnp.maximum(l1, l2), l3)
    e1, e2, e3 = jnp.exp(l1 - mx), jnp.exp(l2 - mx), jnp.exp(l3 - mx)
    a = (e1 * o1_ref[...] + e2 * o2_ref[...] + e3 * o3_ref[...]) * (1.0 / (e1 + e2 + e3))
    x1 = x_ref[...] + _dot(f_ref[...], wf_ref[...]) + _dot(a.astype(BF16), wa_ref[...])
    x1_ref[...] = x1
    ms = jnp.mean(x1 * x1, axis=-1, keepdims=True)
    xn = (x1 * lax.rsqrt(ms + EPS) * g_ref[...]).astype(BF16)
    xn_ref[...] = xn
    qp_ref[...] = _dot(xn, wq_ref[...]).astype(BF16)


def _mix(x, f_out, os_, lses, w_out_bf, g2, wq_bf):
    N = x.shape[0]
    tm = 256
    row = lambda w: pl.BlockSpec((tm, w), lambda i: (i, 0))
    full = lambda r, c: pl.BlockSpec((r, c), lambda i: (0, 0))
    qw = wq_bf.shape[1]
    return pl.pallas_call(
        _mix_body,
        grid=(N // tm,),
        in_specs=[row(D_MODEL), row(FOURIER_WIDTH)] + [row(ATTN_WIDTH)] * 6
                 + [full(FOURIER_WIDTH, D_MODEL), full(ATTN_WIDTH, D_MODEL), full(1, D_MODEL), full(D_MODEL, qw)],
        out_specs=[row(D_MODEL), row(D_MODEL), row(qw)],
        out_shape=[jax.ShapeDtypeStruct((N, D_MODEL), F32), jax.ShapeDtypeStruct((N, D_MODEL), BF16),
                   jax.ShapeDtypeStruct((N, qw), BF16)],
        compiler_params=_params(("parallel",)),
        name="mix",
    )(x, f_out, *os_, *lses, w_out_bf[:FOURIER_WIDTH], w_out_bf[FOURIER_WIDTH:], g2.reshape(1, D_MODEL), wq_bf)


def _argmax_rows(parts, ids_ascending):
    while len(parts) > 1:
        nxt = []
        for k in range(0, len(parts) - 1, 2):
            (va, ia), (vb, ib) = parts[k], parts[k + 1]
            take_b = (vb > va) if ids_ascending else ((vb > va) | ((vb == va) & (ib < ia)))
            nxt.append((jnp.where(take_b, vb, va), jnp.where(take_b, ib, ia)))
        if len(parts) % 2:
            nxt.append(parts[-1])
        parts = nxt
    v, i = parts[0]
    m = jnp.max(v, axis=0, keepdims=True)
    return m, jnp.min(jnp.where(v == m, i, ID_NONE), axis=0, keepdims=True)


def _extract_topk(s, id_parts, ids_ascending):
    tq = s.shape[1]
    parts = [(s[8 * k:8 * k + 8], ids) for k, ids in enumerate(id_parts)]
    rank = lax.broadcasted_iota(jnp.int32, (TOPK, tq), 0)
    vals = jnp.zeros((TOPK, tq), F32)
    idxs = jnp.zeros((TOPK, tq), F32)
    for kk in range(TOPK):
        m, first = _argmax_rows(parts, ids_ascending)
        parts = [(jnp.where(ids == first, -jnp.inf, v), ids) for v, ids in parts]
        vals = jnp.where(rank == kk, m, vals)
        idxs = jnp.where(rank == kk, first, idxs)
    return vals, idxs


def _select_rows(sel, table):
    out = jnp.zeros_like(sel)
    for a in range(TOPK):
        out = jnp.where(sel == float(a), table[a:a + 1, :], out)
    return out


def _candidate_ids():
    ids = [a * TOPK for a in range(TOPK)]
    for b in range(1, 8):
        ids += [a * TOPK + b if (a + 1) * (b + 1) <= TOPK else ID_NONE for a in range(8)]
    ids += list(range(8, TOPK))
    return np.asarray(ids, np.float32)


def _topk_head(q0, q1, k0, k1, key_ids, cand_ids, cand_id_parts):
    (v1, x1), (v2, x2) = [_extract_topk(_dot_nt(k, q), key_ids, True) for k, q in ((k0, q0), (k1, q1))]
    cand = jnp.concatenate([v1 + v2[0:1, :]] + [v1[0:8, :] + v2[b:b + 1, :] for b in range(1, 8)]
                           + [v1[0:1, :] + v2[8:TOPK, :]], axis=0)
    cand = jnp.where(cand_ids < ID_NONE, cand, -jnp.inf)
    top_s, flat = _extract_topk(cand, cand_id_parts, False)
    a_sel = jnp.floor(flat * (1.0 / TOPK))
    b_sel = flat - a_sel * TOPK
    e = jnp.exp(top_s - top_s[0:1, :])
    return _select_rows(a_sel, x1), _select_rows(b_sel, x2), e * (1.0 / jnp.sum(e, axis=0, keepdims=True))


def _pack_bf16_pair(a, b):
    abits = lax.bitcast_convert_type(a.astype(BF16).astype(F32), jnp.uint32)
    bbits = lax.bitcast_convert_type(b.astype(BF16).astype(F32), jnp.uint32)
    return lax.bitcast_convert_type(abits | (bbits >> 16), jnp.int32)


def _unpack_bf16_pair(word):
    bits = lax.bitcast_convert_type(word, jnp.uint32)
    hi = lax.bitcast_convert_type(bits & jnp.uint32(0xFFFF0000), F32)
    lo = lax.bitcast_convert_type(bits << 16, F32)
    return hi, lo

def _peer_body(tq, nh, x1_ref, xn_ref, qp_ref, keys_ref, ids_ref, ul_ref, uh_ref, vl_ref, vh_ref, gf_ref,
               o_ref, gate_ref, sel_ref, i1_ref, g_ref, i2t_ref):
    i = pl.program_id(0)
    c = pl.program_id(1)
    half_keys = N_KEYS // 2
    slot = lax.rem(i, 2)

    @pl.when(jnp.logical_and(c == 0, i == 0))
    def _():
        gate_ref[...] = jnp.zeros_like(gate_ref)

    @pl.when(jnp.logical_and(c == 0, i > 0))
    def _():
        for grp in range(tq // LANES):
            cols = slice(grp * LANES, (grp + 1) * LANES)
            i1_ref[cols, :] = sel_ref[1 - slot, 0, :, cols].T.astype(jnp.int32)
            i2t_ref[grp] = sel_ref[1 - slot, 1, :, cols].astype(jnp.int32)
            g_ref[cols, :] = sel_ref[1 - slot, 2, :, cols].T
        row = lax.broadcasted_iota(jnp.int32, (N_KEYS, N_KEYS), 0)
        lane = lax.broadcasted_iota(jnp.int32, (N_KEYS, N_KEYS), 1)

        def build(grp, carry):
            base = pl.multiple_of(grp * LANES, LANES)
            i1g, gg = i1_ref[pl.ds(base, LANES), :], g_ref[pl.ds(base, LANES), :]
            i2t = i2t_ref[grp]
            for r in range(LANES):
                p1 = jnp.where(i1g[r:r + 1, :] == row, gg[r:r + 1, :], 0.0).astype(BF16)
                p2 = jnp.where(i2t[:, r:r + 1] == lane, 1.0, 0.0).astype(BF16)
                gt = _dot(p1, p2)
                word = _pack_bf16_pair(gt[:half_keys], gt[half_keys:])
                gate_ref[pl.ds(pl.multiple_of((base + r) * G_PITCH, 8), half_keys), :] = word
            return carry

        lax.fori_loop(0, tq // LANES, build, 0)

    @pl.when(c == 0)
    def _():
        o_ref[...] = jnp.zeros_like(o_ref)

    sub = lax.broadcasted_iota(jnp.int32, (8, tq), 0).astype(F32)
    key_ids = [sub + float(r) for r in range(0, N_KEYS, 8)]
    cand_ids = ids_ref[...]
    cand_id_parts = [cand_ids[r:r + 8] for r in range(0, cand_ids.shape[0], 8)]
    sel = _topk_head(qp_ref[:, :N_KEYS], qp_ref[:, N_KEYS:], keys_ref[2 * c], keys_ref[2 * c + 1],
                     key_ids, cand_ids, cand_id_parts)
    rows = pl.ds(pl.multiple_of(c * TOPK, TOPK), TOPK)
    for k, val in enumerate(sel):
        sel_ref[slot, k, rows, :] = val

    xn = xn_ref[...]
    gates = [_unpack_bf16_pair(gate_ref[pl.ds(c * nh + ii, tq, stride=G_PITCH), :])
             for ii in range(nh)]
    contrib = []
    for half, (u_ref, v_ref) in enumerate(((ul_ref, vl_ref), (uh_ref, vh_ref))):
        a = _dot(xn, u_ref[...])
        act = 0.5 * a * (1.0 + lax.erf(a * float(np.sqrt(0.5))))
        w = jnp.concatenate([(act[:, ii * N_KEYS:(ii + 1) * N_KEYS] * gates[ii][half]).astype(BF16)
                             for ii in range(nh)], axis=1)
        contrib.append(_dot(w, v_ref[...]))
    o_ref[...] += contrib[0] + contrib[1]

    @pl.when(c == pl.num_programs(1) - 1)
    def _():
        y = x1_ref[...] + o_ref[...]
        ms = jnp.mean(y * y, axis=-1, keepdims=True)
        o_ref[...] = y * lax.rsqrt(ms + EPS) * gf_ref[...]


def _peer_u_blocks(u):
    ec = PEER_NH * N_KEYS
    return u.astype(BF16).reshape(N_EXPERTS // ec, ec, D_MODEL).transpose(0, 2, 1)


def _peer(x1, xn, qp, keys_bf, u_blk, v_bf, gf):
    N = x1.shape[0]
    tq = 512
    nh = PEER_NH
    ec = nh * N_KEYS
    n_chunks = (N_KEYS // 2) // nh
    assert n_chunks == PEER_HEADS
    nt = N // tq
    nsel = PEER_HEADS * TOPK
    ids = _candidate_ids()
    ids = jnp.asarray(np.broadcast_to(ids[:, None], (ids.shape[0], tq)))
    prev = lambda i: jnp.maximum(i - 1, 0)
    tok = lambda w: pl.BlockSpec((tq, w), lambda i, c: (prev(i), 0), pipeline_mode=pl.Buffered(1))
    const = lambda shape: pl.BlockSpec(shape, lambda i, c: (0,) * len(shape))
    return pl.pallas_call(
        functools.partial(_peer_body, tq, nh),
        grid=(nt + 1, n_chunks),
        in_specs=[tok(D_MODEL), tok(D_MODEL),
                  pl.BlockSpec((tq, 2 * N_KEYS), lambda i, c: (jnp.minimum(i, nt - 1), c)),
                  const((2 * PEER_HEADS, N_KEYS, N_KEYS)), const(ids.shape),
                  pl.BlockSpec((None, D_MODEL, ec), lambda i, c: (c, 0, 0)),
                  pl.BlockSpec((None, D_MODEL, ec), lambda i, c: (c + n_chunks, 0, 0)),
                  pl.BlockSpec((ec, D_MODEL), lambda i, c: (c, 0)),
                  pl.BlockSpec((ec, D_MODEL), lambda i, c: (c + n_chunks, 0)),
                  const((1, D_MODEL))],
        out_specs=pl.BlockSpec((tq, D_MODEL), lambda i, c: (prev(i), 0)),
        out_shape=jax.ShapeDtypeStruct((N, D_MODEL), F32),
        scratch_shapes=[pltpu.VMEM((tq * G_PITCH, N_KEYS), jnp.int32),
                        pltpu.VMEM((2, 3, nsel, tq), F32),
                        pltpu.VMEM((tq, nsel), jnp.int32),
                        pltpu.VMEM((tq, nsel), F32),
                        pltpu.VMEM((tq // LANES, nsel, LANES), jnp.int32)],
        compiler_params=_params(("arbitrary", "arbitrary")),
        name="peer",
    )(x1, xn, qp, keys_bf, ids, u_blk, u_blk, v_bf, v_bf, gf.reshape(1, D_MODEL))


def _encoder(x, w):
    B, S, _ = x.shape
    f_in, qkv1, qkv4, qkv16 = _proj(x, w["norm1_g"], w["w_in"])
    f_out = _fourier(f_in, w["w_fourier"])
    groups = [_attention_natural(*qkv1), _attention_dilated(*qkv4, DILATIONS[1], S),
              _attention_dilated(*qkv16, DILATIONS[2], S)]
    N = B * S
    flat = lambda a: a.reshape(N, a.shape[-1])
    x1, xn, qp = _mix(flat(x), flat(f_out), [flat(o) for o, _ in groups], [flat(l) for _, l in groups],
                      w["w_out"], w["norm2_g"], w["w_query"])
    y = _peer(x1, xn, qp, w["sub_keys"], w["expert_u"], w["expert_v"], w["final_g"])
    return y.reshape(B, S, D_MODEL)


def kernel(x_prompt, x_sample, norm1_g, w_in, w_fourier, w_out, norm2_g, w_query, sub_keys, expert_u, expert_v, final_g):
    w = {
        "norm1_g": norm1_g[0], "w_in": w_in[0].astype(BF16), "w_fourier": w_fourier[0],
        "w_out": w_out[0].astype(BF16), "norm2_g": norm2_g[0], "w_query": w_query[0].astype(BF16),
        "sub_keys": sub_keys[0].reshape(2 * PEER_HEADS, N_KEYS, N_KEYS).astype(BF16),
        "expert_u": _peer_u_blocks(expert_u[0]), "expert_v": expert_v[0].astype(BF16), "final_g": final_g,
    }
    return (_encoder(x_prompt, w), _encoder(x_sample, w))
```

```python
import functools

import numpy as np
import jax
import jax.numpy as jnp
from jax import lax
from jax.experimental import pallas as pl
from jax.experimental.pallas import tpu as pltpu

F32 = jnp.float32
BF16 = jnp.bfloat16

D_MODEL = 1024
HEAD_DIM = 64
FOURIER_WIDTH = 512
ATTN_WIDTH = 512
IN_WIDTH = 2048
N_GROUPS = 8
DILATIONS = (1, 4, 16)
HALF_WINDOW = 64
ROPE_THETA = 500000.0
ROPE_DIM = 16
PEER_HEADS = 8
N_KEYS = 128
N_EXPERTS = N_KEYS * N_KEYS
TOPK = 16
EPS = 1e-6
NEG = -1e30

LANES = 128
DFT_INNER = 128
G_PITCH = 72
ID_NONE = 1e9
VMEM_LIMIT = 56 * 2**20
PEER_NH = 8


def _params(sem, vmem=None):
    return pltpu.CompilerParams(dimension_semantics=sem, vmem_limit_bytes=vmem or VMEM_LIMIT)


def _dot(a, b):
    return jnp.dot(a, b, preferred_element_type=F32)


def _dot_nt(a, b):
    return lax.dot_general(a, b, (((1,), (1,)), ((), ())), preferred_element_type=F32)


def _proj_body(tm, x_ref, g_ref, w_ref, c_ref, sa_ref, sb_ref,
               f_ref, q_ref, k_ref, v_ref, q4_ref, k4_ref, v4_ref, q16_ref, k16_ref, v16_ref, stage_ref):
    x = x_ref[...]
    ms = jnp.mean(x * x, axis=-1, keepdims=True)
    h = (x * lax.rsqrt(ms + EPS) * g_ref[...]).astype(BF16)
    p = _dot(h, w_ref[...])
    f_ref[...] = p[:, :FOURIER_WIDTH].astype(BF16)
    c, sa, sb = c_ref[...], sa_ref[...], sb_ref[...]
    for j in range(ATTN_WIDTH // LANES):
        lo = j * LANES
        qc = p[:, FOURIER_WIDTH + lo:FOURIER_WIDTH + lo + LANES]
        kc = p[:, FOURIER_WIDTH + ATTN_WIDTH + lo:FOURIER_WIDTH + ATTN_WIDTH + lo + LANES]
        vc = p[:, FOURIER_WIDTH + 2 * ATTN_WIDTH + lo:FOURIER_WIDTH + 2 * ATTN_WIDTH + lo + LANES]
        qr = (qc * c + pltpu.roll(qc, LANES - 8, 1) * sa + pltpu.roll(qc, 8, 1) * sb) * (HEAD_DIM ** -0.5)
        kr = kc * c + pltpu.roll(kc, LANES - 8, 1) * sa + pltpu.roll(kc, 8, 1) * sb
        for a, (val, nat) in enumerate(((qr, q_ref), (kr, k_ref), (vc, v_ref))):
            nat[:, lo:lo + LANES] = val.astype(BF16)
            stage_ref[a, j] = val
    for a, dil in enumerate(((q4_ref, q16_ref), (k4_ref, k16_ref), (v4_ref, v16_ref))):
        for R, d_ref in zip(DILATIONS[1:], dil):
            for j in range(ATTN_WIDTH // LANES):
                for r in range(R):
                    piece = stage_ref[a, j, pl.ds(r, tm // R, stride=R), :]
                    d_ref[:, (j * R + r) * LANES:(j * R + r + 1) * LANES] = piece.astype(BF16)


def _rope_tables(S):
    half = ROPE_DIM // 2
    inv = ROPE_THETA ** (-(jnp.arange(half, dtype=F32) * 2.0) / ROPE_DIM)
    ang = jnp.arange(S, dtype=F32)[:, None] * inv[None, :]
    cos, sin = jnp.cos(ang), jnp.sin(ang)
    one = jnp.ones((S, HEAD_DIM - ROPE_DIM), F32)
    zero = jnp.zeros((S, HEAD_DIM - ROPE_DIM), F32)
    z8 = jnp.zeros((S, half), F32)
    c = jnp.concatenate([cos, cos, one], axis=1)
    sa = jnp.concatenate([-sin, z8, zero], axis=1)
    sb = jnp.concatenate([z8, sin, zero], axis=1)
    rep = LANES // HEAD_DIM
    return jnp.tile(c, (1, rep)), jnp.tile(sa, (1, rep)), jnp.tile(sb, (1, rep))


def _proj(x, g, w_bf):
    B, S, _ = x.shape
    tm = min(512, S)
    c, sa, sb = _rope_tables(S)
    tab = pl.BlockSpec((tm, LANES), lambda b, i: (i, 0))
    out = pl.BlockSpec((None, tm, ATTN_WIDTH), lambda b, i: (b, i, 0))
    shp = jax.ShapeDtypeStruct((B, S, ATTN_WIDTH), BF16)
    dil_specs, dil_shapes = [], []
    for R in DILATIONS[1:]:
        dil_specs.append(pl.BlockSpec((None, tm // R, R * ATTN_WIDTH), lambda b, i: (b, i, 0)))
        dil_shapes.append(jax.ShapeDtypeStruct((B, S // R, R * ATTN_WIDTH), BF16))
    outs = pl.pallas_call(
        functools.partial(_proj_body, tm),
        grid=(B, S // tm),
        in_specs=[pl.BlockSpec((None, tm, D_MODEL), lambda b, i: (b, i, 0)),
                  pl.BlockSpec((1, D_MODEL), lambda b, i: (0, 0)),
                  pl.BlockSpec((D_MODEL, IN_WIDTH), lambda b, i: (0, 0)),
                  tab, tab, tab],
        out_specs=[out] * 4 + [dil_specs[0]] * 3 + [dil_specs[1]] * 3,
        out_shape=[shp] * 4 + [dil_shapes[0]] * 3 + [dil_shapes[1]] * 3,
        scratch_shapes=[pltpu.VMEM((3, ATTN_WIDTH // LANES, tm, LANES), F32)],
        compiler_params=_params(("parallel", "parallel")),
        name="proj",
    )(x, g.reshape(1, D_MODEL), w_bf, c, sa, sb)
    return outs[0], outs[1:4], outs[4:7], outs[7:10]


def _cos_sin(n_rows, n_cols, period):
    i = jnp.arange(n_rows, dtype=jnp.int32)[:, None]
    j = jnp.arange(n_cols, dtype=jnp.int32)[None, :]
    ang = ((i * j) % period).astype(F32) * (2.0 * np.pi / period)
    return jnp.cos(ang), jnp.sin(ang)


def _dft_a_body(n1, d_ref, x_ref, yr_ref, yi_ref):
    y = _dot(d_ref[...], x_ref[...])
    yr_ref[...] = y[:n1].astype(BF16)
    yi_ref[...] = y[n1:].astype(BF16)


def _dft_b_body(ct, scale, yr_ref, yi_ref, tc_ref, ts_ref, d2_ref, c64_ref, s64_ref, wf_ref, o_ref):
    rep = FOURIER_WIDTH // LANES
    for cc in range(ct):
        yr = yr_ref[cc].astype(F32)
        yi = yi_ref[cc].astype(F32)
        tc = jnp.concatenate([tc_ref[cc]] * rep, axis=1)
        ts = jnp.concatenate([ts_ref[cc]] * rep, axis=1)
        zr = yr * tc + yi * ts
        zi = yi * tc - yr * ts
        z = jnp.concatenate([zr, zi], axis=0).astype(BF16)
        xx = _dot(d2_ref[...], z)
        xr = xx[:DFT_INNER].astype(BF16)
        xi = xx[DFT_INNER:].astype(BF16)
        re = (_dot(xr, c64_ref[...]) + _dot(xi, s64_ref[...])) * scale
        o_ref[cc] = _dot(re.astype(BF16), wf_ref[...]).astype(BF16)


def _block_diag(m):
    G, n, _ = m.shape
    eye = jnp.eye(G, dtype=m.dtype)
    return (eye[:, None, :, None] * m[:, :, None, :]).reshape(G * n, G * n)


def _fourier(f_in, w_fourier):
    B, S, W = f_in.shape
    n2 = DFT_INNER
    n1 = S // n2
    cols = n2 * W
    tcw = 4096
    c1, s1 = _cos_sin(n1, n1, n1)
    d1 = jnp.concatenate([c1, -s1], axis=0).astype(BF16)
    xa = f_in.reshape(B, n1, cols)
    blk = pl.BlockSpec((None, n1, tcw), lambda b, j: (b, 0, j))
    yshape = jax.ShapeDtypeStruct((B, n1, cols), BF16)
    yr, yi = pl.pallas_call(
        functools.partial(_dft_a_body, n1),
        grid=(B, cols // tcw),
        in_specs=[pl.BlockSpec((2 * n1, n1), lambda b, j: (0, 0)), blk],
        out_specs=[blk, blk],
        out_shape=[yshape, yshape],
        compiler_params=_params(("parallel", "parallel")),
        name="dft_a",
    )(d1, xa)

    yr = yr.reshape(B, n1, n2, W)
    yi = yi.reshape(B, n1, n2, W)
    twc, tws = _cos_sin(n1, n2, S)
    twc = jnp.broadcast_to(twc[:, :, None], (n1, n2, LANES))
    tws = jnp.broadcast_to(tws[:, :, None], (n1, n2, LANES))
    c2, s2 = _cos_sin(n2, n2, n2)
    d2 = jnp.concatenate([jnp.concatenate([c2, s2], axis=1),
                          jnp.concatenate([-s2, c2], axis=1)], axis=0).astype(BF16)
    cg, sg = _cos_sin(HEAD_DIM, HEAD_DIM, HEAD_DIM)
    c64 = _block_diag(jnp.broadcast_to(cg, (N_GROUPS, HEAD_DIM, HEAD_DIM))).astype(BF16)
    s64 = _block_diag(jnp.broadcast_to(sg, (N_GROUPS, HEAD_DIM, HEAD_DIM))).astype(BF16)
    wf = _block_diag(w_fourier).astype(BF16)
    ct = 8
    scale = float(1.0 / np.sqrt(S * HEAD_DIM))
    yblk = pl.BlockSpec((None, ct, n2, W), lambda b, i: (b, i, 0, 0))
    tblk = pl.BlockSpec((ct, n2, LANES), lambda b, i: (i, 0, 0))
    full = lambda r, c: pl.BlockSpec((r, c), lambda b, i: (0, 0))
    out = pl.pallas_call(
        functools.partial(_dft_b_body, ct, scale),
        grid=(B, n1 // ct),
        in_specs=[yblk, yblk, tblk, tblk, full(2 * n2, 2 * n2), full(W, W), full(W, W), full(W, W)],
        out_specs=yblk,
        out_shape=jax.ShapeDtypeStruct((B, n1, n2, W), BF16),
        compiler_params=_params(("parallel", "parallel")),
        name="dft_b",
    )(yr, yi, twc, tws, d2, c64, s64, wf)
    return out.transpose(0, 2, 1, 3).reshape(B, S, W)


def _band_attend(qf, ks, vs, bias, lo_half):
    outs, lses = [], []
    for hh in range(2):
        mine = lo_half if hh == 0 else jnp.logical_not(lo_half)
        qm = jnp.where(mine, qf, 0.0).astype(BF16)
        s = _dot_nt(qm, ks) + bias
        m = jnp.max(s, axis=1, keepdims=True)
        p = jnp.exp(s - m)
        l = jnp.sum(p, axis=1, keepdims=True)
        outs.append(_dot(p.astype(BF16), vs) * (1.0 / l))
        lses.append(m + jnp.log(l))
    return jnp.where(lo_half, outs[0], outs[1]), jnp.where(lo_half, lses[0], lses[1])


def _attn_bias_terms():
    sub, win = 128, 128 + 2 * HALF_WINDOW
    lane = lax.broadcasted_iota(jnp.int32, (sub, LANES), 1)
    r = lax.broadcasted_iota(jnp.int32, (sub, win), 0)
    c = lax.broadcasted_iota(jnp.int32, (sub, win), 1)
    band = jnp.where((c >= r) & (c <= r + 2 * HALF_WINDOW), 0.0, NEG)
    return lane < HEAD_DIM, band, lax.broadcasted_iota(jnp.int32, (1, win), 1)


def _attn_body(tb, S, *refs):
    ins, (a_ref, og_ref, lg_ref) = refs[:-3], refs[-3:]
    sub, win = 128, 128 + 2 * HALF_WINDOW
    lo_half, band, key = _attn_bias_terms()
    i = pl.program_id(1)
    for g, R in enumerate(DILATIONS):
        q_ref, kp_ref, kc_ref, kn_ref, vp_ref, vc_ref, vn_ref = ins[7 * g:7 * g + 7]
        tq, L = tb // R, S // R
        n_sub = tq // sub
        for r in range(R):
            sl = slice(r * LANES, (r + 1) * LANES)
            k2 = jnp.concatenate([kp_ref[:, sl], kc_ref[:, sl], kn_ref[:, sl]], axis=0)
            v2 = jnp.concatenate([vp_ref[:, sl], vc_ref[:, sl], vn_ref[:, sl]], axis=0)
            for sb in range(n_sub):
                bias = band
                if sb == 0 or sb == n_sub - 1:
                    kpos = i * tq + sb * sub - HALF_WINDOW + key
                    bias = band + jnp.where((kpos >= 0) & (kpos < L), 0.0, NEG)
                o, lse = _band_attend(q_ref[sb * sub:(sb + 1) * sub, sl].astype(F32),
                                      k2[sb * sub:sb * sub + win], v2[sb * sub:sb * sub + win], bias, lo_half)
                dst = pl.ds(R * sb * sub + r, sub, stride=R)
                og_ref[g, dst, :] = o
                lg_ref[g, dst, :] = lse
    chunk = 256
    for t in range(0, tb, chunk):
        rows = slice(t, t + chunk)
        l1, l2, l3 = lg_ref[0, rows, :], lg_ref[1, rows, :], lg_ref[2, rows, :]
        mx = jnp.maximum(jnp.maximum(l1, l2), l3)
        e1, e2, e3 = jnp.exp(l1 - mx), jnp.exp(l2 - mx), jnp.exp(l3 - mx)
        a = (e1 * og_ref[0, rows, :] + e2 * og_ref[1, rows, :] + e3 * og_ref[2, rows, :]) * (1.0 / (e1 + e2 + e3))
        a_ref[rows, :] = a.astype(BF16)


def _halo_index(hb, nhalo, offset):
    return lambda b, i, p: (b, jnp.clip(i * hb + offset, 0, nhalo - 1), p)


def _attention(qkv_groups, S):
    B = qkv_groups[0][0].shape[0]
    tb = min(2048, S)
    specs, args = [], []
    for (q, k, v), R in zip(qkv_groups, DILATIONS):
        tq, L, wb = tb // R, S // R, R * LANES
        hb, nhalo = tq // HALF_WINDOW, L // HALF_WINDOW
        ctr = pl.BlockSpec((None, tq, wb), lambda b, i, p: (b, i, p))
        prev = pl.BlockSpec((None, HALF_WINDOW, wb), _halo_index(hb, nhalo, -1))
        nxt = pl.BlockSpec((None, HALF_WINDOW, wb), _halo_index(hb, nhalo, hb))
        specs += [ctr, prev, ctr, nxt, prev, ctr, nxt]
        args += [q, k, k, k, v, v, v]
    return pl.pallas_call(
        functools.partial(_attn_body, tb, S),
        grid=(B, S // tb, ATTN_WIDTH // LANES),
        in_specs=specs,
        out_specs=pl.BlockSpec((None, tb, LANES), lambda b, i, p: (b, i, p)),
        out_shape=jax.ShapeDtypeStruct((B, S, ATTN_WIDTH), BF16),
        scratch_shapes=[pltpu.VMEM((len(DILATIONS), tb, LANES), F32), pltpu.VMEM((len(DILATIONS), tb, LANES), F32)],
        compiler_params=_params(("parallel", "parallel", "parallel")),
        name="attn",
    )(*args)


def _mix_body(x_ref, f_ref, a_ref, wf_ref, wa_ref, g_ref, wq_ref, x1_ref, xn_ref, qp_ref):
    x1 = x_ref[...] + _dot(f_ref[...], wf_ref[...]) + _dot(a_ref[...], wa_ref[...])
    x1_ref[...] = x1
    ms = jnp.mean(x1 * x1, axis=-1, keepdims=True)
    xn = (x1 * lax.rsqrt(ms + EPS) * g_ref[...]).astype(BF16)
    xn_ref[...] = xn
    qp_ref[...] = _dot(xn, wq_ref[...]).astype(BF16)


def _mix(x, f_out, a_out, w_out_bf, g2, wq_bf):
    N = x.shape[0]
    tm = 256
    row = lambda w: pl.BlockSpec((tm, w), lambda i: (i, 0))
    full = lambda r, c: pl.BlockSpec((r, c), lambda i: (0, 0))
    qw = wq_bf.shape[1]
    return pl.pallas_call(
        _mix_body,
        grid=(N // tm,),
        in_specs=[row(D_MODEL), row(FOURIER_WIDTH), row(ATTN_WIDTH),
                  full(FOURIER_WIDTH, D_MODEL), full(ATTN_WIDTH, D_MODEL), full(1, D_MODEL), full(D_MODEL, qw)],
        out_specs=[row(D_MODEL), row(D_MODEL), row(qw)],
        out_shape=[jax.ShapeDtypeStruct((N, D_MODEL), F32), jax.ShapeDtypeStruct((N, D_MODEL), BF16),
                   jax.ShapeDtypeStruct((N, qw), BF16)],
        compiler_params=_params(("parallel",)),
        name="mix",
    )(x, f_out, a_out, w_out_bf[:FOURIER_WIDTH], w_out_bf[FOURIER_WIDTH:], g2.reshape(1, D_MODEL), wq_bf)


def _argmax_rows(parts, ids_ascending):
    while len(parts) > 1:
        nxt = []
        for k in range(0, len(parts) - 1, 2):
            (va, ia), (vb, ib) = parts[k], parts[k + 1]
            take_b = (vb > va) if ids_ascending else ((vb > va) | ((vb == va) & (ib < ia)))
            nxt.append((jnp.where(take_b, vb, va), jnp.where(take_b, ib, ia)))
        if len(parts) % 2:
            nxt.append(parts[-1])
        parts = nxt
    v, i = parts[0]
    m = jnp.max(v, axis=0, keepdims=True)
    return m, jnp.min(jnp.where(v == m, i, ID_NONE), axis=0, keepdims=True)


def _extract_topk(s, id_parts, ids_ascending):
    tq = s.shape[1]
    parts = [(s[8 * k:8 * k + 8], ids) for k, ids in enumerate(id_parts)]
    rank = lax.broadcasted_iota(jnp.int32, (TOPK, tq), 0)
    vals = jnp.zeros((TOPK, tq), F32)
    idxs = jnp.zeros((TOPK, tq), F32)
    for kk in range(TOPK):
        m, first = _argmax_rows(parts, ids_ascending)
        parts = [(jnp.where(ids == first, -jnp.inf, v), ids) for v, ids in parts]
        vals = jnp.where(rank == kk, m, vals)
        idxs = jnp.where(rank == kk, first, idxs)
    return vals, idxs


def _select_rows(sel, table):
    out = jnp.zeros_like(sel)
    for a in range(TOPK):
        out = jnp.where(sel == float(a), table[a:a + 1, :], out)
    return out


def _candidate_ids():
    ids = [a * TOPK for a in range(TOPK)]
    for b in range(1, 8):
        ids += [a * TOPK + b if (a + 1) * (b + 1) <= TOPK else ID_NONE for a in range(8)]
    ids += list(range(8, TOPK))
    return np.asarray(ids, np.float32)


def _topk_head(q0, q1, k0, k1, key_ids, cand_ids, cand_id_parts):
    (v1, x1), (v2, x2) = [_extract_topk(_dot_nt(k, q), key_ids, True) for k, q in ((k0, q0), (k1, q1))]
    cand = jnp.concatenate([v1 + v2[0:1, :]] + [v1[0:8, :] + v2[b:b + 1, :] for b in range(1, 8)]
                           + [v1[0:1, :] + v2[8:TOPK, :]], axis=0)
    cand = jnp.where(cand_ids < ID_NONE, cand, -jnp.inf)
    top_s, flat = _extract_topk(cand, cand_id_parts, False)
    a_sel = jnp.floor(flat * (1.0 / TOPK))
    b_sel = flat - a_sel * TOPK
    e = jnp.exp(top_s - top_s[0:1, :])
    return _select_rows(a_sel, x1), _select_rows(b_sel, x2), e * (1.0 / jnp.sum(e, axis=0, keepdims=True))


def _pack_bf16_pair(a, b):
    abits = lax.bitcast_convert_type(a.astype(BF16).astype(F32), jnp.uint32)
    bbits = lax.bitcast_convert_type(b.astype(BF16).astype(F32), jnp.uint32)
    return lax.bitcast_convert_type(abits | (bbits >> 16), jnp.int32)


def _unpack_bf16_pair(word):
    bits = lax.bitcast_convert_type(word, jnp.uint32)
    hi = lax.bitcast_convert_type(bits & jnp.uint32(0xFFFF0000), F32)
    lo = lax.bitcast_convert_type(bits << 16, F32)
    return hi, lo

def _peer_body(tq, nh, x1_ref, xn_ref, qp_ref, keys_ref, ids_ref, ul_ref, uh_ref, vl_ref, vh_ref, gf_ref,
               o_ref, gate_ref, sel_ref, i1_ref, g_ref, i2t_ref):
    i = pl.program_id(0)
    c = pl.program_id(1)
    half_keys = N_KEYS // 2
    slot = lax.rem(i, 2)

    @pl.when(jnp.logical_and(c == 0, i == 0))
    def _():
        gate_ref[...] = jnp.zeros_like(gate_ref)

    @pl.when(jnp.logical_and(c == 0, i > 0))
    def _():
        for grp in range(tq // LANES):
            cols = slice(grp * LANES, (grp + 1) * LANES)
            i1_ref[cols, :] = sel_ref[1 - slot, 0, :, cols].T.astype(jnp.int32)
            i2t_ref[grp] = sel_ref[1 - slot, 1, :, cols].astype(jnp.int32)
            g_ref[cols, :] = sel_ref[1 - slot, 2, :, cols].T
        row = lax.broadcasted_iota(jnp.int32, (N_KEYS, N_KEYS), 0)
        lane = lax.broadcasted_iota(jnp.int32, (N_KEYS, N_KEYS), 1)

        def build(grp, carry):
            base = pl.multiple_of(grp * LANES, LANES)
            i1g, gg = i1_ref[pl.ds(base, LANES), :], g_ref[pl.ds(base, LANES), :]
            i2t = i2t_ref[grp]
            for r in range(LANES):
                p1 = jnp.where(i1g[r:r + 1, :] == row, gg[r:r + 1, :], 0.0).astype(BF16)
                p2 = jnp.where(i2t[:, r:r + 1] == lane, 1.0, 0.0).astype(BF16)
                gt = _dot(p1, p2)
                word = _pack_bf16_pair(gt[:half_keys], gt[half_keys:])
                gate_ref[pl.ds(pl.multiple_of((base + r) * G_PITCH, 8), half_keys), :] = word
            return carry

        lax.fori_loop(0, tq // LANES, build, 0)

    @pl.when(c == 0)
    def _():
        o_ref[...] = jnp.zeros_like(o_ref)

    sub = lax.broadcasted_iota(jnp.int32, (8, tq), 0).astype(F32)
    key_ids = [sub + float(r) for r in range(0, N_KEYS, 8)]
    cand_ids = ids_ref[...]
    cand_id_parts = [cand_ids[r:r + 8] for r in range(0, cand_ids.shape[0], 8)]
    sel = _topk_head(qp_ref[:, :N_KEYS], qp_ref[:, N_KEYS:], keys_ref[2 * c], keys_ref[2 * c + 1],
                     key_ids, cand_ids, cand_id_parts)
    rows = pl.ds(pl.multiple_of(c * TOPK, TOPK), TOPK)
    for k, val in enumerate(sel):
        sel_ref[slot, k, rows, :] = val

    xn = xn_ref[...]
    gates = [_unpack_bf16_pair(gate_ref[pl.ds(c * nh + ii, tq, stride=G_PITCH), :])
             for ii in range(nh)]
    contrib = []
    for half, (u_ref, v_ref) in enumerate(((ul_ref, vl_ref), (uh_ref, vh_ref))):
        a = _dot(xn, u_ref[...])
        act = 0.5 * a * (1.0 + lax.erf(a * float(np.sqrt(0.5))))
        w = jnp.concatenate([(act[:, ii * N_KEYS:(ii + 1) * N_KEYS] * gates[ii][half]).astype(BF16)
                             for ii in range(nh)], axis=1)
        contrib.append(_dot(w, v_ref[...]))
    o_ref[...] += contrib[0] + contrib[1]

    @pl.when(c == pl.num_programs(1) - 1)
    def _():
        y = x1_ref[...] + o_ref[...]
        ms = jnp.mean(y * y, axis=-1, keepdims=True)
        o_ref[...] = y * lax.rsqrt(ms + EPS) * gf_ref[...]


def _peer_u_blocks(u):
    ec = PEER_NH * N_KEYS
    return u.astype(BF16).reshape(N_EXPERTS // ec, ec, D_MODEL).transpose(0, 2, 1)


def _peer(x1, xn, qp, keys_bf, u_blk, v_bf, gf):
    N = x1.shape[0]
    tq = 512
    nh = PEER_NH
    ec = nh * N_KEYS
    n_chunks = (N_KEYS // 2) // nh
    assert n_chunks == PEER_HEADS
    nt = N // tq
    nsel = PEER_HEADS * TOPK
    ids = _candidate_ids()
    ids = jnp.asarray(np.broadcast_to(ids[:, None], (ids.shape[0], tq)))
    prev = lambda i: jnp.maximum(i - 1, 0)
    tok = lambda w: pl.BlockSpec((tq, w), lambda i, c: (prev(i), 0), pipeline_mode=pl.Buffered(1))
    const = lambda shape: pl.BlockSpec(shape, lambda i, c: (0,) * len(shape))
    return pl.pallas_call(
        functools.partial(_peer_body, tq, nh),
        grid=(nt + 1, n_chunks),
        in_specs=[tok(D_MODEL), tok(D_MODEL),
                  pl.BlockSpec((tq, 2 * N_KEYS), lambda i, c: (jnp.minimum(i, nt - 1), c)),
                  const((2 * PEER_HEADS, N_KEYS, N_KEYS)), const(ids.shape),
                  pl.BlockSpec((None, D_MODEL, ec), lambda i, c: (c, 0, 0)),
                  pl.BlockSpec((None, D_MODEL, ec), lambda i, c: (c + n_chunks, 0, 0)),
                  pl.BlockSpec((ec, D_MODEL), lambda i, c: (c, 0)),
                  pl.BlockSpec((ec, D_MODEL), lambda i, c: (c + n_chunks, 0)),
                  const((1, D_MODEL))],
        out_specs=pl.BlockSpec((tq, D_MODEL), lambda i, c: (prev(i), 0)),
        out_shape=jax.ShapeDtypeStruct((N, D_MODEL), F32),
        scratch_shapes=[pltpu.VMEM((tq * G_PITCH, N_KEYS), jnp.int32),
                        pltpu.VMEM((2, 3, nsel, tq), F32),
                        pltpu.VMEM((tq, nsel), jnp.int32),
                        pltpu.VMEM((tq, nsel), F32),
                        pltpu.VMEM((tq // LANES, nsel, LANES), jnp.int32)],
        compiler_params=_params(("arbitrary", "arbitrary")),
        name="peer",
    )(x1, xn, qp, keys_bf, ids, u_blk, u_blk, v_bf, v_bf, gf.reshape(1, D_MODEL))


def _encoder(x, w):
    B, S, _ = x.shape
    f_in, qkv1, qkv4, qkv16 = _proj(x, w["norm1_g"], w["w_in"])
    f_out = _fourier(f_in, w["w_fourier"])
    a_out = _attention([qkv1, qkv4, qkv16], S)
    N = B * S
    flat = lambda a: a.reshape(N, a.shape[-1])
    x1, xn, qp = _mix(flat(x), flat(f_out), flat(a_out), w["w_out"], w["norm2_g"], w["w_query"])
    y = _peer(x1, xn, qp, w["sub_keys"], w["expert_u"], w["expert_v"], w["final_g"])
    return y.reshape(B, S, D_MODEL)


def kernel(x_prompt, x_sample, norm1_g, w_in, w_fourier, w_out, norm2_g, w_query, sub_keys, expert_u, expert_v, final_g):
    w = {
        "norm1_g": norm1_g[0], "w_in": w_in[0].astype(BF16), "w_fourier": w_fourier[0],
        "w_out": w_out[0].astype(BF16), "norm2_g": norm2_g[0], "w_query": w_query[0].astype(BF16),
        "sub_keys": sub_keys[0].reshape(2 * PEER_HEADS, N_KEYS, N_KEYS).astype(BF16),
        "expert_u": _peer_u_blocks(expert_u[0]), "expert_v": expert_v[0].astype(BF16), "final_g": final_g,
    }
    return (_encoder(x_prompt, w), _encoder(x_sample, w))
```

```python
import functools

import numpy as np
import jax
import jax.numpy as jnp
from jax import lax
from jax.experimental import pallas as pl
from jax.experimental.pallas import tpu as pltpu

F32 = jnp.float32
BF16 = jnp.bfloat16

D_MODEL = 1024
HEAD_DIM = 64
FOURIER_WIDTH = 512
ATTN_WIDTH = 512
IN_WIDTH = 2048
N_GROUPS = 8
DILATIONS = (1, 4, 16)
HALF_WINDOW = 64
ROPE_THETA = 500000.0
ROPE_DIM = 16
PEER_HEADS = 8
N_KEYS = 128
N_EXPERTS = N_KEYS * N_KEYS
TOPK = 16
EPS = 1e-6
NEG = -1e30

LANES = 128
DFT_INNER = 128
G_PITCH = 72
ID_NONE = 1e9
VMEM_LIMIT = 56 * 2**20
PEER_NH = 8


def _params(sem, vmem=None):
    return pltpu.CompilerParams(dimension_semantics=sem, vmem_limit_bytes=vmem or VMEM_LIMIT)


def _dot(a, b):
    return jnp.dot(a, b, preferred_element_type=F32)


def _dot_nt(a, b):
    return lax.dot_general(a, b, (((1,), (1,)), ((), ())), preferred_element_type=F32)


def _proj_body(tm, x_ref, g_ref, w_ref, c_ref, sa_ref, sb_ref,
               f_ref, q_ref, k_ref, v_ref, q4_ref, k4_ref, v4_ref, q16_ref, k16_ref, v16_ref, stage_ref):
    x = x_ref[...]
    ms = jnp.mean(x * x, axis=-1, keepdims=True)
    h = (x * lax.rsqrt(ms + EPS) * g_ref[...]).astype(BF16)
    p = _dot(h, w_ref[...])
    f_ref[...] = p[:, :FOURIER_WIDTH].astype(BF16)
    c, sa, sb = c_ref[...], sa_ref[...], sb_ref[...]
    for j in range(ATTN_WIDTH // LANES):
        lo = j * LANES
        qc = p[:, FOURIER_WIDTH + lo:FOURIER_WIDTH + lo + LANES]
        kc = p[:, FOURIER_WIDTH + ATTN_WIDTH + lo:FOURIER_WIDTH + ATTN_WIDTH + lo + LANES]
        vc = p[:, FOURIER_WIDTH + 2 * ATTN_WIDTH + lo:FOURIER_WIDTH + 2 * ATTN_WIDTH + lo + LANES]
        qr = (qc * c + pltpu.roll(qc, LANES - 8, 1) * sa + pltpu.roll(qc, 8, 1) * sb) * (HEAD_DIM ** -0.5)
        kr = kc * c + pltpu.roll(kc, LANES - 8, 1) * sa + pltpu.roll(kc, 8, 1) * sb
        for a, (val, nat) in enumerate(((qr, q_ref), (kr, k_ref), (vc, v_ref))):
            nat[:, lo:lo + LANES] = val.astype(BF16)
            stage_ref[a, j] = val
    for a, dil in enumerate(((q4_ref, q16_ref), (k4_ref, k16_ref), (v4_ref, v16_ref))):
        for R, d_ref in zip(DILATIONS[1:], dil):
            for j in range(ATTN_WIDTH // LANES):
                for r in range(R):
                    piece = stage_ref[a, j, pl.ds(r, tm // R, stride=R), :]
                    d_ref[:, (j * R + r) * LANES:(j * R + r + 1) * LANES] = piece.astype(BF16)


def _rope_tables(S):
    half = ROPE_DIM // 2
    inv = ROPE_THETA ** (-(jnp.arange(half, dtype=F32) * 2.0) / ROPE_DIM)
    ang = jnp.arange(S, dtype=F32)[:, None] * inv[None, :]
    cos, sin = jnp.cos(ang), jnp.sin(ang)
    one = jnp.ones((S, HEAD_DIM - ROPE_DIM), F32)
    zero = jnp.zeros((S, HEAD_DIM - ROPE_DIM), F32)
    z8 = jnp.zeros((S, half), F32)
    c = jnp.concatenate([cos, cos, one], axis=1)
    sa = jnp.concatenate([-sin, z8, zero], axis=1)
    sb = jnp.concatenate([z8, sin, zero], axis=1)
    rep = LANES // HEAD_DIM
    return jnp.tile(c, (1, rep)), jnp.tile(sa, (1, rep)), jnp.tile(sb, (1, rep))


def _proj(x, g, w_bf):
    B, S, _ = x.shape
    tm = min(512, S)
    c, sa, sb = _rope_tables(S)
    tab = pl.BlockSpec((tm, LANES), lambda b, i: (i, 0))
    out = pl.BlockSpec((None, tm, ATTN_WIDTH), lambda b, i: (b, i, 0))
    shp = jax.ShapeDtypeStruct((B, S, ATTN_WIDTH), BF16)
    dil_specs, dil_shapes = [], []
    for R in DILATIONS[1:]:
        dil_specs.append(pl.BlockSpec((None, tm // R, R * ATTN_WIDTH), lambda b, i: (b, i, 0)))
        dil_shapes.append(jax.ShapeDtypeStruct((B, S // R, R * ATTN_WIDTH), BF16))
    outs = pl.pallas_call(
        functools.partial(_proj_body, tm),
        grid=(B, S // tm),
        in_specs=[pl.BlockSpec((None, tm, D_MODEL), lambda b, i: (b, i, 0)),
                  pl.BlockSpec((1, D_MODEL), lambda b, i: (0, 0)),
                  pl.BlockSpec((D_MODEL, IN_WIDTH), lambda b, i: (0, 0)),
                  tab, tab, tab],
        out_specs=[out] * 4 + [dil_specs[0]] * 3 + [dil_specs[1]] * 3,
        out_shape=[shp] * 4 + [dil_shapes[0]] * 3 + [dil_shapes[1]] * 3,
        scratch_shapes=[pltpu.VMEM((3, ATTN_WIDTH // LANES, tm, LANES), F32)],
        compiler_params=_params(("parallel", "parallel")),
        name="proj",
    )(x, g.reshape(1, D_MODEL), w_bf, c, sa, sb)
    return outs[0], outs[1:4], outs[4:7], outs[7:10]


def _cos_sin(n_rows, n_cols, period):
    i = jnp.arange(n_rows, dtype=jnp.int32)[:, None]
    j = jnp.arange(n_cols, dtype=jnp.int32)[None, :]
    ang = ((i * j) % period).astype(F32) * (2.0 * np.pi / period)
    return jnp.cos(ang), jnp.sin(ang)


def _dft_a_body(n1, d_ref, x_ref, yr_ref, yi_ref):
    y = _dot(d_ref[...], x_ref[...])
    yr_ref[...] = y[:n1].astype(BF16)
    yi_ref[...] = y[n1:].astype(BF16)


def _dft_b_body(ct, scale, yr_ref, yi_ref, tc_ref, ts_ref, d2_ref, c64_ref, s64_ref, wf_ref, o_ref):
    rep = FOURIER_WIDTH // LANES
    for cc in range(ct):
        yr = yr_ref[cc].astype(F32)
        yi = yi_ref[cc].astype(F32)
        tc = jnp.concatenate([tc_ref[cc]] * rep, axis=1)
        ts = jnp.concatenate([ts_ref[cc]] * rep, axis=1)
        zr = yr * tc + yi * ts
        zi = yi * tc - yr * ts
        z = jnp.concatenate([zr, zi], axis=0).astype(BF16)
        xx = _dot(d2_ref[...], z)
        xr = xx[:DFT_INNER].astype(BF16)
        xi = xx[DFT_INNER:].astype(BF16)
        re = (_dot(xr, c64_ref[...]) + _dot(xi, s64_ref[...])) * scale
        o_ref[cc] = _dot(re.astype(BF16), wf_ref[...]).astype(BF16)


def _block_diag(m):
    G, n, _ = m.shape
    eye = jnp.eye(G, dtype=m.dtype)
    return (eye[:, None, :, None] * m[:, :, None, :]).reshape(G * n, G * n)


def _fourier(f_in, w_fourier):
    B, S, W = f_in.shape
    n2 = DFT_INNER
    n1 = S // n2
    cols = n2 * W
    tcw = 4096
    c1, s1 = _cos_sin(n1, n1, n1)
    d1 = jnp.concatenate([c1, -s1], axis=0).astype(BF16)
    xa = f_in.reshape(B, n1, cols)
    blk = pl.BlockSpec((None, n1, tcw), lambda b, j: (b, 0, j))
    yshape = jax.ShapeDtypeStruct((B, n1, cols), BF16)
    yr, yi = pl.pallas_call(
        functools.partial(_dft_a_body, n1),
        grid=(B, cols // tcw),
        in_specs=[pl.BlockSpec((2 * n1, n1), lambda b, j: (0, 0)), blk],
        out_specs=[blk, blk],
        out_shape=[yshape, yshape],
        compiler_params=_params(("parallel", "parallel")),
        name="dft_a",
    )(d1, xa)

    yr = yr.reshape(B, n1, n2, W)
    yi = yi.reshape(B, n1, n2, W)
    twc, tws = _cos_sin(n1, n2, S)
    twc = jnp.broadcast_to(twc[:, :, None], (n1, n2, LANES))
    tws = jnp.broadcast_to(tws[:, :, None], (n1, n2, LANES))
    c2, s2 = _cos_sin(n2, n2, n2)
    d2 = jnp.concatenate([jnp.concatenate([c2, s2], axis=1),
                          jnp.concatenate([-s2, c2], axis=1)], axis=0).astype(BF16)
    cg, sg = _cos_sin(HEAD_DIM, HEAD_DIM, HEAD_DIM)
    c64 = _block_diag(jnp.broadcast_to(cg, (N_GROUPS, HEAD_DIM, HEAD_DIM))).astype(BF16)
    s64 = _block_diag(jnp.broadcast_to(sg, (N_GROUPS, HEAD_DIM, HEAD_DIM))).astype(BF16)
    wf = _block_diag(w_fourier).astype(BF16)
    ct = 8
    scale = float(1.0 / np.sqrt(S * HEAD_DIM))
    yblk = pl.BlockSpec((None, ct, n2, W), lambda b, i: (b, i, 0, 0))
    tblk = pl.BlockSpec((ct, n2, LANES), lambda b, i: (i, 0, 0))
    full = lambda r, c: pl.BlockSpec((r, c), lambda b, i: (0, 0))
    out = pl.pallas_call(
        functools.partial(_dft_b_body, ct, scale),
        grid=(B, n1 // ct),
        in_specs=[yblk, yblk, tblk, tblk, full(2 * n2, 2 * n2), full(W, W), full(W, W), full(W, W)],
        out_specs=yblk,
        out_shape=jax.ShapeDtypeStruct((B, n1, n2, W), BF16),
        compiler_params=_params(("parallel", "parallel")),
        name="dft_b",
    )(yr, yi, twc, tws, d2, c64, s64, wf)
    return out.transpose(0, 2, 1, 3).reshape(B, S, W)


def _band_attend(qf, ks, vs, bias, lo_half):
    outs, lses = [], []
    for hh in range(2):
        mine = lo_half if hh == 0 else jnp.logical_not(lo_half)
        qm = jnp.where(mine, qf, 0.0).astype(BF16)
        s = _dot_nt(qm, ks) + bias
        m = jnp.max(s, axis=1, keepdims=True)
        p = jnp.exp(s - m)
        l = jnp.sum(p, axis=1, keepdims=True)
        outs.append(_dot(p.astype(BF16), vs) * (1.0 / l))
        lses.append(m + jnp.log(l))
    return jnp.where(lo_half, outs[0], outs[1]), jnp.where(lo_half, lses[0], lses[1])


def _attn_bias_terms():
    sub, win = 128, 128 + 2 * HALF_WINDOW
    lane = lax.broadcasted_iota(jnp.int32, (sub, LANES), 1)
    r = lax.broadcasted_iota(jnp.int32, (sub, win), 0)
    c = lax.broadcasted_iota(jnp.int32, (sub, win), 1)
    band = jnp.where((c >= r) & (c <= r + 2 * HALF_WINDOW), 0.0, NEG)
    return lane < HEAD_DIM, band, lax.broadcasted_iota(jnp.int32, (1, win), 1)


def _attn_body(tb, S, *refs):
    ins, (a_ref, og_ref, lg_ref) = refs[:-3], refs[-3:]
    sub, win = 128, 128 + 2 * HALF_WINDOW
    lo_half, band, key = _attn_bias_terms()
    i = pl.program_id(1)
    for g, R in enumerate(DILATIONS):
        q_ref, kp_ref, kc_ref, kn_ref, vp_ref, vc_ref, vn_ref = ins[7 * g:7 * g + 7]
        tq, L = tb // R, S // R
        n_sub = tq // sub
        for r in range(R):
            sl = slice(r * LANES, (r + 1) * LANES)
            k2 = jnp.concatenate([kp_ref[:, sl], kc_ref[:, sl], kn_ref[:, sl]], axis=0)
            v2 = jnp.concatenate([vp_ref[:, sl], vc_ref[:, sl], vn_ref[:, sl]], axis=0)
            for sb in range(n_sub):
                bias = band
                if sb == 0 or sb == n_sub - 1:
                    kpos = i * tq + sb * sub - HALF_WINDOW + key
                    bias = band + jnp.where((kpos >= 0) & (kpos < L), 0.0, NEG)
                o, lse = _band_attend(q_ref[sb * sub:(sb + 1) * sub, sl].astype(F32),
                                      k2[sb * sub:sb * sub + win], v2[sb * sub:sb * sub + win], bias, lo_half)
                dst = pl.ds(R * sb * sub + r, sub, stride=R)
                og_ref[g, dst, :] = o
                lg_ref[g, dst, :] = lse
    chunk = 256
    for t in range(0, tb, chunk):
        rows = slice(t, t + chunk)
        l1, l2, l3 = lg_ref[0, rows, :], lg_ref[1, rows, :], lg_ref[2, rows, :]
        mx = jnp.maximum(jnp.maximum(l1, l2), l3)
        e1, e2, e3 = jnp.exp(l1 - mx), jnp.exp(l2 - mx), jnp.exp(l3 - mx)
        a = (e1 * og_ref[0, rows, :] + e2 * og_ref[1, rows, :] + e3 * og_ref[2, rows, :]) * (1.0 / (e1 + e2 + e3))
        a_ref[rows, :] = a.astype(BF16)


def _halo_index(hb, nhalo, offset):
    return lambda b, i, p: (b, jnp.clip(i * hb + offset, 0, nhalo - 1), p)


def _attention(qkv_groups, S):
    B = qkv_groups[0][0].shape[0]
    tb = min(2048, S)
    specs, args = [], []
    for (q, k, v), R in zip(qkv_groups, DILATIONS):
        tq, L, wb = tb // R, S // R, R * LANES
        hb, nhalo = tq // HALF_WINDOW, L // HALF_WINDOW
        ctr = pl.BlockSpec((None, tq, wb), lambda b, i, p: (b, i, p))
        prev = pl.BlockSpec((None, HALF_WINDOW, wb), _halo_index(hb, nhalo, -1))
        nxt = pl.BlockSpec((None, HALF_WINDOW, wb), _halo_index(hb, nhalo, hb))
        specs += [ctr, prev, ctr, nxt, prev, ctr, nxt]
        args += [q, k, k, k, v, v, v]
    return pl.pallas_call(
        functools.partial(_attn_body, tb, S),
        grid=(B, S // tb, ATTN_WIDTH // LANES),
        in_specs=specs,
        out_specs=pl.BlockSpec((None, tb, LANES), lambda b, i, p: (b, i, p)),
        out_shape=jax.ShapeDtypeStruct((B, S, ATTN_WIDTH), BF16),
        scratch_shapes=[pltpu.VMEM((len(DILATIONS), tb, LANES), F32), pltpu.VMEM((len(DILATIONS), tb, LANES), F32)],
        compiler_params=_params(("parallel", "parallel", "parallel")),
        name="attn",
    )(*args)


def _mix_body(x_ref, f_ref, a_ref, wf_ref, wa_ref, g_ref, wq_ref, x1_ref, xn_ref, qp_ref):
    x1 = x_ref[...] + _dot(f_ref[...], wf_ref[...]) + _dot(a_ref[...], wa_ref[...])
    x1_ref[...] = x1
    ms = jnp.mean(x1 * x1, axis=-1, keepdims=True)
    xn = (x1 * lax.rsqrt(ms + EPS) * g_ref[...]).astype(BF16)
    xn_ref[...] = xn
    qp_ref[...] = _dot(xn, wq_ref[...]).astype(BF16)


def _mix(x, f_out, a_out, w_out_bf, g2, wq_bf):
    N = x.shape[0]
    tm = 512
    row = lambda w: pl.BlockSpec((tm, w), lambda i: (i, 0))
    full = lambda r, c: pl.BlockSpec((r, c), lambda i: (0, 0))
    qw = wq_bf.shape[1]
    return pl.pallas_call(
        _mix_body,
        grid=(N // tm,),
        in_specs=[row(D_MODEL), row(FOURIER_WIDTH), row(ATTN_WIDTH),
                  full(FOURIER_WIDTH, D_MODEL), full(ATTN_WIDTH, D_MODEL), full(1, D_MODEL), full(D_MODEL, qw)],
        out_specs=[row(D_MODEL), row(D_MODEL), row(qw)],
        out_shape=[jax.ShapeDtypeStruct((N, D_MODEL), F32), jax.ShapeDtypeStruct((N, D_MODEL), BF16),
                   jax.ShapeDtypeStruct((N, qw), BF16)],
        compiler_params=_params(("parallel",)),
        name="mix",
    )(x, f_out, a_out, w_out_bf[:FOURIER_WIDTH], w_out_bf[FOURIER_WIDTH:], g2.reshape(1, D_MODEL), wq_bf)


def _argmax_rows(parts, ids_ascending):
    while len(parts) > 1:
        nxt = []
        for k in range(0, len(parts) - 1, 2):
            (va, ia), (vb, ib) = parts[k], parts[k + 1]
            take_b = (vb > va) if ids_ascending else ((vb > va) | ((vb == va) & (ib < ia)))
            nxt.append((jnp.where(take_b, vb, va), jnp.where(take_b, ib, ia)))
        if len(parts) % 2:
            nxt.append(parts[-1])
        parts = nxt
    v, i = parts[0]
    m = jnp.max(v, axis=0, keepdims=True)
    return m, jnp.min(jnp.where(v == m, i, ID_NONE), axis=0, keepdims=True)


def _extract_topk(s, id_parts, ids_ascending):
    tq = s.shape[1]
    parts = [(s[8 * k:8 * k + 8], ids) for k, ids in enumerate(id_parts)]
    rank = lax.broadcasted_iota(jnp.int32, (TOPK, tq), 0)
    vals = jnp.zeros((TOPK, tq), F32)
    idxs = jnp.zeros((TOPK, tq), F32)
    for kk in range(TOPK):
        m, first = _argmax_rows(parts, ids_ascending)
        parts = [(jnp.where(ids == first, -jnp.inf, v), ids) for v, ids in parts]
        vals = jnp.where(rank == kk, m, vals)
        idxs = jnp.where(rank == kk, first, idxs)
    return vals, idxs


def _extract_topk_sorted(s, id_parts, depth):
    tq = s.shape[1]
    blocks = [(s[8 * k:8 * k + 8], ids) for k, ids in enumerate(id_parts)]
    groups = [blocks[g:g + depth] for g in range(0, len(blocks), depth)]
    for col in groups:
        for done in range(depth - 1):
            for j in range(depth - 1 - done):
                (va, ia), (vb, ib) = col[j], col[j + 1]
                swap = vb > va
                col[j] = (jnp.where(swap, vb, va), jnp.where(swap, ib, ia))
                col[j + 1] = (jnp.where(swap, va, vb), jnp.where(swap, ia, ib))
    rank = lax.broadcasted_iota(jnp.int32, (TOPK, tq), 0)
    vals = jnp.zeros((TOPK, tq), F32)
    idxs = jnp.zeros((TOPK, tq), F32)
    for kk in range(TOPK):
        m, first = _argmax_rows([col[0] for col in groups], True)
        for col in groups:
            won = col[0][1] == first
            for d in range(depth - 1):
                col[d] = (jnp.where(won, col[d + 1][0], col[d][0]), jnp.where(won, col[d + 1][1], col[d][1]))
            col[depth - 1] = (jnp.where(won, -jnp.inf, col[depth - 1][0]), col[depth - 1][1])
        vals = jnp.where(rank == kk, m, vals)
        idxs = jnp.where(rank == kk, first, idxs)
    return vals, idxs


def _select_rows(sel, table):
    out = jnp.zeros_like(sel)
    for a in range(TOPK):
        out = jnp.where(sel == float(a), table[a:a + 1, :], out)
    return out


def _candidate_ids():
    ids = [a * TOPK for a in range(TOPK)]
    for b in range(1, 8):
        ids += [a * TOPK + b if (a + 1) * (b + 1) <= TOPK else ID_NONE for a in range(8)]
    ids += list(range(8, TOPK))
    return np.asarray(ids, np.float32)


def _topk_head(q0, q1, k0, k1, key_ids, cand_ids, cand_id_parts):
    (v1, x1), (v2, x2) = [_extract_topk_sorted(_dot_nt(k, q), key_ids, 4) for k, q in ((k0, q0), (k1, q1))]
    cand = jnp.concatenate([v1 + v2[0:1, :]] + [v1[0:8, :] + v2[b:b + 1, :] for b in range(1, 8)]
                           + [v1[0:1, :] + v2[8:TOPK, :]], axis=0)
    cand = jnp.where(cand_ids < ID_NONE, cand, -jnp.inf)
    top_s, flat = _extract_topk(cand, cand_id_parts, False)
    a_sel = jnp.floor(flat * (1.0 / TOPK))
    b_sel = flat - a_sel * TOPK
    e = jnp.exp(top_s - top_s[0:1, :])
    return _select_rows(a_sel, x1), _select_rows(b_sel, x2), e * (1.0 / jnp.sum(e, axis=0, keepdims=True))


def _pack_bf16_pair(a, b):
    abits = lax.bitcast_convert_type(a.astype(BF16).astype(F32), jnp.uint32)
    bbits = lax.bitcast_convert_type(b.astype(BF16).astype(F32), jnp.uint32)
    return lax.bitcast_convert_type(abits | (bbits >> 16), jnp.int32)


def _unpack_bf16_pair(word):
    bits = lax.bitcast_convert_type(word, jnp.uint32)
    hi = lax.bitcast_convert_type(bits & jnp.uint32(0xFFFF0000), F32)
    lo = lax.bitcast_convert_type(bits << 16, F32)
    return hi, lo

def _peer_body(tq, nh, x1_ref, xn_ref, qp_ref, keys_ref, ids_ref, u_ref, v_ref, gf_ref,
               o_ref, gate_ref, sel_ref, i1_ref, g_ref, i2t_ref):
    i = pl.program_id(0)
    c = pl.program_id(1)
    half_keys = N_KEYS // 2
    slot = lax.rem(i, 2)

    @pl.when(jnp.logical_and(c == 0, i == 0))
    def _():
        gate_ref[...] = jnp.zeros_like(gate_ref)

    @pl.when(jnp.logical_and(c == 0, i > 0))
    def _():
        for grp in range(tq // LANES):
            cols = slice(grp * LANES, (grp + 1) * LANES)
            i1_ref[cols, :] = sel_ref[1 - slot, 0, :, cols].T.astype(jnp.int32)
            i2t_ref[grp] = sel_ref[1 - slot, 1, :, cols].astype(jnp.int32)
            g_ref[cols, :] = sel_ref[1 - slot, 2, :, cols].T
        row = lax.broadcasted_iota(jnp.int32, (N_KEYS, N_KEYS), 0)
        lane = lax.broadcasted_iota(jnp.int32, (N_KEYS, N_KEYS), 1)

        def build(grp, carry):
            base = pl.multiple_of(grp * LANES, LANES)
            i1g, gg = i1_ref[pl.ds(base, LANES), :], g_ref[pl.ds(base, LANES), :]
            i2t = i2t_ref[grp]
            for r in range(LANES):
                p1 = jnp.where(i1g[r:r + 1, :] == row, gg[r:r + 1, :], 0.0).astype(BF16)
                p2 = jnp.where(i2t[:, r:r + 1] == lane, 1.0, 0.0).astype(BF16)
                gt = _dot(p1, p2)
                word = _pack_bf16_pair(gt[:half_keys], gt[half_keys:])
                gate_ref[pl.ds(pl.multiple_of((base + r) * G_PITCH, 8), half_keys), :] = word
            return carry

        lax.fori_loop(0, tq // LANES, build, 0)

    @pl.when(c == 0)
    def _():
        o_ref[...] = jnp.zeros_like(o_ref)

    sub = lax.broadcasted_iota(jnp.int32, (8, tq), 0).astype(F32)
    key_ids = [sub + float(r) for r in range(0, N_KEYS, 8)]
    cand_ids = ids_ref[...]
    cand_id_parts = [cand_ids[r:r + 8] for r in range(0, cand_ids.shape[0], 8)]
    sel = _topk_head(qp_ref[:, :N_KEYS], qp_ref[:, N_KEYS:], keys_ref[2 * c], keys_ref[2 * c + 1],
                     key_ids, cand_ids, cand_id_parts)
    rows = pl.ds(pl.multiple_of(c * TOPK, TOPK), TOPK)
    for k, val in enumerate(sel):
        sel_ref[slot, k, rows, :] = val

    gates = [_unpack_bf16_pair(gate_ref[pl.ds(c * nh + ii, tq, stride=G_PITCH), :])
             for ii in range(nh)]
    a = _dot(xn_ref[...], u_ref[...])
    act = 0.5 * a * (1.0 + lax.erf(a * float(np.sqrt(0.5))))
    w = jnp.concatenate([(act[:, (half * nh + ii) * N_KEYS:(half * nh + ii + 1) * N_KEYS] * gates[ii][half]).astype(BF16)
                         for half in range(2) for ii in range(nh)], axis=1)
    o_ref[...] += _dot(w, v_ref[...])

    @pl.when(c == pl.num_programs(1) - 1)
    def _():
        y = x1_ref[...] + o_ref[...]
        ms = jnp.mean(y * y, axis=-1, keepdims=True)
        o_ref[...] = y * lax.rsqrt(ms + EPS) * gf_ref[...]


def _peer_tables(u, v):
    ec = PEER_NH * N_KEYS
    n_chunks = N_EXPERTS // (2 * ec)
    u4 = u.astype(BF16).reshape(2, n_chunks, ec, D_MODEL)
    v4 = v.astype(BF16).reshape(2, n_chunks, ec, D_MODEL)
    return (u4.transpose(1, 3, 0, 2).reshape(n_chunks, D_MODEL, 2 * ec),
            v4.transpose(1, 0, 2, 3).reshape(n_chunks, 2 * ec, D_MODEL))


def _peer(x1, xn, qp, keys_bf, u_blk, v_blk, gf):
    N = x1.shape[0]
    tq = 512
    nh = PEER_NH
    ec = nh * N_KEYS
    n_chunks = (N_KEYS // 2) // nh
    assert n_chunks == PEER_HEADS
    nt = N // tq
    nsel = PEER_HEADS * TOPK
    ids = _candidate_ids()
    ids = jnp.asarray(np.broadcast_to(ids[:, None], (ids.shape[0], tq)))
    prev = lambda i: jnp.maximum(i - 1, 0)
    tok = lambda w: pl.BlockSpec((tq, w), lambda i, c: (prev(i), 0), pipeline_mode=pl.Buffered(1))
    const = lambda shape: pl.BlockSpec(shape, lambda i, c: (0,) * len(shape))
    return pl.pallas_call(
        functools.partial(_peer_body, tq, nh),
        grid=(nt + 1, n_chunks),
        in_specs=[tok(D_MODEL), tok(D_MODEL),
                  pl.BlockSpec((tq, 2 * N_KEYS), lambda i, c: (jnp.minimum(i, nt - 1), c)),
                  const((2 * PEER_HEADS, N_KEYS, N_KEYS)), const(ids.shape),
                  pl.BlockSpec((None, D_MODEL, 2 * ec), lambda i, c: (c, 0, 0)),
                  pl.BlockSpec((None, 2 * ec, D_MODEL), lambda i, c: (c, 0, 0)),
                  const((1, D_MODEL))],
        out_specs=pl.BlockSpec((tq, D_MODEL), lambda i, c: (prev(i), 0)),
        out_shape=jax.ShapeDtypeStruct((N, D_MODEL), F32),
        scratch_shapes=[pltpu.VMEM((tq * G_PITCH, N_KEYS), jnp.int32),
                        pltpu.VMEM((2, 3, nsel, tq), F32),
                        pltpu.VMEM((tq, nsel), jnp.int32),
                        pltpu.VMEM((tq, nsel), F32),
                        pltpu.VMEM((tq // LANES, nsel, LANES), jnp.int32)],
        compiler_params=_params(("arbitrary", "arbitrary")),
        name="peer",
    )(x1, xn, qp, keys_bf, ids, u_blk, v_blk, gf.reshape(1, D_MODEL))


def _encoder(x, w):
    B, S, _ = x.shape
    f_in, qkv1, qkv4, qkv16 = _proj(x, w["norm1_g"], w["w_in"])
    f_out = _fourier(f_in, w["w_fourier"])
    a_out = _attention([qkv1, qkv4, qkv16], S)
    N = B * S
    flat = lambda a: a.reshape(N, a.shape[-1])
    x1, xn, qp = _mix(flat(x), flat(f_out), flat(a_out), w["w_out"], w["norm2_g"], w["w_query"])
    y = _peer(x1, xn, qp, w["sub_keys"], w["expert_u"], w["expert_v"], w["final_g"])
    return y.reshape(B, S, D_MODEL)


def kernel(x_prompt, x_sample, norm1_g, w_in, w_fourier, w_out, norm2_g, w_query, sub_keys, expert_u, expert_v, final_g):
    u_blk, v_blk = _peer_tables(expert_u[0], expert_v[0])
    w = {
        "norm1_g": norm1_g[0], "w_in": w_in[0].astype(BF16), "w_fourier": w_fourier[0],
        "w_out": w_out[0].astype(BF16), "norm2_g": norm2_g[0], "w_query": w_query[0].astype(BF16),
        "sub_keys": sub_keys[0].reshape(2 * PEER_HEADS, N_KEYS, N_KEYS).astype(BF16),
        "expert_u": u_blk, "expert_v": v_blk, "final_g": final_g,
    }
    return (_encoder(x_prompt, w), _encoder(x_sample, w))
```

```python
import functools

import numpy as np
import jax
import jax.numpy as jnp
from jax import lax
from jax.experimental import pallas as pl
from jax.experimental.pallas import tpu as pltpu

F32 = jnp.float32
BF16 = jnp.bfloat16

D_MODEL = 1024
HEAD_DIM = 64
FOURIER_WIDTH = 512
ATTN_WIDTH = 512
IN_WIDTH = 2048
N_GROUPS = 8
DILATIONS = (1, 4, 16)
HALF_WINDOW = 64
ROPE_THETA = 500000.0
ROPE_DIM = 16
PEER_HEADS = 8
N_KEYS = 128
N_EXPERTS = N_KEYS * N_KEYS
TOPK = 16
EPS = 1e-6
NEG = -1e30

LANES = 128
DFT_INNER = 128
G_PITCH = 72
ID_NONE = 1e9
VMEM_LIMIT = 56 * 2**20
PEER_NH = 8


def _params(sem, vmem=None):
    return pltpu.CompilerParams(dimension_semantics=sem, vmem_limit_bytes=vmem or VMEM_LIMIT)


def _dot(a, b):
    return jnp.dot(a, b, preferred_element_type=F32)


def _dot_nt(a, b):
    return lax.dot_general(a, b, (((1,), (1,)), ((), ())), preferred_element_type=F32)


def _proj_body(tm, x_ref, g_ref, w_ref, c_ref, sa_ref, sb_ref,
               f_ref, q_ref, k_ref, v_ref, q4_ref, k4_ref, v4_ref, q16_ref, k16_ref, v16_ref, stage_ref):
    x = x_ref[...]
    ms = jnp.mean(x * x, axis=-1, keepdims=True)
    h = (x * lax.rsqrt(ms + EPS) * g_ref[...]).astype(BF16)
    p = _dot(h, w_ref[...])
    f_ref[...] = p[:, :FOURIER_WIDTH].astype(BF16)
    c, sa, sb = c_ref[...], sa_ref[...], sb_ref[...]
    for j in range(ATTN_WIDTH // LANES):
        lo = j * LANES
        qc = p[:, FOURIER_WIDTH + lo:FOURIER_WIDTH + lo + LANES]
        kc = p[:, FOURIER_WIDTH + ATTN_WIDTH + lo:FOURIER_WIDTH + ATTN_WIDTH + lo + LANES]
        vc = p[:, FOURIER_WIDTH + 2 * ATTN_WIDTH + lo:FOURIER_WIDTH + 2 * ATTN_WIDTH + lo + LANES]
        qr = (qc * c + pltpu.roll(qc, LANES - 8, 1) * sa + pltpu.roll(qc, 8, 1) * sb) * (HEAD_DIM ** -0.5)
        kr = kc * c + pltpu.roll(kc, LANES - 8, 1) * sa + pltpu.roll(kc, 8, 1) * sb
        for a, (val, nat) in enumerate(((qr, q_ref), (kr, k_ref), (vc, v_ref))):
            nat[:, lo:lo + LANES] = val.astype(BF16)
            stage_ref[a, j] = val
    for a, dil in enumerate(((q4_ref, q16_ref), (k4_ref, k16_ref), (v4_ref, v16_ref))):
        for R, d_ref in zip(DILATIONS[1:], dil):
            for j in range(ATTN_WIDTH // LANES):
                for r in range(R):
                    piece = stage_ref[a, j, pl.ds(r, tm // R, stride=R), :]
                    d_ref[:, (j * R + r) * LANES:(j * R + r + 1) * LANES] = piece.astype(BF16)


def _rope_tables(S):
    half = ROPE_DIM // 2
    inv = ROPE_THETA ** (-(jnp.arange(half, dtype=F32) * 2.0) / ROPE_DIM)
    ang = jnp.arange(S, dtype=F32)[:, None] * inv[None, :]
    cos, sin = jnp.cos(ang), jnp.sin(ang)
    one = jnp.ones((S, HEAD_DIM - ROPE_DIM), F32)
    zero = jnp.zeros((S, HEAD_DIM - ROPE_DIM), F32)
    z8 = jnp.zeros((S, half), F32)
    c = jnp.concatenate([cos, cos, one], axis=1)
    sa = jnp.concatenate([-sin, z8, zero], axis=1)
    sb = jnp.concatenate([z8, sin, zero], axis=1)
    rep = LANES // HEAD_DIM
    return jnp.tile(c, (1, rep)), jnp.tile(sa, (1, rep)), jnp.tile(sb, (1, rep))


def _proj(x, g, w_bf):
    B, S, _ = x.shape
    tm = min(512, S)
    c, sa, sb = _rope_tables(S)
    tab = pl.BlockSpec((tm, LANES), lambda b, i: (i, 0))
    out = pl.BlockSpec((None, tm, ATTN_WIDTH), lambda b, i: (b, i, 0))
    shp = jax.ShapeDtypeStruct((B, S, ATTN_WIDTH), BF16)
    dil_specs, dil_shapes = [], []
    for R in DILATIONS[1:]:
        dil_specs.append(pl.BlockSpec((None, tm // R, R * ATTN_WIDTH), lambda b, i: (b, i, 0)))
        dil_shapes.append(jax.ShapeDtypeStruct((B, S // R, R * ATTN_WIDTH), BF16))
    outs = pl.pallas_call(
        functools.partial(_proj_body, tm),
        grid=(B, S // tm),
        in_specs=[pl.BlockSpec((None, tm, D_MODEL), lambda b, i: (b, i, 0)),
                  pl.BlockSpec((1, D_MODEL), lambda b, i: (0, 0)),
                  pl.BlockSpec((D_MODEL, IN_WIDTH), lambda b, i: (0, 0)),
                  tab, tab, tab],
        out_specs=[out] * 4 + [dil_specs[0]] * 3 + [dil_specs[1]] * 3,
        out_shape=[shp] * 4 + [dil_shapes[0]] * 3 + [dil_shapes[1]] * 3,
        scratch_shapes=[pltpu.VMEM((3, ATTN_WIDTH // LANES, tm, LANES), F32)],
        compiler_params=_params(("parallel", "parallel")),
        name="proj",
    )(x, g.reshape(1, D_MODEL), w_bf, c, sa, sb)
    return outs[0], outs[1:4], outs[4:7], outs[7:10]


def _cos_sin(n_rows, n_cols, period):
    i = jnp.arange(n_rows, dtype=jnp.int32)[:, None]
    j = jnp.arange(n_cols, dtype=jnp.int32)[None, :]
    ang = ((i * j) % period).astype(F32) * (2.0 * np.pi / period)
    return jnp.cos(ang), jnp.sin(ang)


def _dft_a_body(n1, d_ref, x_ref, yr_ref, yi_ref):
    y = _dot(d_ref[...], x_ref[...])
    yr_ref[...] = y[:n1].astype(BF16)
    yi_ref[...] = y[n1:].astype(BF16)


def _dft_b_body(ct, scale, yr_ref, yi_ref, tc_ref, ts_ref, d2_ref, c64_ref, s64_ref, wf_ref, o_ref):
    rep = FOURIER_WIDTH // LANES
    for cc in range(ct):
        yr = yr_ref[cc].astype(F32)
        yi = yi_ref[cc].astype(F32)
        tc = jnp.concatenate([tc_ref[cc]] * rep, axis=1)
        ts = jnp.concatenate([ts_ref[cc]] * rep, axis=1)
        zr = yr * tc + yi * ts
        zi = yi * tc - yr * ts
        z = jnp.concatenate([zr, zi], axis=0).astype(BF16)
        xx = _dot(d2_ref[...], z)
        xr = xx[:DFT_INNER].astype(BF16)
        xi = xx[DFT_INNER:].astype(BF16)
        re = (_dot(xr, c64_ref[...]) + _dot(xi, s64_ref[...])) * scale
        o_ref[cc] = _dot(re.astype(BF16), wf_ref[...]).astype(BF16)


def _block_diag(m):
    G, n, _ = m.shape
    eye = jnp.eye(G, dtype=m.dtype)
    return (eye[:, None, :, None] * m[:, :, None, :]).reshape(G * n, G * n)


def _fourier(f_in, w_fourier):
    B, S, W = f_in.shape
    n2 = DFT_INNER
    n1 = S // n2
    cols = n2 * W
    tcw = 4096
    c1, s1 = _cos_sin(n1, n1, n1)
    d1 = jnp.concatenate([c1, -s1], axis=0).astype(BF16)
    xa = f_in.reshape(B, n1, cols)
    blk = pl.BlockSpec((None, n1, tcw), lambda b, j: (b, 0, j))
    yshape = jax.ShapeDtypeStruct((B, n1, cols), BF16)
    yr, yi = pl.pallas_call(
        functools.partial(_dft_a_body, n1),
        grid=(B, cols // tcw),
        in_specs=[pl.BlockSpec((2 * n1, n1), lambda b, j: (0, 0)), blk],
        out_specs=[blk, blk],
        out_shape=[yshape, yshape],
        compiler_params=_params(("parallel", "parallel")),
        name="dft_a",
    )(d1, xa)

    yr = yr.reshape(B, n1, n2, W)
    yi = yi.reshape(B, n1, n2, W)
    twc, tws = _cos_sin(n1, n2, S)
    twc = jnp.broadcast_to(twc[:, :, None], (n1, n2, LANES))
    tws = jnp.broadcast_to(tws[:, :, None], (n1, n2, LANES))
    c2, s2 = _cos_sin(n2, n2, n2)
    d2 = jnp.concatenate([jnp.concatenate([c2, s2], axis=1),
                          jnp.concatenate([-s2, c2], axis=1)], axis=0).astype(BF16)
    cg, sg = _cos_sin(HEAD_DIM, HEAD_DIM, HEAD_DIM)
    c64 = _block_diag(jnp.broadcast_to(cg, (N_GROUPS, HEAD_DIM, HEAD_DIM))).astype(BF16)
    s64 = _block_diag(jnp.broadcast_to(sg, (N_GROUPS, HEAD_DIM, HEAD_DIM))).astype(BF16)
    wf = _block_diag(w_fourier).astype(BF16)
    ct = 8
    scale = float(1.0 / np.sqrt(S * HEAD_DIM))
    yblk = pl.BlockSpec((None, ct, n2, W), lambda b, i: (b, i, 0, 0))
    tblk = pl.BlockSpec((ct, n2, LANES), lambda b, i: (i, 0, 0))
    full = lambda r, c: pl.BlockSpec((r, c), lambda b, i: (0, 0))
    out = pl.pallas_call(
        functools.partial(_dft_b_body, ct, scale),
        grid=(B, n1 // ct),
        in_specs=[yblk, yblk, tblk, tblk, full(2 * n2, 2 * n2), full(W, W), full(W, W), full(W, W)],
        out_specs=yblk,
        out_shape=jax.ShapeDtypeStruct((B, n1, n2, W), BF16),
        compiler_params=_params(("parallel", "parallel")),
        name="dft_b",
    )(yr, yi, twc, tws, d2, c64, s64, wf)
    return out.transpose(0, 2, 1, 3).reshape(B, S, W)


def _band_attend(qf, ks, vs, bias, lo_half):
    outs, lses = [], []
    for hh in range(2):
        mine = lo_half if hh == 0 else jnp.logical_not(lo_half)
        qm = jnp.where(mine, qf, 0.0).astype(BF16)
        s = _dot_nt(qm, ks) + bias
        m = jnp.max(s, axis=1, keepdims=True)
        p = jnp.exp(s - m)
        l = jnp.sum(p, axis=1, keepdims=True)
        outs.append(_dot(p.astype(BF16), vs) * (1.0 / l))
        lses.append(m + jnp.log(l))
    return jnp.where(lo_half, outs[0], outs[1]), jnp.where(lo_half, lses[0], lses[1])


def _attn_bias_terms():
    sub, win = 128, 128 + 2 * HALF_WINDOW
    lane = lax.broadcasted_iota(jnp.int32, (sub, LANES), 1)
    r = lax.broadcasted_iota(jnp.int32, (sub, win), 0)
    c = lax.broadcasted_iota(jnp.int32, (sub, win), 1)
    band = jnp.where((c >= r) & (c <= r + 2 * HALF_WINDOW), 0.0, NEG)
    return lane < HEAD_DIM, band, lax.broadcasted_iota(jnp.int32, (1, win), 1)


def _attn_body(tb, S, *refs):
    ins, (a_ref, og_ref, lg_ref) = refs[:-3], refs[-3:]
    sub, win = 128, 128 + 2 * HALF_WINDOW
    lo_half, band, key = _attn_bias_terms()
    i = pl.program_id(1)
    for g, R in enumerate(DILATIONS):
        q_ref, kp_ref, kc_ref, kn_ref, vp_ref, vc_ref, vn_ref = ins[7 * g:7 * g + 7]
        tq, L = tb // R, S // R
        n_sub = tq // sub
        for r in range(R):
            sl = slice(r * LANES, (r + 1) * LANES)
            k2 = jnp.concatenate([kp_ref[:, sl], kc_ref[:, sl], kn_ref[:, sl]], axis=0)
            v2 = jnp.concatenate([vp_ref[:, sl], vc_ref[:, sl], vn_ref[:, sl]], axis=0)
            for sb in range(n_sub):
                bias = band
                if sb == 0 or sb == n_sub - 1:
                    kpos = i * tq + sb * sub - HALF_WINDOW + key
                    bias = band + jnp.where((kpos >= 0) & (kpos < L), 0.0, NEG)
                o, lse = _band_attend(q_ref[sb * sub:(sb + 1) * sub, sl].astype(F32),
                                      k2[sb * sub:sb * sub + win], v2[sb * sub:sb * sub + win], bias, lo_half)
                dst = pl.ds(R * sb * sub + r, sub, stride=R)
                og_ref[g, dst, :] = o
                lg_ref[g, dst, :] = lse
    chunk = 256
    for t in range(0, tb, chunk):
        rows = slice(t, t + chunk)
        l1, l2, l3 = lg_ref[0, rows, :], lg_ref[1, rows, :], lg_ref[2, rows, :]
        mx = jnp.maximum(jnp.maximum(l1, l2), l3)
        e1, e2, e3 = jnp.exp(l1 - mx), jnp.exp(l2 - mx), jnp.exp(l3 - mx)
        a = (e1 * og_ref[0, rows, :] + e2 * og_ref[1, rows, :] + e3 * og_ref[2, rows, :]) * (1.0 / (e1 + e2 + e3))
        a_ref[rows, :] = a.astype(BF16)


def _halo_index(hb, nhalo, offset):
    return lambda b, i, p: (b, jnp.clip(i * hb + offset, 0, nhalo - 1), p)


def _attention(qkv_groups, S):
    B = qkv_groups[0][0].shape[0]
    tb = min(2048, S)
    specs, args = [], []
    for (q, k, v), R in zip(qkv_groups, DILATIONS):
        tq, L, wb = tb // R, S // R, R * LANES
        hb, nhalo = tq // HALF_WINDOW, L // HALF_WINDOW
        ctr = pl.BlockSpec((None, tq, wb), lambda b, i, p: (b, i, p))
        prev = pl.BlockSpec((None, HALF_WINDOW, wb), _halo_index(hb, nhalo, -1))
        nxt = pl.BlockSpec((None, HALF_WINDOW, wb), _halo_index(hb, nhalo, hb))
        specs += [ctr, prev, ctr, nxt, prev, ctr, nxt]
        args += [q, k, k, k, v, v, v]
    return pl.pallas_call(
        functools.partial(_attn_body, tb, S),
        grid=(B, S // tb, ATTN_WIDTH // LANES),
        in_specs=specs,
        out_specs=pl.BlockSpec((None, tb, LANES), lambda b, i, p: (b, i, p)),
        out_shape=jax.ShapeDtypeStruct((B, S, ATTN_WIDTH), BF16),
        scratch_shapes=[pltpu.VMEM((len(DILATIONS), tb, LANES), F32), pltpu.VMEM((len(DILATIONS), tb, LANES), F32)],
        compiler_params=_params(("parallel", "parallel", "parallel")),
        name="attn",
    )(*args)


def _mix_body(x_ref, f_ref, a_ref, wf_ref, wa_ref, g_ref, wq_ref, x1_ref, xn_ref, qp_ref):
    x1 = x_ref[...] + _dot(f_ref[...], wf_ref[...]) + _dot(a_ref[...], wa_ref[...])
    x1_ref[...] = x1
    ms = jnp.mean(x1 * x1, axis=-1, keepdims=True)
    xn = (x1 * lax.rsqrt(ms + EPS) * g_ref[...]).astype(BF16)
    xn_ref[...] = xn
    qp_ref[...] = _dot(xn, wq_ref[...]).astype(BF16)


def _mix(x, f_out, a_out, w_out_bf, g2, wq_bf):
    N = x.shape[0]
    tm = 512
    row = lambda w: pl.BlockSpec((tm, w), lambda i: (i, 0))
    full = lambda r, c: pl.BlockSpec((r, c), lambda i: (0, 0))
    qw = wq_bf.shape[1]
    return pl.pallas_call(
        _mix_body,
        grid=(N // tm,),
        in_specs=[row(D_MODEL), row(FOURIER_WIDTH), row(ATTN_WIDTH),
                  full(FOURIER_WIDTH, D_MODEL), full(ATTN_WIDTH, D_MODEL), full(1, D_MODEL), full(D_MODEL, qw)],
        out_specs=[row(D_MODEL), row(D_MODEL), row(qw)],
        out_shape=[jax.ShapeDtypeStruct((N, D_MODEL), F32), jax.ShapeDtypeStruct((N, D_MODEL), BF16),
                   jax.ShapeDtypeStruct((N, qw), BF16)],
        compiler_params=_params(("parallel",)),
        name="mix",
    )(x, f_out, a_out, w_out_bf[:FOURIER_WIDTH], w_out_bf[FOURIER_WIDTH:], g2.reshape(1, D_MODEL), wq_bf)


def _argmax_rows(parts):
    while len(parts) > 1:
        nxt = []
        for k in range(0, len(parts) - 1, 2):
            (va, ia), (vb, ib) = parts[k], parts[k + 1]
            take_b = vb > va
            nxt.append((jnp.where(take_b, vb, va), jnp.where(take_b, ib, ia)))
        if len(parts) % 2:
            nxt.append(parts[-1])
        parts = nxt
    v, i = parts[0]
    m = jnp.max(v, axis=0, keepdims=True)
    return m, jnp.min(jnp.where(v == m, i, ID_NONE), axis=0, keepdims=True)


def _extract_topk_sorted(s, id_parts, depth):
    tq = s.shape[1]
    blocks = [(s[8 * k:8 * k + 8], ids) for k, ids in enumerate(id_parts)]
    groups = [blocks[g:g + depth] for g in range(0, len(blocks), depth)]
    for col in groups:
        for done in range(depth - 1):
            for j in range(depth - 1 - done):
                (va, ia), (vb, ib) = col[j], col[j + 1]
                swap = vb > va
                col[j] = (jnp.where(swap, vb, va), jnp.where(swap, ib, ia))
                col[j + 1] = (jnp.where(swap, va, vb), jnp.where(swap, ia, ib))
    rank = lax.broadcasted_iota(jnp.int32, (TOPK, tq), 0)
    vals = jnp.zeros((TOPK, tq), F32)
    idxs = jnp.zeros((TOPK, tq), F32)
    for kk in range(TOPK):
        m, first = _argmax_rows([col[0] for col in groups])
        for col in groups:
            won = col[0][1] == first
            for d in range(depth - 1):
                col[d] = (jnp.where(won, col[d + 1][0], col[d][0]), jnp.where(won, col[d + 1][1], col[d][1]))
            col[depth - 1] = (jnp.where(won, -jnp.inf, col[depth - 1][0]), col[depth - 1][1])
        vals = jnp.where(rank == kk, m, vals)
        idxs = jnp.where(rank == kk, first, idxs)
    return vals, idxs


def _select_rows(sel, table):
    out = jnp.zeros_like(sel)
    for a in range(TOPK):
        out = jnp.where(sel == float(a), table[a:a + 1, :], out)
    return out


def _candidate_ids():
    ids = list(range(TOPK))
    for a in range(1, 8):
        ids += [a * TOPK + b if (a + 1) * (b + 1) <= TOPK else ID_NONE for b in range(8)]
    ids += [a * TOPK for a in range(8, TOPK)]
    return np.asarray(ids, np.float32)


def _topk_head(q0, q1, k0, k1, key_ids, cand_ids, cand_id_parts):
    (v1, x1), (v2, x2) = [_extract_topk_sorted(_dot_nt(k, q), key_ids, 4) for k, q in ((k0, q0), (k1, q1))]
    cand = jnp.concatenate([v1[0:1, :] + v2] + [v1[a:a + 1, :] + v2[0:8, :] for a in range(1, 8)]
                           + [v1[8:TOPK, :] + v2[0:1, :]], axis=0)
    cand = jnp.where(cand_ids < ID_NONE, cand, -jnp.inf)
    top_s, flat = _extract_topk_sorted(cand, cand_id_parts, 5)
    a_sel = jnp.floor(flat * (1.0 / TOPK))
    b_sel = flat - a_sel * TOPK
    e = jnp.exp(top_s - top_s[0:1, :])
    return _select_rows(a_sel, x1), _select_rows(b_sel, x2), e * (1.0 / jnp.sum(e, axis=0, keepdims=True))


def _pack_bf16_pair(a, b):
    abits = lax.bitcast_convert_type(a.astype(BF16).astype(F32), jnp.uint32)
    bbits = lax.bitcast_convert_type(b.astype(BF16).astype(F32), jnp.uint32)
    return lax.bitcast_convert_type(abits | (bbits >> 16), jnp.int32)


def _unpack_bf16_pair(word):
    bits = lax.bitcast_convert_type(word, jnp.uint32)
    hi = lax.bitcast_convert_type(bits & jnp.uint32(0xFFFF0000), F32)
    lo = lax.bitcast_convert_type(bits << 16, F32)
    return hi, lo

def _peer_body(tq, nh, x1_ref, xn_ref, qp_ref, keys_ref, ids_ref, u_ref, v_ref, gf_ref,
               o_ref, gate_ref, sel_ref, i1_ref, g_ref, i2t_ref):
    i = pl.program_id(0)
    c = pl.program_id(1)
    half_keys = N_KEYS // 2
    slot = lax.rem(i, 2)

    @pl.when(jnp.logical_and(c == 0, i == 0))
    def _():
        gate_ref[...] = jnp.zeros_like(gate_ref)

    @pl.when(jnp.logical_and(c == 0, i > 0))
    def _():
        for grp in range(tq // LANES):
            cols = slice(grp * LANES, (grp + 1) * LANES)
            i1_ref[cols, :] = sel_ref[1 - slot, 0, :, cols].T.astype(jnp.int32)
            i2t_ref[grp] = sel_ref[1 - slot, 1, :, cols].astype(jnp.int32)
            g_ref[cols, :] = sel_ref[1 - slot, 2, :, cols].T
        row = lax.broadcasted_iota(jnp.int32, (N_KEYS, N_KEYS), 0)
        lane = lax.broadcasted_iota(jnp.int32, (N_KEYS, N_KEYS), 1)

        def build(grp, carry):
            base = pl.multiple_of(grp * LANES, LANES)
            i1g, gg = i1_ref[pl.ds(base, LANES), :], g_ref[pl.ds(base, LANES), :]
            i2t = i2t_ref[grp]
            for r in range(LANES):
                p1 = jnp.where(i1g[r:r + 1, :] == row, gg[r:r + 1, :], 0.0).astype(BF16)
                p2 = jnp.where(i2t[:, r:r + 1] == lane, 1.0, 0.0).astype(BF16)
                gt = _dot(p1, p2)
                word = _pack_bf16_pair(gt[:half_keys], gt[half_keys:])
                gate_ref[pl.ds(pl.multiple_of((base + r) * G_PITCH, 8), half_keys), :] = word
            return carry

        lax.fori_loop(0, tq // LANES, build, 0)

    @pl.when(c == 0)
    def _():
        o_ref[...] = jnp.zeros_like(o_ref)

    sub = lax.broadcasted_iota(jnp.int32, (8, tq), 0).astype(F32)
    key_ids = [sub + float(r) for r in range(0, N_KEYS, 8)]
    cand_ids = ids_ref[...]
    cand_id_parts = [cand_ids[r:r + 8] for r in range(0, cand_ids.shape[0], 8)]
    sel = _topk_head(qp_ref[:, :N_KEYS], qp_ref[:, N_KEYS:], keys_ref[2 * c], keys_ref[2 * c + 1],
                     key_ids, cand_ids, cand_id_parts)
    rows = pl.ds(pl.multiple_of(c * TOPK, TOPK), TOPK)
    for k, val in enumerate(sel):
        sel_ref[slot, k, rows, :] = val

    gates = [_unpack_bf16_pair(gate_ref[pl.ds(c * nh + ii, tq, stride=G_PITCH), :])
             for ii in range(nh)]
    a = _dot(xn_ref[...], u_ref[...])
    act = 0.5 * a * (1.0 + lax.erf(a * float(np.sqrt(0.5))))
    w = jnp.concatenate([(act[:, (half * nh + ii) * N_KEYS:(half * nh + ii + 1) * N_KEYS] * gates[ii][half]).astype(BF16)
                         for half in range(2) for ii in range(nh)], axis=1)
    o_ref[...] += _dot(w, v_ref[...])

    @pl.when(c == pl.num_programs(1) - 1)
    def _():
        y = x1_ref[...] + o_ref[...]
        ms = jnp.mean(y * y, axis=-1, keepdims=True)
        o_ref[...] = y * lax.rsqrt(ms + EPS) * gf_ref[...]


def _peer_tables(u, v):
    ec = PEER_NH * N_KEYS
    n_chunks = N_EXPERTS // (2 * ec)
    u4 = u.astype(BF16).reshape(2, n_chunks, ec, D_MODEL)
    v4 = v.astype(BF16).reshape(2, n_chunks, ec, D_MODEL)
    return (u4.transpose(1, 3, 0, 2).reshape(n_chunks, D_MODEL, 2 * ec),
            v4.transpose(1, 0, 2, 3).reshape(n_chunks, 2 * ec, D_MODEL))


def _peer(x1, xn, qp, keys_bf, u_blk, v_blk, gf):
    N = x1.shape[0]
    tq = 512
    nh = PEER_NH
    ec = nh * N_KEYS
    n_chunks = (N_KEYS // 2) // nh
    assert n_chunks == PEER_HEADS
    nt = N // tq
    nsel = PEER_HEADS * TOPK
    ids = _candidate_ids()
    ids = jnp.asarray(np.broadcast_to(ids[:, None], (ids.shape[0], tq)))
    prev = lambda i: jnp.maximum(i - 1, 0)
    tok = lambda w: pl.BlockSpec((tq, w), lambda i, c: (prev(i), 0), pipeline_mode=pl.Buffered(1))
    const = lambda shape: pl.BlockSpec(shape, lambda i, c: (0,) * len(shape))
    return pl.pallas_call(
        functools.partial(_peer_body, tq, nh),
        grid=(nt + 1, n_chunks),
        in_specs=[tok(D_MODEL), tok(D_MODEL),
                  pl.BlockSpec((tq, 2 * N_KEYS), lambda i, c: (jnp.minimum(i, nt - 1), c)),
                  const((2 * PEER_HEADS, N_KEYS, N_KEYS)), const(ids.shape),
                  pl.BlockSpec((None, D_MODEL, 2 * ec), lambda i, c: (c, 0, 0)),
                  pl.BlockSpec((None, 2 * ec, D_MODEL), lambda i, c: (c, 0, 0)),
                  const((1, D_MODEL))],
        out_specs=pl.BlockSpec((tq, D_MODEL), lambda i, c: (prev(i), 0)),
        out_shape=jax.ShapeDtypeStruct((N, D_MODEL), F32),
        scratch_shapes=[pltpu.VMEM((tq * G_PITCH, N_KEYS), jnp.int32),
                        pltpu.VMEM((2, 3, nsel, tq), F32),
                        pltpu.VMEM((tq, nsel), jnp.int32),
                        pltpu.VMEM((tq, nsel), F32),
                        pltpu.VMEM((tq // LANES, nsel, LANES), jnp.int32)],
        compiler_params=_params(("arbitrary", "arbitrary")),
        name="peer",
    )(x1, xn, qp, keys_bf, ids, u_blk, v_blk, gf.reshape(1, D_MODEL))


def _encoder(x, w):
    B, S, _ = x.shape
    f_in, qkv1, qkv4, qkv16 = _proj(x, w["norm1_g"], w["w_in"])
    f_out = _fourier(f_in, w["w_fourier"])
    a_out = _attention([qkv1, qkv4, qkv16], S)
    N = B * S
    flat = lambda a: a.reshape(N, a.shape[-1])
    x1, xn, qp = _mix(flat(x), flat(f_out), flat(a_out), w["w_out"], w["norm2_g"], w["w_query"])
    y = _peer(x1, xn, qp, w["sub_keys"], w["expert_u"], w["expert_v"], w["final_g"])
    return y.reshape(B, S, D_MODEL)


def kernel(x_prompt, x_sample, norm1_g, w_in, w_fourier, w_out, norm2_g, w_query, sub_keys, expert_u, expert_v, final_g):
    u_blk, v_blk = _peer_tables(expert_u[0], expert_v[0])
    w = {
        "norm1_g": norm1_g[0], "w_in": w_in[0].astype(BF16), "w_fourier": w_fourier[0],
        "w_out": w_out[0].astype(BF16), "norm2_g": norm2_g[0], "w_query": w_query[0].astype(BF16),
        "sub_keys": sub_keys[0].reshape(2 * PEER_HEADS, N_KEYS, N_KEYS).astype(BF16),
        "expert_u": u_blk, "expert_v": v_blk, "final_g": final_g,
    }
    return (_encoder(x_prompt, w), _encoder(x_sample, w))
```

```python
import functools

import numpy as np
import jax
import jax.numpy as jnp
from jax import lax
from jax.experimental import pallas as pl
from jax.experimental.pallas import tpu as pltpu

F32 = jnp.float32
BF16 = jnp.bfloat16

D_MODEL = 1024
HEAD_DIM = 64
FOURIER_WIDTH = 512
ATTN_WIDTH = 512
IN_WIDTH = 2048
N_GROUPS = 8
DILATIONS = (1, 4, 16)
HALF_WINDOW = 64
ROPE_THETA = 500000.0
ROPE_DIM = 16
PEER_HEADS = 8
N_KEYS = 128
N_EXPERTS = N_KEYS * N_KEYS
TOPK = 16
EPS = 1e-6
NEG = -1e30

LANES = 128
DFT_INNER = 128
G_PITCH = 72
ID_NONE = 1e9
VMEM_LIMIT = 56 * 2**20

PROJ_TILE = 512
DFT_COL_TILE = 4096
DFT_ROW_TILE = 8
ATTN_QUERY_BLOCK = 128
ATTN_TILE = 2048
ATTN_MERGE_CHUNK = 256
MIX_TILE = 512
PEER_TILE = 512
PEER_NH = 8


def _params(sem, vmem=None):
    return pltpu.CompilerParams(dimension_semantics=sem, vmem_limit_bytes=vmem or VMEM_LIMIT)


def _dot(a, b):
    return jnp.dot(a, b, preferred_element_type=F32)


def _dot_nt(a, b):
    return lax.dot_general(a, b, (((1,), (1,)), ((), ())), preferred_element_type=F32)


def _proj_body(tm, x_ref, g_ref, w_ref, c_ref, sa_ref, sb_ref,
               f_ref, q_ref, k_ref, v_ref, q4_ref, k4_ref, v4_ref, q16_ref, k16_ref, v16_ref, stage_ref, hop_ref):
    x = x_ref[...]
    ms = jnp.mean(x * x, axis=-1, keepdims=True)
    h = (x * lax.rsqrt(ms + EPS) * g_ref[...]).astype(BF16)
    p = _dot(h, w_ref[...])
    f_ref[...] = p[:, :FOURIER_WIDTH].astype(BF16)
    c, sa, sb = c_ref[...], sa_ref[...], sb_ref[...]
    for j in range(ATTN_WIDTH // LANES):
        lo = j * LANES
        qc = p[:, FOURIER_WIDTH + lo:FOURIER_WIDTH + lo + LANES]
        kc = p[:, FOURIER_WIDTH + ATTN_WIDTH + lo:FOURIER_WIDTH + ATTN_WIDTH + lo + LANES]
        vc = p[:, FOURIER_WIDTH + 2 * ATTN_WIDTH + lo:FOURIER_WIDTH + 2 * ATTN_WIDTH + lo + LANES]
        qr = (qc * c + pltpu.roll(qc, LANES - 8, 1) * sa + pltpu.roll(qc, 8, 1) * sb) * (HEAD_DIM ** -0.5)
        kr = kc * c + pltpu.roll(kc, LANES - 8, 1) * sa + pltpu.roll(kc, 8, 1) * sb
        for a, (val, nat) in enumerate(((qr, q_ref), (kr, k_ref), (vc, v_ref))):
            nat[:, lo:lo + LANES] = val.astype(BF16)
            stage_ref[a, j] = val
    _, R1, R2 = DILATIONS
    step = R2 // R1
    for a, (d1_ref, d2_ref) in enumerate(((q4_ref, q16_ref), (k4_ref, k16_ref), (v4_ref, v16_ref))):
        for j in range(ATTN_WIDTH // LANES):
            for r in range(R1):
                piece = stage_ref[a, j, pl.ds(r, tm // R1, stride=R1), :]
                d1_ref[:, (j * R1 + r) * LANES:(j * R1 + r + 1) * LANES] = piece.astype(BF16)
                hop_ref[r] = piece
            for r2 in range(R2):
                piece = hop_ref[r2 % R1, pl.ds(r2 // R1, tm // R2, stride=step), :]
                d2_ref[:, (j * R2 + r2) * LANES:(j * R2 + r2 + 1) * LANES] = piece.astype(BF16)


def _rope_tables(S):
    half = ROPE_DIM // 2
    inv = ROPE_THETA ** (-(jnp.arange(half, dtype=F32) * 2.0) / ROPE_DIM)
    ang = jnp.arange(S, dtype=F32)[:, None] * inv[None, :]
    cos, sin = jnp.cos(ang), jnp.sin(ang)
    one = jnp.ones((S, HEAD_DIM - ROPE_DIM), F32)
    zero = jnp.zeros((S, HEAD_DIM - ROPE_DIM), F32)
    z8 = jnp.zeros((S, half), F32)
    c = jnp.concatenate([cos, cos, one], axis=1)
    sa = jnp.concatenate([-sin, z8, zero], axis=1)
    sb = jnp.concatenate([z8, sin, zero], axis=1)
    rep = LANES // HEAD_DIM
    return jnp.tile(c, (1, rep)), jnp.tile(sa, (1, rep)), jnp.tile(sb, (1, rep))


def _proj(x, g, w_bf):
    B, S, _ = x.shape
    tm = min(PROJ_TILE, S)
    c, sa, sb = _rope_tables(S)
    tab = pl.BlockSpec((tm, LANES), lambda b, i: (i, 0))
    out = pl.BlockSpec((None, tm, ATTN_WIDTH), lambda b, i: (b, i, 0))
    shp = jax.ShapeDtypeStruct((B, S, ATTN_WIDTH), BF16)
    dil_specs, dil_shapes = [], []
    for R in DILATIONS[1:]:
        dil_specs.append(pl.BlockSpec((None, tm // R, R * ATTN_WIDTH), lambda b, i: (b, i, 0)))
        dil_shapes.append(jax.ShapeDtypeStruct((B, S // R, R * ATTN_WIDTH), BF16))
    outs = pl.pallas_call(
        functools.partial(_proj_body, tm),
        grid=(B, S // tm),
        in_specs=[pl.BlockSpec((None, tm, D_MODEL), lambda b, i: (b, i, 0)),
                  pl.BlockSpec((1, D_MODEL), lambda b, i: (0, 0)),
                  pl.BlockSpec((D_MODEL, IN_WIDTH), lambda b, i: (0, 0)),
                  tab, tab, tab],
        out_specs=[out] * 4 + [dil_specs[0]] * 3 + [dil_specs[1]] * 3,
        out_shape=[shp] * 4 + [dil_shapes[0]] * 3 + [dil_shapes[1]] * 3,
        scratch_shapes=[pltpu.VMEM((3, ATTN_WIDTH // LANES, tm, LANES), F32),
                        pltpu.VMEM((DILATIONS[1], tm // DILATIONS[1], LANES), F32)],
        compiler_params=_params(("parallel", "parallel")),
        name="proj",
    )(x, g.reshape(1, D_MODEL), w_bf, c, sa, sb)
    return outs[0], outs[1:4], outs[4:7], outs[7:10]


def _cos_sin(n_rows, n_cols, period):
    i = jnp.arange(n_rows, dtype=jnp.int32)[:, None]
    j = jnp.arange(n_cols, dtype=jnp.int32)[None, :]
    ang = ((i * j) % period).astype(F32) * (2.0 * np.pi / period)
    return jnp.cos(ang), jnp.sin(ang)


def _dft_a_body(n1, d_ref, x_ref, yr_ref, yi_ref):
    y = _dot(d_ref[...], x_ref[...])
    yr_ref[...] = y[:n1].astype(BF16)
    yi_ref[...] = y[n1:].astype(BF16)


def _dft_b_body(ct, scale, yr_ref, yi_ref, tc_ref, ts_ref, d2_ref, c64_ref, s64_ref, wf_ref, o_ref):
    rep = FOURIER_WIDTH // LANES
    xr, xi = [], []
    for cc in range(ct):
        yr = yr_ref[cc].astype(F32)
        yi = yi_ref[cc].astype(F32)
        tc = jnp.concatenate([tc_ref[cc]] * rep, axis=1)
        ts = jnp.concatenate([ts_ref[cc]] * rep, axis=1)
        zr = yr * tc + yi * ts
        zi = yi * tc - yr * ts
        z = jnp.concatenate([zr, zi], axis=0).astype(BF16)
        xx = _dot(d2_ref[...], z)
        xr.append(xx[:DFT_INNER].astype(BF16))
        xi.append(xx[DFT_INNER:].astype(BF16))
    xr, xi = jnp.concatenate(xr, axis=0), jnp.concatenate(xi, axis=0)
    re = (_dot(xr, c64_ref[...]) + _dot(xi, s64_ref[...])) * scale
    y = _dot(re.astype(BF16), wf_ref[...]).astype(BF16)
    for cc in range(ct):
        o_ref[cc] = y[cc * DFT_INNER:(cc + 1) * DFT_INNER]


def _block_diag(m):
    G, n, _ = m.shape
    eye = jnp.eye(G, dtype=m.dtype)
    return (eye[:, None, :, None] * m[:, :, None, :]).reshape(G * n, G * n)


def _fourier(f_in, w_fourier):
    B, S, W = f_in.shape
    n2 = DFT_INNER
    n1 = S // n2
    cols = n2 * W
    tcw = DFT_COL_TILE
    c1, s1 = _cos_sin(n1, n1, n1)
    d1 = jnp.concatenate([c1, -s1], axis=0).astype(BF16)
    xa = f_in.reshape(B, n1, cols)
    blk = pl.BlockSpec((None, n1, tcw), lambda b, j: (b, 0, j))
    yshape = jax.ShapeDtypeStruct((B, n1, cols), BF16)
    yr, yi = pl.pallas_call(
        functools.partial(_dft_a_body, n1),
        grid=(B, cols // tcw),
        in_specs=[pl.BlockSpec((2 * n1, n1), lambda b, j: (0, 0)), blk],
        out_specs=[blk, blk],
        out_shape=[yshape, yshape],
        compiler_params=_params(("parallel", "parallel")),
        name="dft_a",
    )(d1, xa)

    yr = yr.reshape(B, n1, n2, W)
    yi = yi.reshape(B, n1, n2, W)
    twc, tws = _cos_sin(n1, n2, S)
    twc = jnp.broadcast_to(twc[:, :, None], (n1, n2, LANES))
    tws = jnp.broadcast_to(tws[:, :, None], (n1, n2, LANES))
    c2, s2 = _cos_sin(n2, n2, n2)
    d2 = jnp.concatenate([jnp.concatenate([c2, s2], axis=1),
                          jnp.concatenate([-s2, c2], axis=1)], axis=0).astype(BF16)
    cg, sg = _cos_sin(HEAD_DIM, HEAD_DIM, HEAD_DIM)
    c64 = _block_diag(jnp.broadcast_to(cg, (N_GROUPS, HEAD_DIM, HEAD_DIM))).astype(BF16)
    s64 = _block_diag(jnp.broadcast_to(sg, (N_GROUPS, HEAD_DIM, HEAD_DIM))).astype(BF16)
    wf = _block_diag(w_fourier).astype(BF16)
    ct = DFT_ROW_TILE
    scale = float(1.0 / np.sqrt(S * HEAD_DIM))
    yblk = pl.BlockSpec((None, ct, n2, W), lambda b, i: (b, i, 0, 0))
    tblk = pl.BlockSpec((ct, n2, LANES), lambda b, i: (i, 0, 0))
    full = lambda r, c: pl.BlockSpec((r, c), lambda b, i: (0, 0))
    out = pl.pallas_call(
        functools.partial(_dft_b_body, ct, scale),
        grid=(B, n1 // ct),
        in_specs=[yblk, yblk, tblk, tblk, full(2 * n2, 2 * n2), full(W, W), full(W, W), full(W, W)],
        out_specs=yblk,
        out_shape=jax.ShapeDtypeStruct((B, n1, n2, W), BF16),
        compiler_params=_params(("parallel", "parallel")),
        name="dft_b",
    )(yr, yi, twc, tws, d2, c64, s64, wf)
    return out.transpose(0, 2, 1, 3).reshape(B, S, W)


def _band_attend(qf, ks, vs, bias, lo_half):
    outs, lses = [], []
    for hh in range(2):
        mine = lo_half if hh == 0 else jnp.logical_not(lo_half)
        qm = jnp.where(mine, qf, 0.0).astype(BF16)
        s = _dot_nt(qm, ks) + bias
        m = jnp.max(s, axis=1, keepdims=True)
        p = jnp.exp(s - m)
        l = jnp.sum(p, axis=1, keepdims=True)
        outs.append(_dot(p.astype(BF16), vs) * (1.0 / l))
        lses.append(m + jnp.log(l))
    return jnp.where(lo_half, outs[0], outs[1]), jnp.where(lo_half, lses[0], lses[1])


def _attn_bias_terms():
    sub, win = ATTN_QUERY_BLOCK, ATTN_QUERY_BLOCK + 2 * HALF_WINDOW
    lane = lax.broadcasted_iota(jnp.int32, (sub, LANES), 1)
    r = lax.broadcasted_iota(jnp.int32, (sub, win), 0)
    c = lax.broadcasted_iota(jnp.int32, (sub, win), 1)
    band = jnp.where((c >= r) & (c <= r + 2 * HALF_WINDOW), 0.0, NEG)
    return lane < HEAD_DIM, band, lax.broadcasted_iota(jnp.int32, (1, win), 1)


def _attn_body(tb, S, *refs):
    ins, (a_ref, og_ref, lg_ref) = refs[:-3], refs[-3:]
    sub, win = ATTN_QUERY_BLOCK, ATTN_QUERY_BLOCK + 2 * HALF_WINDOW
    lo_half, band, key = _attn_bias_terms()
    i = pl.program_id(1)
    for g, R in enumerate(DILATIONS):
        q_ref, kp_ref, kc_ref, kn_ref, vp_ref, vc_ref, vn_ref = ins[7 * g:7 * g + 7]
        tq, L = tb // R, S // R
        n_sub = tq // sub
        for r in range(R):
            sl = slice(r * LANES, (r + 1) * LANES)
            k2 = jnp.concatenate([kp_ref[:, sl], kc_ref[:, sl], kn_ref[:, sl]], axis=0)
            v2 = jnp.concatenate([vp_ref[:, sl], vc_ref[:, sl], vn_ref[:, sl]], axis=0)
            for sb in range(n_sub):
                bias = band
                if sb == 0 or sb == n_sub - 1:
                    kpos = i * tq + sb * sub - HALF_WINDOW + key
                    bias = band + jnp.where((kpos >= 0) & (kpos < L), 0.0, NEG)
                o, lse = _band_attend(q_ref[sb * sub:(sb + 1) * sub, sl].astype(F32),
                                      k2[sb * sub:sb * sub + win], v2[sb * sub:sb * sub + win], bias, lo_half)
                dst = pl.ds(R * sb * sub + r, sub, stride=R)
                og_ref[g, dst, :] = o
                lg_ref[g, dst, :] = lse
    chunk = min(ATTN_MERGE_CHUNK, tb)
    for t in range(0, tb, chunk):
        rows = slice(t, t + chunk)
        l1, l2, l3 = lg_ref[0, rows, :], lg_ref[1, rows, :], lg_ref[2, rows, :]
        mx = jnp.maximum(jnp.maximum(l1, l2), l3)
        e1, e2, e3 = jnp.exp(l1 - mx), jnp.exp(l2 - mx), jnp.exp(l3 - mx)
        a = (e1 * og_ref[0, rows, :] + e2 * og_ref[1, rows, :] + e3 * og_ref[2, rows, :]) * (1.0 / (e1 + e2 + e3))
        a_ref[rows, :] = a.astype(BF16)


def _halo_index(hb, nhalo, offset):
    return lambda b, i, p: (b, jnp.clip(i * hb + offset, 0, nhalo - 1), p)


def _attention(qkv_groups, S):
    B = qkv_groups[0][0].shape[0]
    tb = min(ATTN_TILE, S)
    specs, args = [], []
    for (q, k, v), R in zip(qkv_groups, DILATIONS):
        tq, L, wb = tb // R, S // R, R * LANES
        hb, nhalo = tq // HALF_WINDOW, L // HALF_WINDOW
        ctr = pl.BlockSpec((None, tq, wb), lambda b, i, p: (b, i, p))
        prev = pl.BlockSpec((None, HALF_WINDOW, wb), _halo_index(hb, nhalo, -1))
        nxt = pl.BlockSpec((None, HALF_WINDOW, wb), _halo_index(hb, nhalo, hb))
        specs += [ctr, prev, ctr, nxt, prev, ctr, nxt]
        args += [q, k, k, k, v, v, v]
    return pl.pallas_call(
        functools.partial(_attn_body, tb, S),
        grid=(B, S // tb, ATTN_WIDTH // LANES),
        in_specs=specs,
        out_specs=pl.BlockSpec((None, tb, LANES), lambda b, i, p: (b, i, p)),
        out_shape=jax.ShapeDtypeStruct((B, S, ATTN_WIDTH), BF16),
        scratch_shapes=[pltpu.VMEM((len(DILATIONS), tb, LANES), F32), pltpu.VMEM((len(DILATIONS), tb, LANES), F32)],
        compiler_params=_params(("parallel", "parallel", "parallel")),
        name="attn",
    )(*args)


def _mix_body(x_ref, f_ref, a_ref, wf_ref, wa_ref, g_ref, wq_ref, x1_ref, xn_ref, qp_ref):
    x1 = x_ref[...] + _dot(f_ref[...], wf_ref[...]) + _dot(a_ref[...], wa_ref[...])
    x1_ref[...] = x1
    ms = jnp.mean(x1 * x1, axis=-1, keepdims=True)
    xn = (x1 * lax.rsqrt(ms + EPS) * g_ref[...]).astype(BF16)
    xn_ref[...] = xn
    qp_ref[...] = _dot(xn, wq_ref[...]).astype(BF16)


def _mix(x, f_out, a_out, w_out_bf, g2, wq_bf):
    N = x.shape[0]
    tm = MIX_TILE
    row = lambda w: pl.BlockSpec((tm, w), lambda i: (i, 0))
    full = lambda r, c: pl.BlockSpec((r, c), lambda i: (0, 0))
    qw = wq_bf.shape[1]
    return pl.pallas_call(
        _mix_body,
        grid=(N // tm,),
        in_specs=[row(D_MODEL), row(FOURIER_WIDTH), row(ATTN_WIDTH),
                  full(FOURIER_WIDTH, D_MODEL), full(ATTN_WIDTH, D_MODEL), full(1, D_MODEL), full(D_MODEL, qw)],
        out_specs=[row(D_MODEL), row(D_MODEL), row(qw)],
        out_shape=[jax.ShapeDtypeStruct((N, D_MODEL), F32), jax.ShapeDtypeStruct((N, D_MODEL), BF16),
                   jax.ShapeDtypeStruct((N, qw), BF16)],
        compiler_params=_params(("parallel",)),
        name="mix",
    )(x, f_out, a_out, w_out_bf[:FOURIER_WIDTH], w_out_bf[FOURIER_WIDTH:], g2.reshape(1, D_MODEL), wq_bf)


def _argmax_rows(parts):
    while len(parts) > 1:
        nxt = []
        for k in range(0, len(parts) - 1, 2):
            (va, ia), (vb, ib) = parts[k], parts[k + 1]
            take_b = vb > va
            nxt.append((jnp.where(take_b, vb, va), jnp.where(take_b, ib, ia)))
        if len(parts) % 2:
            nxt.append(parts[-1])
        parts = nxt
    v, i = parts[0]
    m = jnp.max(v, axis=0, keepdims=True)
    return m, jnp.min(jnp.where(v == m, i, ID_NONE), axis=0, keepdims=True)


def _extract_topk_sorted(s, id_parts, depth):
    tq = s.shape[1]
    blocks = [(s[8 * k:8 * k + 8], ids) for k, ids in enumerate(id_parts)]
    groups = [blocks[g:g + depth] for g in range(0, len(blocks), depth)]
    for col in groups:
        for done in range(depth - 1):
            for j in range(depth - 1 - done):
                (va, ia), (vb, ib) = col[j], col[j + 1]
                swap = vb > va
                col[j] = (jnp.where(swap, vb, va), jnp.where(swap, ib, ia))
                col[j + 1] = (jnp.where(swap, va, vb), jnp.where(swap, ia, ib))
    rank = lax.broadcasted_iota(jnp.int32, (TOPK, tq), 0)
    vals = jnp.zeros((TOPK, tq), F32)
    idxs = jnp.zeros((TOPK, tq), F32)
    for kk in range(TOPK):
        m, first = _argmax_rows([col[0] for col in groups])
        for col in groups:
            won = col[0][1] == first
            for d in range(depth - 1):
                col[d] = (jnp.where(won, col[d + 1][0], col[d][0]), jnp.where(won, col[d + 1][1], col[d][1]))
            col[depth - 1] = (jnp.where(won, -jnp.inf, col[depth - 1][0]), col[depth - 1][1])
        vals = jnp.where(rank == kk, m, vals)
        idxs = jnp.where(rank == kk, first, idxs)
    return vals, idxs


def _select_rows(sel, table):
    out = jnp.zeros_like(sel)
    for a in range(TOPK):
        out = jnp.where(sel == float(a), table[a:a + 1, :], out)
    return out


def _candidate_ids():
    ids = list(range(TOPK))
    for a in range(1, 8):
        ids += [a * TOPK + b if (a + 1) * (b + 1) <= TOPK else ID_NONE for b in range(8)]
    ids += [a * TOPK for a in range(8, TOPK)]
    return np.asarray(ids, np.float32)


def _topk_head(q0, q1, k0, k1, key_ids, cand_ids, cand_id_parts):
    (v1, x1), (v2, x2) = [_extract_topk_sorted(_dot_nt(k, q), key_ids, 4) for k, q in ((k0, q0), (k1, q1))]
    cand = jnp.concatenate([v1[0:1, :] + v2] + [v1[a:a + 1, :] + v2[0:8, :] for a in range(1, 8)]
                           + [v1[8:TOPK, :] + v2[0:1, :]], axis=0)
    cand = jnp.where(cand_ids < ID_NONE, cand, -jnp.inf)
    top_s, flat = _extract_topk_sorted(cand, cand_id_parts, 5)
    a_sel = jnp.floor(flat * (1.0 / TOPK))
    b_sel = flat - a_sel * TOPK
    e = jnp.exp(top_s - top_s[0:1, :])
    return _select_rows(a_sel, x1), _select_rows(b_sel, x2), e * (1.0 / jnp.sum(e, axis=0, keepdims=True))


def _pack_bf16_pair(a, b):
    abits = lax.bitcast_convert_type(a.astype(BF16).astype(F32), jnp.uint32)
    bbits = lax.bitcast_convert_type(b.astype(BF16).astype(F32), jnp.uint32)
    return lax.bitcast_convert_type(abits | (bbits >> 16), jnp.int32)


def _unpack_bf16_pair(word):
    bits = lax.bitcast_convert_type(word, jnp.uint32)
    hi = lax.bitcast_convert_type(bits & jnp.uint32(0xFFFF0000), F32)
    lo = lax.bitcast_convert_type(bits << 16, F32)
    return hi, lo

def _peer_body(tq, nh, x1_ref, xn_ref, qp_ref, keys_ref, ids_ref, u_ref, v_ref, gf_ref,
               o_ref, gate_ref, sel_ref, i1_ref, g_ref, i2t_ref):
    i = pl.program_id(0)
    c = pl.program_id(1)
    half_keys = N_KEYS // 2
    slot = lax.rem(i, 2)

    @pl.when(jnp.logical_and(c == 0, i == 0))
    def _():
        gate_ref[...] = jnp.zeros_like(gate_ref)

    @pl.when(jnp.logical_and(c == 0, i > 0))
    def _():
        for grp in range(tq // LANES):
            cols = slice(grp * LANES, (grp + 1) * LANES)
            i1_ref[cols, :] = sel_ref[1 - slot, 0, :, cols].T.astype(jnp.int32)
            i2t_ref[grp] = sel_ref[1 - slot, 1, :, cols].astype(jnp.int32)
            g_ref[cols, :] = sel_ref[1 - slot, 2, :, cols].T
        row = lax.broadcasted_iota(jnp.int32, (N_KEYS, N_KEYS), 0)
        lane = lax.broadcasted_iota(jnp.int32, (N_KEYS, N_KEYS), 1)

        def build(grp, carry):
            base = pl.multiple_of(grp * LANES, LANES)
            i1g, gg = i1_ref[pl.ds(base, LANES), :], g_ref[pl.ds(base, LANES), :]
            i2t = i2t_ref[grp]
            for r in range(LANES):
                p1 = jnp.where(i1g[r:r + 1, :] == row, gg[r:r + 1, :], 0.0).astype(BF16)
                p2 = jnp.where(i2t[:, r:r + 1] == lane, 1.0, 0.0).astype(BF16)
                gt = _dot(p1, p2)
                word = _pack_bf16_pair(gt[:half_keys], gt[half_keys:])
                gate_ref[pl.ds(pl.multiple_of((base + r) * G_PITCH, 8), half_keys), :] = word
            return carry

        lax.fori_loop(0, tq // LANES, build, 0)

    @pl.when(c == 0)
    def _():
        o_ref[...] = jnp.zeros_like(o_ref)

    sub = lax.broadcasted_iota(jnp.int32, (8, tq), 0).astype(F32)
    key_ids = [sub + float(r) for r in range(0, N_KEYS, 8)]
    cand_ids = ids_ref[...]
    cand_id_parts = [cand_ids[r:r + 8] for r in range(0, cand_ids.shape[0], 8)]
    sel = _topk_head(qp_ref[:, :N_KEYS], qp_ref[:, N_KEYS:], keys_ref[2 * c], keys_ref[2 * c + 1],
                     key_ids, cand_ids, cand_id_parts)
    rows = pl.ds(pl.multiple_of(c * TOPK, TOPK), TOPK)
    for k, val in enumerate(sel):
        sel_ref[slot, k, rows, :] = val

    gates = [_unpack_bf16_pair(gate_ref[pl.ds(c * nh + ii, tq, stride=G_PITCH), :])
             for ii in range(nh)]
    a = _dot(xn_ref[...], u_ref[...])
    act = 0.5 * a * (1.0 + lax.erf(a * float(np.sqrt(0.5))))
    w = jnp.concatenate([(act[:, (half * nh + ii) * N_KEYS:(half * nh + ii + 1) * N_KEYS] * gates[ii][half]).astype(BF16)
                         for half in range(2) for ii in range(nh)], axis=1)
    o_ref[...] += _dot(w, v_ref[...])

    @pl.when(c == pl.num_programs(1) - 1)
    def _():
        y = x1_ref[...] + o_ref[...]
        ms = jnp.mean(y * y, axis=-1, keepdims=True)
        o_ref[...] = y * lax.rsqrt(ms + EPS) * gf_ref[...]


def _peer_tables(u, v):
    ec = PEER_NH * N_KEYS
    n_chunks = N_EXPERTS // (2 * ec)
    u4 = u.astype(BF16).reshape(2, n_chunks, ec, D_MODEL)
    v4 = v.astype(BF16).reshape(2, n_chunks, ec, D_MODEL)
    return (u4.transpose(1, 3, 0, 2).reshape(n_chunks, D_MODEL, 2 * ec),
            v4.transpose(1, 0, 2, 3).reshape(n_chunks, 2 * ec, D_MODEL))


def _peer(x1, xn, qp, keys_bf, u_blk, v_blk, gf):
    N = x1.shape[0]
    tq = PEER_TILE
    nh = PEER_NH
    ec = nh * N_KEYS
    n_chunks = (N_KEYS // 2) // nh
    assert n_chunks == PEER_HEADS
    nt = N // tq
    nsel = PEER_HEADS * TOPK
    ids = _candidate_ids()
    ids = jnp.asarray(np.broadcast_to(ids[:, None], (ids.shape[0], tq)))
    prev = lambda i: jnp.maximum(i - 1, 0)
    tok = lambda w: pl.BlockSpec((tq, w), lambda i, c: (prev(i), 0), pipeline_mode=pl.Buffered(1))
    const = lambda shape: pl.BlockSpec(shape, lambda i, c: (0,) * len(shape))
    return pl.pallas_call(
        functools.partial(_peer_body, tq, nh),
        grid=(nt + 1, n_chunks),
        in_specs=[tok(D_MODEL), tok(D_MODEL),
                  pl.BlockSpec((tq, 2 * N_KEYS), lambda i, c: (jnp.minimum(i, nt - 1), c)),
                  const((2 * PEER_HEADS, N_KEYS, N_KEYS)), const(ids.shape),
                  pl.BlockSpec((None, D_MODEL, 2 * ec), lambda i, c: (c, 0, 0)),
                  pl.BlockSpec((None, 2 * ec, D_MODEL), lambda i, c: (c, 0, 0)),
                  const((1, D_MODEL))],
        out_specs=pl.BlockSpec((tq, D_MODEL), lambda i, c: (prev(i), 0)),
        out_shape=jax.ShapeDtypeStruct((N, D_MODEL), F32),
        scratch_shapes=[pltpu.VMEM((tq * G_PITCH, N_KEYS), jnp.int32),
                        pltpu.VMEM((2, 3, nsel, tq), F32),
                        pltpu.VMEM((tq, nsel), jnp.int32),
                        pltpu.VMEM((tq, nsel), F32),
                        pltpu.VMEM((tq // LANES, nsel, LANES), jnp.int32)],
        compiler_params=_params(("arbitrary", "arbitrary")),
        name="peer",
    )(x1, xn, qp, keys_bf, ids, u_blk, v_blk, gf.reshape(1, D_MODEL))


def _encoder(x, w):
    B, S, _ = x.shape
    f_in, qkv1, qkv4, qkv16 = _proj(x, w["norm1_g"], w["w_in"])
    f_out = _fourier(f_in, w["w_fourier"])
    a_out = _attention([qkv1, qkv4, qkv16], S)
    N = B * S
    flat = lambda a: a.reshape(N, a.shape[-1])
    x1, xn, qp = _mix(flat(x), flat(f_out), flat(a_out), w["w_out"], w["norm2_g"], w["w_query"])
    y = _peer(x1, xn, qp, w["sub_keys"], w["expert_u"], w["expert_v"], w["final_g"])
    return y.reshape(B, S, D_MODEL)


def kernel(x_prompt, x_sample, norm1_g, w_in, w_fourier, w_out, norm2_g, w_query, sub_keys, expert_u, expert_v, final_g):
    u_blk, v_blk = _peer_tables(expert_u[0], expert_v[0])
    w = {
        "norm1_g": norm1_g[0], "w_in": w_in[0].astype(BF16), "w_fourier": w_fourier[0],
        "w_out": w_out[0].astype(BF16), "norm2_g": norm2_g[0], "w_query": w_query[0].astype(BF16),
        "sub_keys": sub_keys[0].reshape(2 * PEER_HEADS, N_KEYS, N_KEYS).astype(BF16),
        "expert_u": u_blk, "expert_v": v_blk, "final_g": final_g,
    }
    return (_encoder(x_prompt, w), _encoder(x_sample, w))
```

```python
import functools

import numpy as np
import jax
import jax.numpy as jnp
from jax import lax
from jax.experimental import pallas as pl
from jax.experimental.pallas import tpu as pltpu

F32 = jnp.float32
BF16 = jnp.bfloat16

D_MODEL = 1024
HEAD_DIM = 64
FOURIER_WIDTH = 512
ATTN_WIDTH = 512
IN_WIDTH = 2048
N_GROUPS = 8
DILATIONS = (1, 4, 16)
HALF_WINDOW = 64
ROPE_THETA = 500000.0
ROPE_DIM = 16
PEER_HEADS = 8
N_KEYS = 128
N_EXPERTS = N_KEYS * N_KEYS
TOPK = 16
EPS = 1e-6
NEG = -1e30

LANES = 128
DFT_INNER = 128
G_PITCH = 72
ID_NONE = 1e9
VMEM_LIMIT = 56 * 2**20

PROJ_TILE = 512
DFT_COL_TILE = 4096
DFT_ROW_TILE = 8
ATTN_QUERY_BLOCK = 128
ATTN_TILE = 2048
ATTN_MERGE_CHUNK = 256
MIX_TILE = 512
PEER_TILE = 512
PEER_NH = 8


def _params(sem, vmem=None):
    return pltpu.CompilerParams(dimension_semantics=sem, vmem_limit_bytes=vmem or VMEM_LIMIT)


def _dot(a, b):
    return jnp.dot(a, b, preferred_element_type=F32)


def _dot_nt(a, b):
    return lax.dot_general(a, b, (((1,), (1,)), ((), ())), preferred_element_type=F32)


def _proj_body(tm, x_ref, g_ref, w_ref, c_ref, sa_ref, sb_ref,
               f_ref, q_ref, k_ref, v_ref, q4_ref, k4_ref, v4_ref, q16_ref, k16_ref, v16_ref, stage_ref, hop_ref):
    x = x_ref[...]
    ms = jnp.mean(x * x, axis=-1, keepdims=True)
    h = (x * lax.rsqrt(ms + EPS) * g_ref[...]).astype(BF16)
    p = _dot(h, w_ref[...])
    f_ref[...] = p[:, :FOURIER_WIDTH].astype(BF16)
    c, sa, sb = c_ref[...], sa_ref[...], sb_ref[...]
    for j in range(ATTN_WIDTH // LANES):
        lo = j * LANES
        qc = p[:, FOURIER_WIDTH + lo:FOURIER_WIDTH + lo + LANES]
        kc = p[:, FOURIER_WIDTH + ATTN_WIDTH + lo:FOURIER_WIDTH + ATTN_WIDTH + lo + LANES]
        vc = p[:, FOURIER_WIDTH + 2 * ATTN_WIDTH + lo:FOURIER_WIDTH + 2 * ATTN_WIDTH + lo + LANES]
        qr = (qc * c + pltpu.roll(qc, LANES - 8, 1) * sa + pltpu.roll(qc, 8, 1) * sb) * (HEAD_DIM ** -0.5)
        kr = kc * c + pltpu.roll(kc, LANES - 8, 1) * sa + pltpu.roll(kc, 8, 1) * sb
        for a, (val, nat) in enumerate(((qr, q_ref), (kr, k_ref), (vc, v_ref))):
            nat[:, lo:lo + LANES] = val.astype(BF16)
            stage_ref[a, j] = val
    _, R1, R2 = DILATIONS
    step = R2 // R1
    for a, (d1_ref, d2_ref) in enumerate(((q4_ref, q16_ref), (k4_ref, k16_ref), (v4_ref, v16_ref))):
        for j in range(ATTN_WIDTH // LANES):
            for r in range(R1):
                piece = stage_ref[a, j, pl.ds(r, tm // R1, stride=R1), :]
                d1_ref[:, (j * R1 + r) * LANES:(j * R1 + r + 1) * LANES] = piece.astype(BF16)
                hop_ref[r] = piece
            for r2 in range(R2):
                piece = hop_ref[r2 % R1, pl.ds(r2 // R1, tm // R2, stride=step), :]
                d2_ref[:, (j * R2 + r2) * LANES:(j * R2 + r2 + 1) * LANES] = piece.astype(BF16)


def _rope_tables(S):
    half = ROPE_DIM // 2
    inv = ROPE_THETA ** (-(jnp.arange(half, dtype=F32) * 2.0) / ROPE_DIM)
    ang = jnp.arange(S, dtype=F32)[:, None] * inv[None, :]
    cos, sin = jnp.cos(ang), jnp.sin(ang)
    one = jnp.ones((S, HEAD_DIM - ROPE_DIM), F32)
    zero = jnp.zeros((S, HEAD_DIM - ROPE_DIM), F32)
    z8 = jnp.zeros((S, half), F32)
    c = jnp.concatenate([cos, cos, one], axis=1)
    sa = jnp.concatenate([-sin, z8, zero], axis=1)
    sb = jnp.concatenate([z8, sin, zero], axis=1)
    rep = LANES // HEAD_DIM
    return jnp.tile(c, (1, rep)), jnp.tile(sa, (1, rep)), jnp.tile(sb, (1, rep))


def _proj(x, g, w_bf):
    B, S, _ = x.shape
    tm = min(PROJ_TILE, S)
    c, sa, sb = _rope_tables(S)
    tab = pl.BlockSpec((tm, LANES), lambda b, i: (i, 0))
    out = pl.BlockSpec((None, tm, ATTN_WIDTH), lambda b, i: (b, i, 0))
    shp = jax.ShapeDtypeStruct((B, S, ATTN_WIDTH), BF16)
    dil_specs, dil_shapes = [], []
    for R in DILATIONS[1:]:
        dil_specs.append(pl.BlockSpec((None, tm // R, R * ATTN_WIDTH), lambda b, i: (b, i, 0)))
        dil_shapes.append(jax.ShapeDtypeStruct((B, S // R, R * ATTN_WIDTH), BF16))
    outs = pl.pallas_call(
        functools.partial(_proj_body, tm),
        grid=(B, S // tm),
        in_specs=[pl.BlockSpec((None, tm, D_MODEL), lambda b, i: (b, i, 0)),
                  pl.BlockSpec((1, D_MODEL), lambda b, i: (0, 0)),
                  pl.BlockSpec((D_MODEL, IN_WIDTH), lambda b, i: (0, 0)),
                  tab, tab, tab],
        out_specs=[out] * 4 + [dil_specs[0]] * 3 + [dil_specs[1]] * 3,
        out_shape=[shp] * 4 + [dil_shapes[0]] * 3 + [dil_shapes[1]] * 3,
        scratch_shapes=[pltpu.VMEM((3, ATTN_WIDTH // LANES, tm, LANES), F32),
                        pltpu.VMEM((DILATIONS[1], tm // DILATIONS[1], LANES), F32)],
        compiler_params=_params(("parallel", "parallel")),
        name="proj",
    )(x, g.reshape(1, D_MODEL), w_bf, c, sa, sb)
    return outs[0], outs[1:4], outs[4:7], outs[7:10]


def _cos_sin(n_rows, n_cols, period):
    i = jnp.arange(n_rows, dtype=jnp.int32)[:, None]
    j = jnp.arange(n_cols, dtype=jnp.int32)[None, :]
    ang = ((i * j) % period).astype(F32) * (2.0 * np.pi / period)
    return jnp.cos(ang), jnp.sin(ang)


def _dft_a_body(n1, d_ref, x_ref, yr_ref, yi_ref):
    y = _dot(d_ref[...], x_ref[...])
    yr_ref[...] = y[:n1].astype(BF16)
    yi_ref[...] = y[n1:].astype(BF16)


def _dft_b_body(ct, scale, yr_ref, yi_ref, tc_ref, ts_ref, d2_ref, c64_ref, s64_ref, wf_ref, o_ref):
    rep = FOURIER_WIDTH // LANES
    xr, xi = [], []
    for cc in range(ct):
        yr = yr_ref[cc].astype(F32)
        yi = yi_ref[cc].astype(F32)
        tc = jnp.concatenate([tc_ref[cc]] * rep, axis=1)
        ts = jnp.concatenate([ts_ref[cc]] * rep, axis=1)
        zr = yr * tc + yi * ts
        zi = yi * tc - yr * ts
        z = jnp.concatenate([zr, zi], axis=0).astype(BF16)
        xx = _dot(d2_ref[...], z)
        xr.append(xx[:DFT_INNER].astype(BF16))
        xi.append(xx[DFT_INNER:].astype(BF16))
    xr, xi = jnp.concatenate(xr, axis=0), jnp.concatenate(xi, axis=0)
    re = (_dot(xr, c64_ref[...]) + _dot(xi, s64_ref[...])) * scale
    y = _dot(re.astype(BF16), wf_ref[...]).astype(BF16)
    for cc in range(ct):
        o_ref[cc] = y[cc * DFT_INNER:(cc + 1) * DFT_INNER]


def _block_diag(m):
    G, n, _ = m.shape
    eye = jnp.eye(G, dtype=m.dtype)
    return (eye[:, None, :, None] * m[:, :, None, :]).reshape(G * n, G * n)


def _fourier(f_in, w_fourier):
    B, S, W = f_in.shape
    n2 = DFT_INNER
    n1 = S // n2
    cols = n2 * W
    tcw = DFT_COL_TILE
    c1, s1 = _cos_sin(n1, n1, n1)
    d1 = jnp.concatenate([c1, -s1], axis=0).astype(BF16)
    xa = f_in.reshape(B, n1, cols)
    blk = pl.BlockSpec((None, n1, tcw), lambda b, j: (b, 0, j))
    yshape = jax.ShapeDtypeStruct((B, n1, cols), BF16)
    yr, yi = pl.pallas_call(
        functools.partial(_dft_a_body, n1),
        grid=(B, cols // tcw),
        in_specs=[pl.BlockSpec((2 * n1, n1), lambda b, j: (0, 0)), blk],
        out_specs=[blk, blk],
        out_shape=[yshape, yshape],
        compiler_params=_params(("parallel", "parallel")),
        name="dft_a",
    )(d1, xa)

    yr = yr.reshape(B, n1, n2, W)
    yi = yi.reshape(B, n1, n2, W)
    twc, tws = _cos_sin(n1, n2, S)
    twc = jnp.broadcast_to(twc[:, :, None], (n1, n2, LANES))
    tws = jnp.broadcast_to(tws[:, :, None], (n1, n2, LANES))
    c2, s2 = _cos_sin(n2, n2, n2)
    d2 = jnp.concatenate([jnp.concatenate([c2, s2], axis=1),
                          jnp.concatenate([-s2, c2], axis=1)], axis=0).astype(BF16)
    cg, sg = _cos_sin(HEAD_DIM, HEAD_DIM, HEAD_DIM)
    c64 = _block_diag(jnp.broadcast_to(cg, (N_GROUPS, HEAD_DIM, HEAD_DIM))).astype(BF16)
    s64 = _block_diag(jnp.broadcast_to(sg, (N_GROUPS, HEAD_DIM, HEAD_DIM))).astype(BF16)
    wf = _block_diag(w_fourier).astype(BF16)
    ct = DFT_ROW_TILE
    scale = float(1.0 / np.sqrt(S * HEAD_DIM))
    yblk = pl.BlockSpec((None, ct, n2, W), lambda b, i: (b, i, 0, 0))
    tblk = pl.BlockSpec((ct, n2, LANES), lambda b, i: (i, 0, 0))
    full = lambda r, c: pl.BlockSpec((r, c), lambda b, i: (0, 0))
    out = pl.pallas_call(
        functools.partial(_dft_b_body, ct, scale),
        grid=(B, n1 // ct),
        in_specs=[yblk, yblk, tblk, tblk, full(2 * n2, 2 * n2), full(W, W), full(W, W), full(W, W)],
        out_specs=yblk,
        out_shape=jax.ShapeDtypeStruct((B, n1, n2, W), BF16),
        compiler_params=_params(("parallel", "parallel")),
        name="dft_b",
    )(yr, yi, twc, tws, d2, c64, s64, wf)
    return out.transpose(0, 2, 1, 3).reshape(B, S, W)


def _band_attend(qf, ks, vs, bias, lo_half):
    n = qf.shape[0]
    q2 = jnp.concatenate([jnp.where(lo_half, qf, 0.0), jnp.where(lo_half, 0.0, qf)], axis=0).astype(BF16)
    s = _dot_nt(q2, ks) + jnp.concatenate([bias, bias], axis=0)
    m = jnp.max(s, axis=1, keepdims=True)
    p = jnp.exp(s - m)
    l = jnp.sum(p, axis=1, keepdims=True)
    o = _dot(p.astype(BF16), vs) * (1.0 / l)
    lse = m + jnp.log(l)
    return jnp.where(lo_half, o[:n], o[n:]), jnp.where(lo_half, lse[:n], lse[n:])


def _attn_bias_terms():
    sub, win = ATTN_QUERY_BLOCK, ATTN_QUERY_BLOCK + 2 * HALF_WINDOW
    lane = lax.broadcasted_iota(jnp.int32, (sub, LANES), 1)
    r = lax.broadcasted_iota(jnp.int32, (sub, win), 0)
    c = lax.broadcasted_iota(jnp.int32, (sub, win), 1)
    band = jnp.where((c >= r) & (c <= r + 2 * HALF_WINDOW), 0.0, NEG)
    return lane < HEAD_DIM, band, lax.broadcasted_iota(jnp.int32, (1, win), 1)


def _attn_body(tb, S, *refs):
    ins, (a_ref, og_ref, lg_ref) = refs[:-3], refs[-3:]
    sub, win = ATTN_QUERY_BLOCK, ATTN_QUERY_BLOCK + 2 * HALF_WINDOW
    lo_half, band, key = _attn_bias_terms()
    i = pl.program_id(1)
    for g, R in enumerate(DILATIONS):
        q_ref, kp_ref, kc_ref, kn_ref, vp_ref, vc_ref, vn_ref = ins[7 * g:7 * g + 7]
        tq, L = tb // R, S // R
        n_sub = tq // sub
        for r in range(R):
            sl = slice(r * LANES, (r + 1) * LANES)
            k2 = jnp.concatenate([kp_ref[:, sl], kc_ref[:, sl], kn_ref[:, sl]], axis=0)
            v2 = jnp.concatenate([vp_ref[:, sl], vc_ref[:, sl], vn_ref[:, sl]], axis=0)
            for sb in range(n_sub):
                bias = band
                if sb == 0 or sb == n_sub - 1:
                    kpos = i * tq + sb * sub - HALF_WINDOW + key
                    bias = band + jnp.where((kpos >= 0) & (kpos < L), 0.0, NEG)
                o, lse = _band_attend(q_ref[sb * sub:(sb + 1) * sub, sl].astype(F32),
                                      k2[sb * sub:sb * sub + win], v2[sb * sub:sb * sub + win], bias, lo_half)
                dst = pl.ds(R * sb * sub + r, sub, stride=R)
                og_ref[g, dst, :] = o
                lg_ref[g, dst, :] = lse
    chunk = min(ATTN_MERGE_CHUNK, tb)
    for t in range(0, tb, chunk):
        rows = slice(t, t + chunk)
        l1, l2, l3 = lg_ref[0, rows, :], lg_ref[1, rows, :], lg_ref[2, rows, :]
        mx = jnp.maximum(jnp.maximum(l1, l2), l3)
        e1, e2, e3 = jnp.exp(l1 - mx), jnp.exp(l2 - mx), jnp.exp(l3 - mx)
        a = (e1 * og_ref[0, rows, :] + e2 * og_ref[1, rows, :] + e3 * og_ref[2, rows, :]) * (1.0 / (e1 + e2 + e3))
        a_ref[rows, :] = a.astype(BF16)


def _halo_index(hb, nhalo, offset):
    return lambda b, i, p: (b, jnp.clip(i * hb + offset, 0, nhalo - 1), p)


def _attention(qkv_groups, S):
    B = qkv_groups[0][0].shape[0]
    tb = min(ATTN_TILE, S)
    specs, args = [], []
    for (q, k, v), R in zip(qkv_groups, DILATIONS):
        tq, L, wb = tb // R, S // R, R * LANES
        hb, nhalo = tq // HALF_WINDOW, L // HALF_WINDOW
        ctr = pl.BlockSpec((None, tq, wb), lambda b, i, p: (b, i, p))
        prev = pl.BlockSpec((None, HALF_WINDOW, wb), _halo_index(hb, nhalo, -1))
        nxt = pl.BlockSpec((None, HALF_WINDOW, wb), _halo_index(hb, nhalo, hb))
        specs += [ctr, prev, ctr, nxt, prev, ctr, nxt]
        args += [q, k, k, k, v, v, v]
    return pl.pallas_call(
        functools.partial(_attn_body, tb, S),
        grid=(B, S // tb, ATTN_WIDTH // LANES),
        in_specs=specs,
        out_specs=pl.BlockSpec((None, tb, LANES), lambda b, i, p: (b, i, p)),
        out_shape=jax.ShapeDtypeStruct((B, S, ATTN_WIDTH), BF16),
        scratch_shapes=[pltpu.VMEM((len(DILATIONS), tb, LANES), F32), pltpu.VMEM((len(DILATIONS), tb, LANES), F32)],
        compiler_params=_params(("parallel", "parallel", "parallel")),
        name="attn",
    )(*args)


def _mix_body(x_ref, f_ref, a_ref, wf_ref, wa_ref, g_ref, wq_ref, x1_ref, xn_ref, qp_ref):
    x1 = x_ref[...] + _dot(f_ref[...], wf_ref[...]) + _dot(a_ref[...], wa_ref[...])
    x1_ref[...] = x1
    ms = jnp.mean(x1 * x1, axis=-1, keepdims=True)
    xn = (x1 * lax.rsqrt(ms + EPS) * g_ref[...]).astype(BF16)
    xn_ref[...] = xn
    qp_ref[...] = _dot(xn, wq_ref[...]).astype(BF16)


def _mix(x, f_out, a_out, w_out_bf, g2, wq_bf):
    N = x.shape[0]
    tm = MIX_TILE
    row = lambda w: pl.BlockSpec((tm, w), lambda i: (i, 0))
    full = lambda r, c: pl.BlockSpec((r, c), lambda i: (0, 0))
    qw = wq_bf.shape[1]
    return pl.pallas_call(
        _mix_body,
        grid=(N // tm,),
        in_specs=[row(D_MODEL), row(FOURIER_WIDTH), row(ATTN_WIDTH),
                  full(FOURIER_WIDTH, D_MODEL), full(ATTN_WIDTH, D_MODEL), full(1, D_MODEL), full(D_MODEL, qw)],
        out_specs=[row(D_MODEL), row(D_MODEL), row(qw)],
        out_shape=[jax.ShapeDtypeStruct((N, D_MODEL), F32), jax.ShapeDtypeStruct((N, D_MODEL), BF16),
                   jax.ShapeDtypeStruct((N, qw), BF16)],
        compiler_params=_params(("parallel",)),
        name="mix",
    )(x, f_out, a_out, w_out_bf[:FOURIER_WIDTH], w_out_bf[FOURIER_WIDTH:], g2.reshape(1, D_MODEL), wq_bf)


def _argmax_rows(parts):
    while len(parts) > 1:
        nxt = []
        for k in range(0, len(parts) - 1, 2):
            (va, ia), (vb, ib) = parts[k], parts[k + 1]
            take_b = vb > va
            nxt.append((jnp.where(take_b, vb, va), jnp.where(take_b, ib, ia)))
        if len(parts) % 2:
            nxt.append(parts[-1])
        parts = nxt
    v, i = parts[0]
    m = jnp.max(v, axis=0, keepdims=True)
    return m, jnp.min(jnp.where(v == m, i, ID_NONE), axis=0, keepdims=True)


def _extract_topk_sorted(s, id_parts, depth):
    tq = s.shape[1]
    blocks = [(s[8 * k:8 * k + 8], ids) for k, ids in enumerate(id_parts)]
    groups = [blocks[g:g + depth] for g in range(0, len(blocks), depth)]
    for col in groups:
        for done in range(depth - 1):
            for j in range(depth - 1 - done):
                (va, ia), (vb, ib) = col[j], col[j + 1]
                swap = vb > va
                col[j] = (jnp.where(swap, vb, va), jnp.where(swap, ib, ia))
                col[j + 1] = (jnp.where(swap, va, vb), jnp.where(swap, ia, ib))
    rank = lax.broadcasted_iota(jnp.int32, (TOPK, tq), 0)
    vals = jnp.zeros((TOPK, tq), F32)
    idxs = jnp.zeros((TOPK, tq), F32)
    for kk in range(TOPK):
        m, first = _argmax_rows([col[0] for col in groups])
        for col in groups:
            won = col[0][1] == first
            for d in range(depth - 1):
                col[d] = (jnp.where(won, col[d + 1][0], col[d][0]), jnp.where(won, col[d + 1][1], col[d][1]))
            col[depth - 1] = (jnp.where(won, -jnp.inf, col[depth - 1][0]), col[depth - 1][1])
        vals = jnp.where(rank == kk, m, vals)
        idxs = jnp.where(rank == kk, first, idxs)
    return vals, idxs


def _select_rows(sel, table):
    out = jnp.zeros_like(sel)
    for a in range(TOPK):
        out = jnp.where(sel == float(a), table[a:a + 1, :], out)
    return out


def _candidate_ids():
    ids = list(range(TOPK))
    for a in range(1, 8):
        ids += [a * TOPK + b if (a + 1) * (b + 1) <= TOPK else ID_NONE for b in range(8)]
    ids += [a * TOPK for a in range(8, TOPK)]
    return np.asarray(ids, np.float32)


def _topk_head(q0, q1, k0, k1, key_ids, cand_ids, cand_id_parts):
    (v1, x1), (v2, x2) = [_extract_topk_sorted(_dot_nt(k, q), key_ids, 4) for k, q in ((k0, q0), (k1, q1))]
    cand = jnp.concatenate([v1[0:1, :] + v2] + [v1[a:a + 1, :] + v2[0:8, :] for a in range(1, 8)]
                           + [v1[8:TOPK, :] + v2[0:1, :]], axis=0)
    cand = jnp.where(cand_ids < ID_NONE, cand, -jnp.inf)
    top_s, flat = _extract_topk_sorted(cand, cand_id_parts, 5)
    a_sel = jnp.floor(flat * (1.0 / TOPK))
    b_sel = flat - a_sel * TOPK
    e = jnp.exp(top_s - top_s[0:1, :])
    return _select_rows(a_sel, x1), _select_rows(b_sel, x2), e * (1.0 / jnp.sum(e, axis=0, keepdims=True))


def _pack_bf16_pair(a, b):
    abits = lax.bitcast_convert_type(a.astype(BF16).astype(F32), jnp.uint32)
    bbits = lax.bitcast_convert_type(b.astype(BF16).astype(F32), jnp.uint32)
    return lax.bitcast_convert_type(abits | (bbits >> 16), jnp.int32)


def _unpack_bf16_pair(word):
    bits = lax.bitcast_convert_type(word, jnp.uint32)
    hi = lax.bitcast_convert_type(bits & jnp.uint32(0xFFFF0000), F32)
    lo = lax.bitcast_convert_type(bits << 16, F32)
    return hi, lo

def _peer_body(tq, nh, x1_ref, xn_ref, qp_ref, keys_ref, ids_ref, u_ref, v_ref, gf_ref,
               o_ref, gate_ref, sel_ref, i1_ref, g_ref, i2t_ref):
    i = pl.program_id(0)
    c = pl.program_id(1)
    half_keys = N_KEYS // 2
    slot = lax.rem(i, 2)

    @pl.when(jnp.logical_and(c == 0, i == 0))
    def _():
        gate_ref[...] = jnp.zeros_like(gate_ref)

    @pl.when(jnp.logical_and(c == 0, i > 0))
    def _():
        for grp in range(tq // LANES):
            cols = slice(grp * LANES, (grp + 1) * LANES)
            i1_ref[cols, :] = sel_ref[1 - slot, 0, :, cols].T.astype(jnp.int32)
            i2t_ref[grp] = sel_ref[1 - slot, 1, :, cols].astype(jnp.int32)
            g_ref[cols, :] = sel_ref[1 - slot, 2, :, cols].T
        row = lax.broadcasted_iota(jnp.int32, (N_KEYS, N_KEYS), 0)
        lane = lax.broadcasted_iota(jnp.int32, (N_KEYS, N_KEYS), 1)

        def build(grp, carry):
            base = pl.multiple_of(grp * LANES, LANES)
            i1g, gg = i1_ref[pl.ds(base, LANES), :], g_ref[pl.ds(base, LANES), :]
            i2t = i2t_ref[grp]
            for r in range(LANES):
                p1 = jnp.where(i1g[r:r + 1, :] == row, gg[r:r + 1, :], 0.0).astype(BF16)
                p2 = jnp.where(i2t[:, r:r + 1] == lane, 1.0, 0.0).astype(BF16)
                gt = _dot(p1, p2)
                word = _pack_bf16_pair(gt[:half_keys], gt[half_keys:])
                gate_ref[pl.ds(pl.multiple_of((base + r) * G_PITCH, 8), half_keys), :] = word
            return carry

        lax.fori_loop(0, tq // LANES, build, 0)

    @pl.when(c == 0)
    def _():
        o_ref[...] = jnp.zeros_like(o_ref)

    sub = lax.broadcasted_iota(jnp.int32, (8, tq), 0).astype(F32)
    key_ids = [sub + float(r) for r in range(0, N_KEYS, 8)]
    cand_ids = ids_ref[...]
    cand_id_parts = [cand_ids[r:r + 8] for r in range(0, cand_ids.shape[0], 8)]
    sel = _topk_head(qp_ref[:, :N_KEYS], qp_ref[:, N_KEYS:], keys_ref[2 * c], keys_ref[2 * c + 1],
                     key_ids, cand_ids, cand_id_parts)
    rows = pl.ds(pl.multiple_of(c * TOPK, TOPK), TOPK)
    for k, val in enumerate(sel):
        sel_ref[slot, k, rows, :] = val

    gates = [_unpack_bf16_pair(gate_ref[pl.ds(c * nh + ii, tq, stride=G_PITCH), :])
             for ii in range(nh)]
    a = _dot(xn_ref[...], u_ref[...])
    act = 0.5 * a * (1.0 + lax.erf(a * float(np.sqrt(0.5))))
    w = jnp.concatenate([(act[:, (half * nh + ii) * N_KEYS:(half * nh + ii + 1) * N_KEYS] * gates[ii][half]).astype(BF16)
                         for half in range(2) for ii in range(nh)], axis=1)
    o_ref[...] += _dot(w, v_ref[...])

    @pl.when(c == pl.num_programs(1) - 1)
    def _():
        y = x1_ref[...] + o_ref[...]
        ms = jnp.mean(y * y, axis=-1, keepdims=True)
        o_ref[...] = y * lax.rsqrt(ms + EPS) * gf_ref[...]


def _peer_tables(u, v):
    ec = PEER_NH * N_KEYS
    n_chunks = N_EXPERTS // (2 * ec)
    u4 = u.astype(BF16).reshape(2, n_chunks, ec, D_MODEL)
    v4 = v.astype(BF16).reshape(2, n_chunks, ec, D_MODEL)
    return (u4.transpose(1, 3, 0, 2).reshape(n_chunks, D_MODEL, 2 * ec),
            v4.transpose(1, 0, 2, 3).reshape(n_chunks, 2 * ec, D_MODEL))


def _peer(x1, xn, qp, keys_bf, u_blk, v_blk, gf):
    N = x1.shape[0]
    tq = PEER_TILE
    nh = PEER_NH
    ec = nh * N_KEYS
    n_chunks = (N_KEYS // 2) // nh
    assert n_chunks == PEER_HEADS
    nt = N // tq
    nsel = PEER_HEADS * TOPK
    ids = _candidate_ids()
    ids = jnp.asarray(np.broadcast_to(ids[:, None], (ids.shape[0], tq)))
    prev = lambda i: jnp.maximum(i - 1, 0)
    tok = lambda w: pl.BlockSpec((tq, w), lambda i, c: (prev(i), 0), pipeline_mode=pl.Buffered(1))
    const = lambda shape: pl.BlockSpec(shape, lambda i, c: (0,) * len(shape))
    return pl.pallas_call(
        functools.partial(_peer_body, tq, nh),
        grid=(nt + 1, n_chunks),
        in_specs=[tok(D_MODEL), tok(D_MODEL),
                  pl.BlockSpec((tq, 2 * N_KEYS), lambda i, c: (jnp.minimum(i, nt - 1), c)),
                  const((2 * PEER_HEADS, N_KEYS, N_KEYS)), const(ids.shape),
                  pl.BlockSpec((None, D_MODEL, 2 * ec), lambda i, c: (c, 0, 0)),
                  pl.BlockSpec((None, 2 * ec, D_MODEL), lambda i, c: (c, 0, 0)),
                  const((1, D_MODEL))],
        out_specs=pl.BlockSpec((tq, D_MODEL), lambda i, c: (prev(i), 0)),
        out_shape=jax.ShapeDtypeStruct((N, D_MODEL), F32),
        scratch_shapes=[pltpu.VMEM((tq * G_PITCH, N_KEYS), jnp.int32),
                        pltpu.VMEM((2, 3, nsel, tq), F32),
                        pltpu.VMEM((tq, nsel), jnp.int32),
                        pltpu.VMEM((tq, nsel), F32),
                        pltpu.VMEM((tq // LANES, nsel, LANES), jnp.int32)],
        compiler_params=_params(("arbitrary", "arbitrary")),
        name="peer",
    )(x1, xn, qp, keys_bf, ids, u_blk, v_blk, gf.reshape(1, D_MODEL))


def _encoder(x, w):
    B, S, _ = x.shape
    f_in, qkv1, qkv4, qkv16 = _proj(x, w["norm1_g"], w["w_in"])
    f_out = _fourier(f_in, w["w_fourier"])
    a_out = _attention([qkv1, qkv4, qkv16], S)
    N = B * S
    flat = lambda a: a.reshape(N, a.shape[-1])
    x1, xn, qp = _mix(flat(x), flat(f_out), flat(a_out), w["w_out"], w["norm2_g"], w["w_query"])
    y = _peer(x1, xn, qp, w["sub_keys"], w["expert_u"], w["expert_v"], w["final_g"])
    return y.reshape(B, S, D_MODEL)


def kernel(x_prompt, x_sample, norm1_g, w_in, w_fourier, w_out, norm2_g, w_query, sub_keys, expert_u, expert_v, final_g):
    u_blk, v_blk = _peer_tables(expert_u[0], expert_v[0])
    w = {
        "norm1_g": norm1_g[0], "w_in": w_in[0].astype(BF16), "w_fourier": w_fourier[0],
        "w_out": w_out[0].astype(BF16), "norm2_g": norm2_g[0], "w_query": w_query[0].astype(BF16),
        "sub_keys": sub_keys[0].reshape(2 * PEER_HEADS, N_KEYS, N_KEYS).astype(BF16),
        "expert_u": u_blk, "expert_v": v_blk, "final_g": final_g,
    }
    return (_encoder(x_prompt, w), _encoder(x_sample, w))
```

```python
import functools

import numpy as np
import jax
import jax.numpy as jnp
from jax import lax
from jax.experimental import pallas as pl
from jax.experimental.pallas import tpu as pltpu

F32 = jnp.float32
BF16 = jnp.bfloat16

D_MODEL = 1024
HEAD_DIM = 64
FOURIER_WIDTH = 512
ATTN_WIDTH = 512
IN_WIDTH = 2048
N_GROUPS = 8
DILATIONS = (1, 4, 16)
HALF_WINDOW = 64
ROPE_THETA = 500000.0
ROPE_DIM = 16
PEER_HEADS = 8
N_KEYS = 128
N_EXPERTS = N_KEYS * N_KEYS
TOPK = 16
EPS = 1e-6
NEG = -1e30

LANES = 128
DFT_INNER = 128
G_PITCH = 72
ID_NONE = 1e9
VMEM_LIMIT = 56 * 2**20

PROJ_TILE = 512
DFT_COL_TILE = 4096
DFT_ROW_TILE = 8
ATTN_QUERY_BLOCK = 128
ATTN_TILE = 2048
ATTN_MERGE_CHUNK = 256
MIX_TILE = 512
PEER_TILE = 512
PEER_NH = 8


def _params(sem, vmem=None):
    return pltpu.CompilerParams(dimension_semantics=sem, vmem_limit_bytes=vmem or VMEM_LIMIT)


def _dot(a, b):
    return jnp.dot(a, b, preferred_element_type=F32)


def _dot_nt(a, b):
    return lax.dot_general(a, b, (((1,), (1,)), ((), ())), preferred_element_type=F32)


def _proj_body(tm, x_ref, g_ref, w_ref, c_ref, sa_ref, sb_ref,
               f_ref, q_ref, k_ref, v_ref, q4_ref, k4_ref, v4_ref, q16_ref, k16_ref, v16_ref, stage_ref, hop_ref):
    x = x_ref[...]
    ms = jnp.mean(x * x, axis=-1, keepdims=True)
    h = (x * lax.rsqrt(ms + EPS) * g_ref[...]).astype(BF16)
    p = _dot(h, w_ref[...])
    f_ref[...] = p[:, :FOURIER_WIDTH].astype(BF16)
    c, sa, sb = c_ref[...], sa_ref[...], sb_ref[...]
    for j in range(ATTN_WIDTH // LANES):
        lo = j * LANES
        qc = p[:, FOURIER_WIDTH + lo:FOURIER_WIDTH + lo + LANES]
        kc = p[:, FOURIER_WIDTH + ATTN_WIDTH + lo:FOURIER_WIDTH + ATTN_WIDTH + lo + LANES]
        vc = p[:, FOURIER_WIDTH + 2 * ATTN_WIDTH + lo:FOURIER_WIDTH + 2 * ATTN_WIDTH + lo + LANES]
        qr = (qc * c + pltpu.roll(qc, LANES - 8, 1) * sa + pltpu.roll(qc, 8, 1) * sb) * (HEAD_DIM ** -0.5)
        kr = kc * c + pltpu.roll(kc, LANES - 8, 1) * sa + pltpu.roll(kc, 8, 1) * sb
        for a, (val, nat) in enumerate(((qr, q_ref), (kr, k_ref), (vc, v_ref))):
            nat[:, lo:lo + LANES] = val.astype(BF16)
            stage_ref[a, j] = val
    _, R1, R2 = DILATIONS
    step = R2 // R1
    for a, (d1_ref, d2_ref) in enumerate(((q4_ref, q16_ref), (k4_ref, k16_ref), (v4_ref, v16_ref))):
        for j in range(ATTN_WIDTH // LANES):
            for r in range(R1):
                piece = stage_ref[a, j, pl.ds(r, tm // R1, stride=R1), :]
                d1_ref[:, (j * R1 + r) * LANES:(j * R1 + r + 1) * LANES] = piece.astype(BF16)
                hop_ref[r] = piece
            for r2 in range(R2):
                piece = hop_ref[r2 % R1, pl.ds(r2 // R1, tm // R2, stride=step), :]
                d2_ref[:, (j * R2 + r2) * LANES:(j * R2 + r2 + 1) * LANES] = piece.astype(BF16)


def _rope_tables(S):
    half = ROPE_DIM // 2
    inv = ROPE_THETA ** (-(jnp.arange(half, dtype=F32) * 2.0) / ROPE_DIM)
    ang = jnp.arange(S, dtype=F32)[:, None] * inv[None, :]
    cos, sin = jnp.cos(ang), jnp.sin(ang)
    one = jnp.ones((S, HEAD_DIM - ROPE_DIM), F32)
    zero = jnp.zeros((S, HEAD_DIM - ROPE_DIM), F32)
    z8 = jnp.zeros((S, half), F32)
    c = jnp.concatenate([cos, cos, one], axis=1)
    sa = jnp.concatenate([-sin, z8, zero], axis=1)
    sb = jnp.concatenate([z8, sin, zero], axis=1)
    rep = LANES // HEAD_DIM
    return jnp.tile(c, (1, rep)), jnp.tile(sa, (1, rep)), jnp.tile(sb, (1, rep))


def _proj(x, g, w_bf):
    B, S, _ = x.shape
    tm = min(PROJ_TILE, S)
    c, sa, sb = _rope_tables(S)
    tab = pl.BlockSpec((tm, LANES), lambda b, i: (i, 0))
    out = pl.BlockSpec((None, tm, ATTN_WIDTH), lambda b, i: (b, i, 0))
    shp = jax.ShapeDtypeStruct((B, S, ATTN_WIDTH), BF16)
    dil_specs, dil_shapes = [], []
    for R in DILATIONS[1:]:
        dil_specs.append(pl.BlockSpec((None, tm // R, R * ATTN_WIDTH), lambda b, i: (b, i, 0)))
        dil_shapes.append(jax.ShapeDtypeStruct((B, S // R, R * ATTN_WIDTH), BF16))
    outs = pl.pallas_call(
        functools.partial(_proj_body, tm),
        grid=(B, S // tm),
        in_specs=[pl.BlockSpec((None, tm, D_MODEL), lambda b, i: (b, i, 0)),
                  pl.BlockSpec((1, D_MODEL), lambda b, i: (0, 0)),
                  pl.BlockSpec((D_MODEL, IN_WIDTH), lambda b, i: (0, 0)),
                  tab, tab, tab],
        out_specs=[out] * 4 + [dil_specs[0]] * 3 + [dil_specs[1]] * 3,
        out_shape=[shp] * 4 + [dil_shapes[0]] * 3 + [dil_shapes[1]] * 3,
        scratch_shapes=[pltpu.VMEM((3, ATTN_WIDTH // LANES, tm, LANES), F32),
                        pltpu.VMEM((DILATIONS[1], tm // DILATIONS[1], LANES), F32)],
        compiler_params=_params(("parallel", "parallel")),
        name="proj",
    )(x, g.reshape(1, D_MODEL), w_bf, c, sa, sb)
    return outs[0], outs[1:4], outs[4:7], outs[7:10]


def _cos_sin(n_rows, n_cols, period):
    i = jnp.arange(n_rows, dtype=jnp.int32)[:, None]
    j = jnp.arange(n_cols, dtype=jnp.int32)[None, :]
    ang = ((i * j) % period).astype(F32) * (2.0 * np.pi / period)
    return jnp.cos(ang), jnp.sin(ang)


def _dft_a_body(n1, d_ref, x_ref, yr_ref, yi_ref):
    y = _dot(d_ref[...], x_ref[...])
    yr_ref[...] = y[:n1].astype(BF16)
    yi_ref[...] = y[n1:].astype(BF16)


def _dft_b_body(ct, scale, yr_ref, yi_ref, tc_ref, ts_ref, d2_ref, c64_ref, s64_ref, wf_ref, o_ref):
    rep = FOURIER_WIDTH // LANES
    xr, xi = [], []
    for cc in range(ct):
        yr = yr_ref[cc].astype(F32)
        yi = yi_ref[cc].astype(F32)
        tc = jnp.concatenate([tc_ref[cc]] * rep, axis=1)
        ts = jnp.concatenate([ts_ref[cc]] * rep, axis=1)
        zr = yr * tc + yi * ts
        zi = yi * tc - yr * ts
        z = jnp.concatenate([zr, zi], axis=0).astype(BF16)
        xx = _dot(d2_ref[...], z)
        xr.append(xx[:DFT_INNER].astype(BF16))
        xi.append(xx[DFT_INNER:].astype(BF16))
    xr, xi = jnp.concatenate(xr, axis=0), jnp.concatenate(xi, axis=0)
    re = (_dot(xr, c64_ref[...]) + _dot(xi, s64_ref[...])) * scale
    y = _dot(re.astype(BF16), wf_ref[...]).astype(BF16)
    for cc in range(ct):
        o_ref[cc] = y[cc * DFT_INNER:(cc + 1) * DFT_INNER]


def _block_diag(m):
    G, n, _ = m.shape
    eye = jnp.eye(G, dtype=m.dtype)
    return (eye[:, None, :, None] * m[:, :, None, :]).reshape(G * n, G * n)


def _fourier(f_in, w_fourier):
    B, S, W = f_in.shape
    n2 = DFT_INNER
    n1 = S // n2
    cols = n2 * W
    tcw = DFT_COL_TILE
    c1, s1 = _cos_sin(n1, n1, n1)
    d1 = jnp.concatenate([c1, -s1], axis=0).astype(BF16)
    xa = f_in.reshape(B, n1, cols)
    blk = pl.BlockSpec((None, n1, tcw), lambda b, j: (b, 0, j))
    yshape = jax.ShapeDtypeStruct((B, n1, cols), BF16)
    yr, yi = pl.pallas_call(
        functools.partial(_dft_a_body, n1),
        grid=(B, cols // tcw),
        in_specs=[pl.BlockSpec((2 * n1, n1), lambda b, j: (0, 0)), blk],
        out_specs=[blk, blk],
        out_shape=[yshape, yshape],
        compiler_params=_params(("parallel", "parallel")),
        name="dft_a",
    )(d1, xa)

    yr = yr.reshape(B, n1, n2, W)
    yi = yi.reshape(B, n1, n2, W)
    twc, tws = _cos_sin(n1, n2, S)
    twc = jnp.broadcast_to(twc[:, :, None], (n1, n2, LANES))
    tws = jnp.broadcast_to(tws[:, :, None], (n1, n2, LANES))
    c2, s2 = _cos_sin(n2, n2, n2)
    d2 = jnp.concatenate([jnp.concatenate([c2, s2], axis=1),
                          jnp.concatenate([-s2, c2], axis=1)], axis=0).astype(BF16)
    cg, sg = _cos_sin(HEAD_DIM, HEAD_DIM, HEAD_DIM)
    c64 = _block_diag(jnp.broadcast_to(cg, (N_GROUPS, HEAD_DIM, HEAD_DIM))).astype(BF16)
    s64 = _block_diag(jnp.broadcast_to(sg, (N_GROUPS, HEAD_DIM, HEAD_DIM))).astype(BF16)
    wf = _block_diag(w_fourier).astype(BF16)
    ct = DFT_ROW_TILE
    scale = float(1.0 / np.sqrt(S * HEAD_DIM))
    yblk = pl.BlockSpec((None, ct, n2, W), lambda b, i: (b, i, 0, 0))
    tblk = pl.BlockSpec((ct, n2, LANES), lambda b, i: (i, 0, 0))
    full = lambda r, c: pl.BlockSpec((r, c), lambda b, i: (0, 0))
    out = pl.pallas_call(
        functools.partial(_dft_b_body, ct, scale),
        grid=(B, n1 // ct),
        in_specs=[yblk, yblk, tblk, tblk, full(2 * n2, 2 * n2), full(W, W), full(W, W), full(W, W)],
        out_specs=yblk,
        out_shape=jax.ShapeDtypeStruct((B, n1, n2, W), BF16),
        compiler_params=_params(("parallel", "parallel")),
        name="dft_b",
    )(yr, yi, twc, tws, d2, c64, s64, wf)
    return out.transpose(0, 2, 1, 3).reshape(B, S, W)


def _band_attend(qf, ks, vs, bias, lo_half):
    n = qf.shape[0]
    q2 = jnp.concatenate([jnp.where(lo_half, qf, 0.0), jnp.where(lo_half, 0.0, qf)], axis=0).astype(BF16)
    s = _dot_nt(q2, ks) + jnp.concatenate([bias, bias], axis=0)
    m = jnp.max(s, axis=1, keepdims=True)
    p = jnp.exp(s - m)
    l = jnp.sum(p, axis=1, keepdims=True)
    o = _dot(p.astype(BF16), vs) * (1.0 / l)
    lse = m + jnp.log(l)
    return jnp.where(lo_half, o[:n], o[n:]), jnp.where(lo_half, lse[:n], lse[n:])


def _attn_bias_terms():
    sub, win = ATTN_QUERY_BLOCK, ATTN_QUERY_BLOCK + 2 * HALF_WINDOW
    lane = lax.broadcasted_iota(jnp.int32, (sub, LANES), 1)
    r = lax.broadcasted_iota(jnp.int32, (sub, win), 0)
    c = lax.broadcasted_iota(jnp.int32, (sub, win), 1)
    band = jnp.where((c >= r) & (c <= r + 2 * HALF_WINDOW), 0.0, NEG)
    return lane < HEAD_DIM, band, lax.broadcasted_iota(jnp.int32, (1, win), 1)


def _attn_body(tb, S, *refs):
    ins, (a_ref, og_ref, lg_ref) = refs[:-3], refs[-3:]
    sub, win = ATTN_QUERY_BLOCK, ATTN_QUERY_BLOCK + 2 * HALF_WINDOW
    lo_half, band, key = _attn_bias_terms()
    i = pl.program_id(1)
    for g, R in enumerate(DILATIONS):
        q_ref, kp_ref, kc_ref, kn_ref, vp_ref, vc_ref, vn_ref = ins[7 * g:7 * g + 7]
        tq, L = tb // R, S // R
        n_sub = tq // sub
        for r in range(R):
            sl = slice(r * LANES, (r + 1) * LANES)
            k2 = jnp.concatenate([kp_ref[:, sl], kc_ref[:, sl], kn_ref[:, sl]], axis=0)
            v2 = jnp.concatenate([vp_ref[:, sl], vc_ref[:, sl], vn_ref[:, sl]], axis=0)
            for sb in range(n_sub):
                bias = band
                if sb == 0 or sb == n_sub - 1:
                    kpos = i * tq + sb * sub - HALF_WINDOW + key
                    bias = band + jnp.where((kpos >= 0) & (kpos < L), 0.0, NEG)
                o, lse = _band_attend(q_ref[sb * sub:(sb + 1) * sub, sl].astype(F32),
                                      k2[sb * sub:sb * sub + win], v2[sb * sub:sb * sub + win], bias, lo_half)
                dst = pl.ds(R * sb * sub + r, sub, stride=R)
                og_ref[g, dst, :] = o
                lg_ref[g, dst, :] = lse
    chunk = min(ATTN_MERGE_CHUNK, tb)
    for t in range(0, tb, chunk):
        rows = slice(t, t + chunk)
        l1, l2, l3 = lg_ref[0, rows, :], lg_ref[1, rows, :], lg_ref[2, rows, :]
        mx = jnp.maximum(jnp.maximum(l1, l2), l3)
        e1, e2, e3 = jnp.exp(l1 - mx), jnp.exp(l2 - mx), jnp.exp(l3 - mx)
        a = (e1 * og_ref[0, rows, :] + e2 * og_ref[1, rows, :] + e3 * og_ref[2, rows, :]) * (1.0 / (e1 + e2 + e3))
        a_ref[rows, :] = a.astype(BF16)


def _halo_index(hb, nhalo, offset):
    return lambda b, i, p: (b, jnp.clip(i * hb + offset, 0, nhalo - 1), p)


def _attention(qkv_groups, S):
    B = qkv_groups[0][0].shape[0]
    tb = min(ATTN_TILE, S)
    specs, args = [], []
    for (q, k, v), R in zip(qkv_groups, DILATIONS):
        tq, L, wb = tb // R, S // R, R * LANES
        hb, nhalo = tq // HALF_WINDOW, L // HALF_WINDOW
        ctr = pl.BlockSpec((None, tq, wb), lambda b, i, p: (b, i, p))
        prev = pl.BlockSpec((None, HALF_WINDOW, wb), _halo_index(hb, nhalo, -1))
        nxt = pl.BlockSpec((None, HALF_WINDOW, wb), _halo_index(hb, nhalo, hb))
        specs += [ctr, prev, ctr, nxt, prev, ctr, nxt]
        args += [q, k, k, k, v, v, v]
    return pl.pallas_call(
        functools.partial(_attn_body, tb, S),
        grid=(B, S // tb, ATTN_WIDTH // LANES),
        in_specs=specs,
        out_specs=pl.BlockSpec((None, tb, LANES), lambda b, i, p: (b, i, p)),
        out_shape=jax.ShapeDtypeStruct((B, S, ATTN_WIDTH), BF16),
        scratch_shapes=[pltpu.VMEM((len(DILATIONS), tb, LANES), F32), pltpu.VMEM((len(DILATIONS), tb, LANES), F32)],
        compiler_params=_params(("parallel", "parallel", "parallel")),
        name="attn",
    )(*args)


def _mix_body(x_ref, f_ref, a_ref, wf_ref, wa_ref, g_ref, wq_ref, x1_ref, xn_ref, qp_ref):
    x1 = x_ref[...] + _dot(f_ref[...], wf_ref[...]) + _dot(a_ref[...], wa_ref[...])
    x1_ref[...] = x1
    ms = jnp.mean(x1 * x1, axis=-1, keepdims=True)
    xn = (x1 * lax.rsqrt(ms + EPS) * g_ref[...]).astype(BF16)
    xn_ref[...] = xn
    qp_ref[...] = _dot(xn, wq_ref[...]).astype(BF16)


def _mix(x, f_out, a_out, w_out_bf, g2, wq_bf):
    N = x.shape[0]
    tm = MIX_TILE
    row = lambda w: pl.BlockSpec((tm, w), lambda i: (i, 0))
    full = lambda r, c: pl.BlockSpec((r, c), lambda i: (0, 0))
    qw = wq_bf.shape[1]
    return pl.pallas_call(
        _mix_body,
        grid=(N // tm,),
        in_specs=[row(D_MODEL), row(FOURIER_WIDTH), row(ATTN_WIDTH),
                  full(FOURIER_WIDTH, D_MODEL), full(ATTN_WIDTH, D_MODEL), full(1, D_MODEL), full(D_MODEL, qw)],
        out_specs=[row(D_MODEL), row(D_MODEL), row(qw)],
        out_shape=[jax.ShapeDtypeStruct((N, D_MODEL), F32), jax.ShapeDtypeStruct((N, D_MODEL), BF16),
                   jax.ShapeDtypeStruct((N, qw), BF16)],
        compiler_params=_params(("parallel",)),
        name="mix",
    )(x, f_out, a_out, w_out_bf[:FOURIER_WIDTH], w_out_bf[FOURIER_WIDTH:], g2.reshape(1, D_MODEL), wq_bf)


def _argmax_rows(parts):
    while len(parts) > 1:
        nxt = []
        for k in range(0, len(parts) - 1, 2):
            (va, ia), (vb, ib) = parts[k], parts[k + 1]
            take_b = vb > va
            nxt.append((jnp.where(take_b, vb, va), jnp.where(take_b, ib, ia)))
        if len(parts) % 2:
            nxt.append(parts[-1])
        parts = nxt
    v, i = parts[0]
    m = jnp.max(v, axis=0, keepdims=True)
    return m, jnp.min(jnp.where(v == m, i, ID_NONE), axis=0, keepdims=True)


def _extract_topk_sorted(s, id_parts, depth):
    tq = s.shape[1]
    blocks = [(s[8 * k:8 * k + 8], ids) for k, ids in enumerate(id_parts)]
    groups = [blocks[g:g + depth] for g in range(0, len(blocks), depth)]
    for col in groups:
        for done in range(depth - 1):
            for j in range(depth - 1 - done):
                (va, ia), (vb, ib) = col[j], col[j + 1]
                swap = vb > va
                col[j] = (jnp.where(swap, vb, va), jnp.where(swap, ib, ia))
                col[j + 1] = (jnp.where(swap, va, vb), jnp.where(swap, ia, ib))
    rank = lax.broadcasted_iota(jnp.int32, (TOPK, tq), 0)
    vals = jnp.zeros((TOPK, tq), F32)
    idxs = jnp.zeros((TOPK, tq), F32)
    for kk in range(TOPK):
        m, first = _argmax_rows([col[0] for col in groups])
        for col in groups:
            won = col[0][1] == first
            for d in range(depth - 1):
                col[d] = (jnp.where(won, col[d + 1][0], col[d][0]), jnp.where(won, col[d + 1][1], col[d][1]))
            col[depth - 1] = (jnp.where(won, -jnp.inf, col[depth - 1][0]), col[depth - 1][1])
        vals = jnp.where(rank == kk, m, vals)
        idxs = jnp.where(rank == kk, first, idxs)
    return vals, idxs


def _select_rows(sel, table):
    out = jnp.zeros_like(sel)
    for a in range(TOPK):
        out = jnp.where(sel == float(a), table[a:a + 1, :], out)
    return out


def _candidate_ids():
    ids = list(range(TOPK))
    for a in range(1, 8):
        ids += [a * TOPK + b if (a + 1) * (b + 1) <= TOPK else ID_NONE for b in range(8)]
    ids += [a * TOPK for a in range(8, TOPK)]
    return np.asarray(ids, np.float32)


def _topk_head(q0, q1, k0, k1, key_ids, cand_ids, cand_id_parts):
    (v1, x1), (v2, x2) = [_extract_topk_sorted(_dot_nt(k, q), key_ids, 4) for k, q in ((k0, q0), (k1, q1))]
    cand = jnp.concatenate([v1[0:1, :] + v2] + [v1[a:a + 1, :] + v2[0:8, :] for a in range(1, 8)]
                           + [v1[8:TOPK, :] + v2[0:1, :]], axis=0)
    cand = jnp.where(cand_ids < ID_NONE, cand, -jnp.inf)
    top_s, flat = _extract_topk_sorted(cand, cand_id_parts, 5)
    a_sel = jnp.floor(flat * (1.0 / TOPK))
    b_sel = flat - a_sel * TOPK
    e = jnp.exp(top_s - top_s[0:1, :])
    return _select_rows(a_sel, x1), _select_rows(b_sel, x2), e * (1.0 / jnp.sum(e, axis=0, keepdims=True))


def _pack_bf16_pair(a, b):
    abits = lax.bitcast_convert_type(a.astype(BF16).astype(F32), jnp.uint32)
    bbits = lax.bitcast_convert_type(b.astype(BF16).astype(F32), jnp.uint32)
    return lax.bitcast_convert_type(abits | (bbits >> 16), jnp.int32)


def _unpack_bf16_pair(word):
    bits = lax.bitcast_convert_type(word, jnp.uint32)
    hi = lax.bitcast_convert_type(bits & jnp.uint32(0xFFFF0000), F32)
    lo = lax.bitcast_convert_type(bits << 16, F32)
    return hi, lo

def _peer_body(tq, nh, x1_ref, xn_ref, qp_ref, keys_ref, ids_ref, ul_ref, uh_ref, vl_ref, vh_ref, gf_ref,
               o_ref, gate_ref, sel_ref, i1_ref, g_ref, i2t_ref):
    i = pl.program_id(0)
    c = pl.program_id(1)
    half_keys = N_KEYS // 2
    slot = lax.rem(i, 2)

    @pl.when(jnp.logical_and(c == 0, i == 0))
    def _():
        gate_ref[...] = jnp.zeros_like(gate_ref)

    @pl.when(jnp.logical_and(c == 0, i > 0))
    def _():
        for grp in range(tq // LANES):
            cols = slice(grp * LANES, (grp + 1) * LANES)
            i1_ref[cols, :] = sel_ref[1 - slot, 0, :, cols].T.astype(jnp.int32)
            i2t_ref[grp] = sel_ref[1 - slot, 1, :, cols].astype(jnp.int32)
            g_ref[cols, :] = sel_ref[1 - slot, 2, :, cols].T
        row = lax.broadcasted_iota(jnp.int32, (N_KEYS, N_KEYS), 0)
        lane = lax.broadcasted_iota(jnp.int32, (N_KEYS, N_KEYS), 1)

        def build(grp, carry):
            base = pl.multiple_of(grp * LANES, LANES)
            i1g, gg = i1_ref[pl.ds(base, LANES), :], g_ref[pl.ds(base, LANES), :]
            i2t = i2t_ref[grp]
            for r in range(LANES):
                p1 = jnp.where(i1g[r:r + 1, :] == row, gg[r:r + 1, :], 0.0).astype(BF16)
                p2 = jnp.where(i2t[:, r:r + 1] == lane, 1.0, 0.0).astype(BF16)
                gt = _dot(p1, p2)
                word = _pack_bf16_pair(gt[:half_keys], gt[half_keys:])
                gate_ref[pl.ds(pl.multiple_of((base + r) * G_PITCH, 8), half_keys), :] = word
            return carry

        lax.fori_loop(0, tq // LANES, build, 0)

    @pl.when(c == 0)
    def _():
        o_ref[...] = jnp.zeros_like(o_ref)

    sub = lax.broadcasted_iota(jnp.int32, (8, tq), 0).astype(F32)
    key_ids = [sub + float(r) for r in range(0, N_KEYS, 8)]
    cand_ids = ids_ref[...]
    cand_id_parts = [cand_ids[r:r + 8] for r in range(0, cand_ids.shape[0], 8)]
    sel = _topk_head(qp_ref[:, :N_KEYS], qp_ref[:, N_KEYS:], keys_ref[2 * c], keys_ref[2 * c + 1],
                     key_ids, cand_ids, cand_id_parts)
    rows = pl.ds(pl.multiple_of(c * TOPK, TOPK), TOPK)
    for k, val in enumerate(sel):
        sel_ref[slot, k, rows, :] = val

    xn = xn_ref[...]
    gates = [_unpack_bf16_pair(gate_ref[pl.ds(c * nh + ii, tq, stride=G_PITCH), :])
             for ii in range(nh)]
    contrib = []
    for half, (u_ref, v_ref) in enumerate(((ul_ref, vl_ref), (uh_ref, vh_ref))):
        a = _dot(xn, u_ref[...])
        act = 0.5 * a * (1.0 + lax.erf(a * float(np.sqrt(0.5))))
        w = jnp.concatenate([(act[:, ii * N_KEYS:(ii + 1) * N_KEYS] * gates[ii][half]).astype(BF16)
                             for ii in range(nh)], axis=1)
        contrib.append(_dot(w, v_ref[...]))
    o_ref[...] += contrib[0] + contrib[1]

    @pl.when(c == pl.num_programs(1) - 1)
    def _():
        y = x1_ref[...] + o_ref[...]
        ms = jnp.mean(y * y, axis=-1, keepdims=True)
        o_ref[...] = y * lax.rsqrt(ms + EPS) * gf_ref[...]


def _peer_u_blocks(u):
    ec = PEER_NH * N_KEYS
    return u.astype(BF16).reshape(N_EXPERTS // ec, ec, D_MODEL).transpose(0, 2, 1)


def _peer(x1, xn, qp, keys_bf, u_blk, v_bf, gf):
    N = x1.shape[0]
    tq = PEER_TILE
    nh = PEER_NH
    ec = nh * N_KEYS
    n_chunks = (N_KEYS // 2) // nh
    assert n_chunks == PEER_HEADS
    nt = N // tq
    nsel = PEER_HEADS * TOPK
    ids = _candidate_ids()
    ids = jnp.asarray(np.broadcast_to(ids[:, None], (ids.shape[0], tq)))
    prev = lambda i: jnp.maximum(i - 1, 0)
    tok = lambda w: pl.BlockSpec((tq, w), lambda i, c: (prev(i), 0), pipeline_mode=pl.Buffered(1))
    const = lambda shape: pl.BlockSpec(shape, lambda i, c: (0,) * len(shape))
    return pl.pallas_call(
        functools.partial(_peer_body, tq, nh),
        grid=(nt + 1, n_chunks),
        in_specs=[tok(D_MODEL), tok(D_MODEL),
                  pl.BlockSpec((tq, 2 * N_KEYS), lambda i, c: (jnp.minimum(i, nt - 1), c)),
                  const((2 * PEER_HEADS, N_KEYS, N_KEYS)), const(ids.shape),
                  pl.BlockSpec((None, D_MODEL, ec), lambda i, c: (c, 0, 0)),
                  pl.BlockSpec((None, D_MODEL, ec), lambda i, c: (c + n_chunks, 0, 0)),
                  pl.BlockSpec((ec, D_MODEL), lambda i, c: (c, 0)),
                  pl.BlockSpec((ec, D_MODEL), lambda i, c: (c + n_chunks, 0)),
                  const((1, D_MODEL))],
        out_specs=pl.BlockSpec((tq, D_MODEL), lambda i, c: (prev(i), 0)),
        out_shape=jax.ShapeDtypeStruct((N, D_MODEL), F32),
        scratch_shapes=[pltpu.VMEM((tq * G_PITCH, N_KEYS), jnp.int32),
                        pltpu.VMEM((2, 3, nsel, tq), F32),
                        pltpu.VMEM((tq, nsel), jnp.int32),
                        pltpu.VMEM((tq, nsel), F32),
                        pltpu.VMEM((tq // LANES, nsel, LANES), jnp.int32)],
        compiler_params=_params(("arbitrary", "arbitrary")),
        name="peer",
    )(x1, xn, qp, keys_bf, ids, u_blk, u_blk, v_bf, v_bf, gf.reshape(1, D_MODEL))


def _encoder(x, w):
    B, S, _ = x.shape
    f_in, qkv1, qkv4, qkv16 = _proj(x, w["norm1_g"], w["w_in"])
    f_out = _fourier(f_in, w["w_fourier"])
    a_out = _attention([qkv1, qkv4, qkv16], S)
    N = B * S
    flat = lambda a: a.reshape(N, a.shape[-1])
    x1, xn, qp = _mix(flat(x), flat(f_out), flat(a_out), w["w_out"], w["norm2_g"], w["w_query"])
    y = _peer(x1, xn, qp, w["sub_keys"], w["expert_u"], w["expert_v"], w["final_g"])
    return y.reshape(B, S, D_MODEL)


def kernel(x_prompt, x_sample, norm1_g, w_in, w_fourier, w_out, norm2_g, w_query, sub_keys, expert_u, expert_v, final_g):
    w = {
        "norm1_g": norm1_g[0], "w_in": w_in[0].astype(BF16), "w_fourier": w_fourier[0],
        "w_out": w_out[0].astype(BF16), "norm2_g": norm2_g[0], "w_query": w_query[0].astype(BF16),
        "sub_keys": sub_keys[0].reshape(2 * PEER_HEADS, N_KEYS, N_KEYS).astype(BF16),
        "expert_u": _peer_u_blocks(expert_u[0]), "expert_v": expert_v[0].astype(BF16), "final_g": final_g,
    }
    return (_encoder(x_prompt, w), _encoder(x_sample, w))
```

```python
import functools

import numpy as np
import jax
import jax.numpy as jnp
from jax import lax
from jax.experimental import pallas as pl
from jax.experimental.pallas import tpu as pltpu

F32 = jnp.float32
BF16 = jnp.bfloat16

D_MODEL = 1024
HEAD_DIM = 64
FOURIER_WIDTH = 512
ATTN_WIDTH = 512
IN_WIDTH = 2048
N_GROUPS = 8
DILATIONS = (1, 4, 16)
HALF_WINDOW = 64
ROPE_THETA = 500000.0
ROPE_DIM = 16
PEER_HEADS = 8
N_KEYS = 128
N_EXPERTS = N_KEYS * N_KEYS
TOPK = 16
EPS = 1e-6
NEG = -1e30

LANES = 128
DFT_INNER = 128
G_PITCH = 72
ID_NONE = 1e9
VMEM_LIMIT = 56 * 2**20

PROJ_TILE = 512
DFT_COL_TILE = 4096
DFT_ROW_TILE = 8
ATTN_QUERY_BLOCK = 128
ATTN_TILE = 2048
ATTN_MERGE_CHUNK = 256
MIX_TILE = 512
PEER_TILE = 512
PEER_NH = 8


def _params(sem, vmem=None):
    return pltpu.CompilerParams(dimension_semantics=sem, vmem_limit_bytes=vmem or VMEM_LIMIT)


def _dot(a, b):
    return jnp.dot(a, b, preferred_element_type=F32)


def _dot_nt(a, b):
    return lax.dot_general(a, b, (((1,), (1,)), ((), ())), preferred_element_type=F32)


def _proj_body(tm, x_ref, g_ref, w_ref, c_ref, sa_ref, sb_ref,
               f_ref, q_ref, k_ref, v_ref, q4_ref, k4_ref, v4_ref, q16_ref, k16_ref, v16_ref, stage_ref, hop_ref):
    x = x_ref[...]
    ms = jnp.mean(x * x, axis=-1, keepdims=True)
    h = (x * lax.rsqrt(ms + EPS) * g_ref[...]).astype(BF16)
    p = _dot(h, w_ref[...])
    f_ref[...] = p[:, :FOURIER_WIDTH].astype(BF16)
    c, sa, sb = c_ref[...], sa_ref[...], sb_ref[...]
    for j in range(ATTN_WIDTH // LANES):
        lo = j * LANES
        qc = p[:, FOURIER_WIDTH + lo:FOURIER_WIDTH + lo + LANES]
        kc = p[:, FOURIER_WIDTH + ATTN_WIDTH + lo:FOURIER_WIDTH + ATTN_WIDTH + lo + LANES]
        vc = p[:, FOURIER_WIDTH + 2 * ATTN_WIDTH + lo:FOURIER_WIDTH + 2 * ATTN_WIDTH + lo + LANES]
        qr = (qc * c + pltpu.roll(qc, LANES - 8, 1) * sa + pltpu.roll(qc, 8, 1) * sb) * (HEAD_DIM ** -0.5)
        kr = kc * c + pltpu.roll(kc, LANES - 8, 1) * sa + pltpu.roll(kc, 8, 1) * sb
        for a, (val, nat) in enumerate(((qr, q_ref), (kr, k_ref), (vc, v_ref))):
            nat[:, lo:lo + LANES] = val.astype(BF16)
            stage_ref[a, j] = val
    _, R1, R2 = DILATIONS
    step = R2 // R1
    for a, (d1_ref, d2_ref) in enumerate(((q4_ref, q16_ref), (k4_ref, k16_ref), (v4_ref, v16_ref))):
        for j in range(ATTN_WIDTH // LANES):
            for r in range(R1):
                piece = stage_ref[a, j, pl.ds(r, tm // R1, stride=R1), :]
                d1_ref[:, (j * R1 + r) * LANES:(j * R1 + r + 1) * LANES] = piece.astype(BF16)
                hop_ref[r] = piece
            for r2 in range(R2):
                piece = hop_ref[r2 % R1, pl.ds(r2 // R1, tm // R2, stride=step), :]
                d2_ref[:, (j * R2 + r2) * LANES:(j * R2 + r2 + 1) * LANES] = piece.astype(BF16)


def _rope_tables(S):
    half = ROPE_DIM // 2
    inv = ROPE_THETA ** (-(jnp.arange(half, dtype=F32) * 2.0) / ROPE_DIM)
    ang = jnp.arange(S, dtype=F32)[:, None] * inv[None, :]
    cos, sin = jnp.cos(ang), jnp.sin(ang)
    one = jnp.ones((S, HEAD_DIM - ROPE_DIM), F32)
    zero = jnp.zeros((S, HEAD_DIM - ROPE_DIM), F32)
    z8 = jnp.zeros((S, half), F32)
    c = jnp.concatenate([cos, cos, one], axis=1)
    sa = jnp.concatenate([-sin, z8, zero], axis=1)
    sb = jnp.concatenate([z8, sin, zero], axis=1)
    rep = LANES // HEAD_DIM
    return jnp.tile(c, (1, rep)), jnp.tile(sa, (1, rep)), jnp.tile(sb, (1, rep))


def _proj(x, g, w_bf):
    B, S, _ = x.shape
    tm = min(PROJ_TILE, S)
    c, sa, sb = _rope_tables(S)
    tab = pl.BlockSpec((tm, LANES), lambda b, i: (i, 0))
    out = pl.BlockSpec((None, tm, ATTN_WIDTH), lambda b, i: (b, i, 0))
    shp = jax.ShapeDtypeStruct((B, S, ATTN_WIDTH), BF16)
    dil_specs, dil_shapes = [], []
    for R in DILATIONS[1:]:
        dil_specs.append(pl.BlockSpec((None, tm // R, R * ATTN_WIDTH), lambda b, i: (b, i, 0)))
        dil_shapes.append(jax.ShapeDtypeStruct((B, S // R, R * ATTN_WIDTH), BF16))
    outs = pl.pallas_call(
        functools.partial(_proj_body, tm),
        grid=(B, S // tm),
        in_specs=[pl.BlockSpec((None, tm, D_MODEL), lambda b, i: (b, i, 0)),
                  pl.BlockSpec((1, D_MODEL), lambda b, i: (0, 0)),
                  pl.BlockSpec((D_MODEL, IN_WIDTH), lambda b, i: (0, 0)),
                  tab, tab, tab],
        out_specs=[out] * 4 + [dil_specs[0]] * 3 + [dil_specs[1]] * 3,
        out_shape=[shp] * 4 + [dil_shapes[0]] * 3 + [dil_shapes[1]] * 3,
        scratch_shapes=[pltpu.VMEM((3, ATTN_WIDTH // LANES, tm, LANES), F32),
                        pltpu.VMEM((DILATIONS[1], tm // DILATIONS[1], LANES), F32)],
        compiler_params=_params(("parallel", "parallel")),
        name="proj",
    )(x, g.reshape(1, D_MODEL), w_bf, c, sa, sb)
    return outs[0], outs[1:4], outs[4:7], outs[7:10]


def _cos_sin(n_rows, n_cols, period):
    i = jnp.arange(n_rows, dtype=jnp.int32)[:, None]
    j = jnp.arange(n_cols, dtype=jnp.int32)[None, :]
    ang = ((i * j) % period).astype(F32) * (2.0 * np.pi / period)
    return jnp.cos(ang), jnp.sin(ang)


def _dft_a_body(n1, d_ref, x_ref, yr_ref, yi_ref):
    y = _dot(d_ref[...], x_ref[...])
    yr_ref[...] = y[:n1].astype(BF16)
    yi_ref[...] = y[n1:].astype(BF16)


def _dft_b_body(ct, scale, yr_ref, yi_ref, tc_ref, ts_ref, d2_ref, c64_ref, s64_ref, wf_ref, o_ref):
    rep = FOURIER_WIDTH // LANES
    xr, xi = [], []
    for cc in range(ct):
        yr = yr_ref[cc].astype(F32)
        yi = yi_ref[cc].astype(F32)
        tc = jnp.concatenate([tc_ref[cc]] * rep, axis=1)
        ts = jnp.concatenate([ts_ref[cc]] * rep, axis=1)
        zr = yr * tc + yi * ts
        zi = yi * tc - yr * ts
        z = jnp.concatenate([zr, zi], axis=0).astype(BF16)
        xx = _dot(d2_ref[...], z)
        xr.append(xx[:DFT_INNER].astype(BF16))
        xi.append(xx[DFT_INNER:].astype(BF16))
    xr, xi = jnp.concatenate(xr, axis=0), jnp.concatenate(xi, axis=0)
    re = (_dot(xr, c64_ref[...]) + _dot(xi, s64_ref[...])) * scale
    y = _dot(re.astype(BF16), wf_ref[...]).astype(BF16)
    for cc in range(ct):
        o_ref[cc] = y[cc * DFT_INNER:(cc + 1) * DFT_INNER]


def _block_diag(m):
    G, n, _ = m.shape
    eye = jnp.eye(G, dtype=m.dtype)
    return (eye[:, None, :, None] * m[:, :, None, :]).reshape(G * n, G * n)


def _fourier(f_in, w_fourier):
    B, S, W = f_in.shape
    n2 = DFT_INNER
    n1 = S // n2
    cols = n2 * W
    tcw = DFT_COL_TILE
    c1, s1 = _cos_sin(n1, n1, n1)
    d1 = jnp.concatenate([c1, -s1], axis=0).astype(BF16)
    xa = f_in.reshape(B, n1, cols)
    blk = pl.BlockSpec((None, n1, tcw), lambda b, j: (b, 0, j))
    yshape = jax.ShapeDtypeStruct((B, n1, cols), BF16)
    yr, yi = pl.pallas_call(
        functools.partial(_dft_a_body, n1),
        grid=(B, cols // tcw),
        in_specs=[pl.BlockSpec((2 * n1, n1), lambda b, j: (0, 0)), blk],
        out_specs=[blk, blk],
        out_shape=[yshape, yshape],
        compiler_params=_params(("parallel", "parallel")),
        name="dft_a",
    )(d1, xa)

    yr = yr.reshape(B, n1, n2, W)
    yi = yi.reshape(B, n1, n2, W)
    twc, tws = _cos_sin(n1, n2, S)
    twc = jnp.broadcast_to(twc[:, :, None], (n1, n2, LANES))
    tws = jnp.broadcast_to(tws[:, :, None], (n1, n2, LANES))
    c2, s2 = _cos_sin(n2, n2, n2)
    d2 = jnp.concatenate([jnp.concatenate([c2, s2], axis=1),
                          jnp.concatenate([-s2, c2], axis=1)], axis=0).astype(BF16)
    cg, sg = _cos_sin(HEAD_DIM, HEAD_DIM, HEAD_DIM)
    c64 = _block_diag(jnp.broadcast_to(cg, (N_GROUPS, HEAD_DIM, HEAD_DIM))).astype(BF16)
    s64 = _block_diag(jnp.broadcast_to(sg, (N_GROUPS, HEAD_DIM, HEAD_DIM))).astype(BF16)
    wf = _block_diag(w_fourier).astype(BF16)
    ct = DFT_ROW_TILE
    scale = float(1.0 / np.sqrt(S * HEAD_DIM))
    yblk = pl.BlockSpec((None, ct, n2, W), lambda b, i: (b, i, 0, 0))
    tblk = pl.BlockSpec((ct, n2, LANES), lambda b, i: (i, 0, 0))
    full = lambda r, c: pl.BlockSpec((r, c), lambda b, i: (0, 0))
    out = pl.pallas_call(
        functools.partial(_dft_b_body, ct, scale),
        grid=(B, n1 // ct),
        in_specs=[yblk, yblk, tblk, tblk, full(2 * n2, 2 * n2), full(W, W), full(W, W), full(W, W)],
        out_specs=yblk,
        out_shape=jax.ShapeDtypeStruct((B, n1, n2, W), BF16),
        compiler_params=_params(("parallel", "parallel")),
        name="dft_b",
    )(yr, yi, twc, tws, d2, c64, s64, wf)
    return out.transpose(0, 2, 1, 3).reshape(B, S, W)


def _band_attend(qf, ks, vs, bias, lo_half):
    n = qf.shape[0]
    q2 = jnp.concatenate([jnp.where(lo_half, qf, 0.0), jnp.where(lo_half, 0.0, qf)], axis=0).astype(BF16)
    s = _dot_nt(q2, ks) + jnp.concatenate([bias, bias], axis=0)
    m = jnp.max(s, axis=1, keepdims=True)
    p = jnp.exp(s - m)
    l = jnp.sum(p, axis=1, keepdims=True)
    o = _dot(p.astype(BF16), vs) * (1.0 / l)
    lse = m + jnp.log(l)
    return jnp.where(lo_half, o[:n], o[n:]), jnp.where(lo_half, lse[:n], lse[n:])


def _attn_bias_terms():
    sub, win = ATTN_QUERY_BLOCK, ATTN_QUERY_BLOCK + 2 * HALF_WINDOW
    lane = lax.broadcasted_iota(jnp.int32, (sub, LANES), 1)
    r = lax.broadcasted_iota(jnp.int32, (sub, win), 0)
    c = lax.broadcasted_iota(jnp.int32, (sub, win), 1)
    band = jnp.where((c >= r) & (c <= r + 2 * HALF_WINDOW), 0.0, NEG)
    return lane < HEAD_DIM, band, lax.broadcasted_iota(jnp.int32, (1, win), 1)


def _attn_body(tb, S, *refs):
    ins, (a_ref, og_ref, lg_ref) = refs[:-3], refs[-3:]
    sub, win = ATTN_QUERY_BLOCK, ATTN_QUERY_BLOCK + 2 * HALF_WINDOW
    lo_half, band, key = _attn_bias_terms()
    i = pl.program_id(1)
    for g, R in enumerate(DILATIONS):
        q_ref, kp_ref, kc_ref, kn_ref, vp_ref, vc_ref, vn_ref = ins[7 * g:7 * g + 7]
        tq, L = tb // R, S // R
        n_sub = tq // sub
        for r in range(R):
            sl = slice(r * LANES, (r + 1) * LANES)
            k2 = jnp.concatenate([kp_ref[:, sl], kc_ref[:, sl], kn_ref[:, sl]], axis=0)
            v2 = jnp.concatenate([vp_ref[:, sl], vc_ref[:, sl], vn_ref[:, sl]], axis=0)
            for sb in range(n_sub):
                bias = band
                if sb == 0 or sb == n_sub - 1:
                    kpos = i * tq + sb * sub - HALF_WINDOW + key
                    bias = band + jnp.where((kpos >= 0) & (kpos < L), 0.0, NEG)
                o, lse = _band_attend(q_ref[sb * sub:(sb + 1) * sub, sl].astype(F32),
                                      k2[sb * sub:sb * sub + win], v2[sb * sub:sb * sub + win], bias, lo_half)
                dst = pl.ds(R * sb * sub + r, sub, stride=R)
                og_ref[g, dst, :] = o
                lg_ref[g, dst, :] = lse
    chunk = min(ATTN_MERGE_CHUNK, tb)
    for t in range(0, tb, chunk):
        rows = slice(t, t + chunk)
        l1, l2, l3 = lg_ref[0, rows, :], lg_ref[1, rows, :], lg_ref[2, rows, :]
        mx = jnp.maximum(jnp.maximum(l1, l2), l3)
        e1, e2, e3 = jnp.exp(l1 - mx), jnp.exp(l2 - mx), jnp.exp(l3 - mx)
        a = (e1 * og_ref[0, rows, :] + e2 * og_ref[1, rows, :] + e3 * og_ref[2, rows, :]) * (1.0 / (e1 + e2 + e3))
        a_ref[rows, :] = a.astype(BF16)


def _halo_index(hb, nhalo, offset):
    return lambda b, i, p: (b, jnp.clip(i * hb + offset, 0, nhalo - 1), p)


def _attention(qkv_groups, S):
    B = qkv_groups[0][0].shape[0]
    tb = min(ATTN_TILE, S)
    specs, args = [], []
    for (q, k, v), R in zip(qkv_groups, DILATIONS):
        tq, L, wb = tb // R, S // R, R * LANES
        hb, nhalo = tq // HALF_WINDOW, L // HALF_WINDOW
        ctr = pl.BlockSpec((None, tq, wb), lambda b, i, p: (b, i, p))
        prev = pl.BlockSpec((None, HALF_WINDOW, wb), _halo_index(hb, nhalo, -1))
        nxt = pl.BlockSpec((None, HALF_WINDOW, wb), _halo_index(hb, nhalo, hb))
        specs += [ctr, prev, ctr, nxt, prev, ctr, nxt]
        args += [q, k, k, k, v, v, v]
    return pl.pallas_call(
        functools.partial(_attn_body, tb, S),
        grid=(B, S // tb, ATTN_WIDTH // LANES),
        in_specs=specs,
        out_specs=pl.BlockSpec((None, tb, LANES), lambda b, i, p: (b, i, p)),
        out_shape=jax.ShapeDtypeStruct((B, S, ATTN_WIDTH), BF16),
        scratch_shapes=[pltpu.VMEM((len(DILATIONS), tb, LANES), F32), pltpu.VMEM((len(DILATIONS), tb, LANES), F32)],
        compiler_params=_params(("parallel", "parallel", "parallel")),
        name="attn",
    )(*args)


def _mix_body(x_ref, f_ref, a_ref, wf_ref, wa_ref, g_ref, wq_ref, x1_ref, xn_ref, qp_ref):
    x1 = x_ref[...] + _dot(f_ref[...], wf_ref[...]) + _dot(a_ref[...], wa_ref[...])
    x1_ref[...] = x1
    ms = jnp.mean(x1 * x1, axis=-1, keepdims=True)
    xn = (x1 * lax.rsqrt(ms + EPS) * g_ref[...]).astype(BF16)
    xn_ref[...] = xn
    qp_ref[...] = _dot(xn, wq_ref[...]).astype(BF16)


def _mix(x, f_out, a_out, w_out_bf, g2, wq_bf):
    N = x.shape[0]
    tm = MIX_TILE
    row = lambda w: pl.BlockSpec((tm, w), lambda i: (i, 0))
    full = lambda r, c: pl.BlockSpec((r, c), lambda i: (0, 0))
    qw = wq_bf.shape[1]
    return pl.pallas_call(
        _mix_body,
        grid=(N // tm,),
        in_specs=[row(D_MODEL), row(FOURIER_WIDTH), row(ATTN_WIDTH),
                  full(FOURIER_WIDTH, D_MODEL), full(ATTN_WIDTH, D_MODEL), full(1, D_MODEL), full(D_MODEL, qw)],
        out_specs=[row(D_MODEL), row(D_MODEL), row(qw)],
        out_shape=[jax.ShapeDtypeStruct((N, D_MODEL), F32), jax.ShapeDtypeStruct((N, D_MODEL), BF16),
                   jax.ShapeDtypeStruct((N, qw), BF16)],
        compiler_params=_params(("parallel",)),
        name="mix",
    )(x, f_out, a_out, w_out_bf[:FOURIER_WIDTH], w_out_bf[FOURIER_WIDTH:], g2.reshape(1, D_MODEL), wq_bf)


def _argmax_rows(parts):
    while len(parts) > 1:
        nxt = []
        for k in range(0, len(parts) - 1, 2):
            (va, ia), (vb, ib) = parts[k], parts[k + 1]
            take_b = vb > va
            nxt.append((jnp.where(take_b, vb, va), jnp.where(take_b, ib, ia)))
        if len(parts) % 2:
            nxt.append(parts[-1])
        parts = nxt
    v, i = parts[0]
    m = jnp.max(v, axis=0, keepdims=True)
    return m, jnp.min(jnp.where(v == m, i, ID_NONE), axis=0, keepdims=True)


def _extract_topk_sorted(s, id_parts, depth):
    tq = s.shape[1]
    blocks = [(s[8 * k:8 * k + 8], ids) for k, ids in enumerate(id_parts)]
    groups = [blocks[g:g + depth] for g in range(0, len(blocks), depth)]
    for col in groups:
        for done in range(depth - 1):
            for j in range(depth - 1 - done):
                (va, ia), (vb, ib) = col[j], col[j + 1]
                swap = vb > va
                col[j] = (jnp.where(swap, vb, va), jnp.where(swap, ib, ia))
                col[j + 1] = (jnp.where(swap, va, vb), jnp.where(swap, ia, ib))
    rank = lax.broadcasted_iota(jnp.int32, (TOPK, tq), 0)
    vals = jnp.zeros((TOPK, tq), F32)
    idxs = jnp.zeros((TOPK, tq), F32)
    for kk in range(TOPK):
        m, first = _argmax_rows([col[0] for col in groups])
        for col in groups:
            won = col[0][1] == first
            for d in range(depth - 1):
                col[d] = (jnp.where(won, col[d + 1][0], col[d][0]), jnp.where(won, col[d + 1][1], col[d][1]))
            col[depth - 1] = (jnp.where(won, -jnp.inf, col[depth - 1][0]), col[depth - 1][1])
        vals = jnp.where(rank == kk, m, vals)
        idxs = jnp.where(rank == kk, first, idxs)
    return vals, idxs


def _select_rows(sel, table):
    out = jnp.zeros_like(sel)
    for a in range(TOPK):
        out = jnp.where(sel == float(a), table[a:a + 1, :], out)
    return out


def _candidate_ids():
    ids = list(range(TOPK))
    for a in range(1, 8):
        ids += [a * TOPK + b if (a + 1) * (b + 1) <= TOPK else ID_NONE for b in range(8)]
    ids += [a * TOPK for a in range(8, TOPK)]
    return np.asarray(ids, np.float32)


def _topk_head(q0, q1, k0, k1, key_ids, cand_ids, cand_id_parts):
    (v1, x1), (v2, x2) = [_extract_topk_sorted(_dot_nt(k, q), key_ids, 4) for k, q in ((k0, q0), (k1, q1))]
    cand = jnp.concatenate([v1[0:1, :] + v2] + [v1[a:a + 1, :] + v2[0:8, :] for a in range(1, 8)]
                           + [v1[8:TOPK, :] + v2[0:1, :]], axis=0)
    cand = jnp.where(cand_ids < ID_NONE, cand, -jnp.inf)
    top_s, flat = _extract_topk_sorted(cand, cand_id_parts, 5)
    a_sel = jnp.floor(flat * (1.0 / TOPK))
    b_sel = flat - a_sel * TOPK
    e = jnp.exp(top_s - top_s[0:1, :])
    return _select_rows(a_sel, x1), _select_rows(b_sel, x2), e * (1.0 / jnp.sum(e, axis=0, keepdims=True))


def _pack_bf16_pair(a, b):
    abits = lax.bitcast_convert_type(a.astype(BF16).astype(F32), jnp.uint32)
    bbits = lax.bitcast_convert_type(b.astype(BF16).astype(F32), jnp.uint32)
    return lax.bitcast_convert_type(abits | (bbits >> 16), jnp.int32)


def _unpack_bf16_pair(word):
    bits = lax.bitcast_convert_type(word, jnp.uint32)
    hi = lax.bitcast_convert_type(bits & jnp.uint32(0xFFFF0000), F32)
    lo = lax.bitcast_convert_type(bits << 16, F32)
    return hi, lo

def _peer_body(tq, nh, x1_ref, xn_ref, qp_ref, keys_ref, ids_ref, ul_ref, uh_ref, vl_ref, vh_ref, gf_ref,
               o_ref, gate_ref, sel_ref, i1_ref, g_ref, i2t_ref):
    i = pl.program_id(0)
    c = pl.program_id(1)
    half_keys = N_KEYS // 2
    slot = lax.rem(i, 2)

    @pl.when(jnp.logical_and(c == 0, i == 0))
    def _():
        gate_ref[...] = jnp.zeros_like(gate_ref)

    @pl.when(jnp.logical_and(c == 0, i > 0))
    def _():
        for grp in range(tq // LANES):
            cols = slice(grp * LANES, (grp + 1) * LANES)
            i1_ref[cols, :] = sel_ref[1 - slot, 0, :, cols].T.astype(jnp.int32)
            i2t_ref[grp] = sel_ref[1 - slot, 1, :, cols].astype(jnp.int32)
            g_ref[cols, :] = sel_ref[1 - slot, 2, :, cols].T
        row = lax.broadcasted_iota(jnp.int32, (N_KEYS, N_KEYS), 0)
        lane = lax.broadcasted_iota(jnp.int32, (N_KEYS, N_KEYS), 1)

        def build(grp, carry):
            base = pl.multiple_of(grp * LANES, LANES)
            i1g, gg = i1_ref[pl.ds(base, LANES), :], g_ref[pl.ds(base, LANES), :]
            i2t = i2t_ref[grp]
            for r in range(LANES):
                p1 = jnp.where(i1g[r:r + 1, :] == row, gg[r:r + 1, :], 0.0).astype(BF16)
                p2 = jnp.where(i2t[:, r:r + 1] == lane, 1.0, 0.0).astype(BF16)
                gt = _dot(p1, p2)
                word = _pack_bf16_pair(gt[:half_keys], gt[half_keys:])
                gate_ref[pl.ds(pl.multiple_of((base + r) * G_PITCH, 8), half_keys), :] = word
            return carry

        lax.fori_loop(0, tq // LANES, build, 0)

    @pl.when(c == 0)
    def _():
        o_ref[...] = jnp.zeros_like(o_ref)

    sub = lax.broadcasted_iota(jnp.int32, (8, tq), 0).astype(F32)
    key_ids = [sub + float(r) for r in range(0, N_KEYS, 8)]
    cand_ids = ids_ref[...]
    cand_id_parts = [cand_ids[r:r + 8] for r in range(0, cand_ids.shape[0], 8)]
    sel = _topk_head(qp_ref[:, :N_KEYS], qp_ref[:, N_KEYS:], keys_ref[2 * c], keys_ref[2 * c + 1],
                     key_ids, cand_ids, cand_id_parts)
    rows = pl.ds(pl.multiple_of(c * TOPK, TOPK), TOPK)
    for k, val in enumerate(sel):
        sel_ref[slot, k, rows, :] = val

    xn = xn_ref[...]
    gates = [_unpack_bf16_pair(gate_ref[pl.ds(c * nh + ii, tq, stride=G_PITCH), :])
             for ii in range(nh)]
    contrib = []
    for half, (u_ref, v_ref) in enumerate(((ul_ref, vl_ref), (uh_ref, vh_ref))):
        a = _dot(xn, u_ref[...])
        act = 0.5 * a * (1.0 + lax.erf(a * float(np.sqrt(0.5))))
        w = jnp.concatenate([(act[:, ii * N_KEYS:(ii + 1) * N_KEYS] * gates[ii][half]).astype(BF16)
                             for ii in range(nh)], axis=1)
        contrib.append(_dot(w, v_ref[...]))
    o_ref[...] += contrib[0] + contrib[1]

    @pl.when(c == pl.num_programs(1) - 1)
    def _():
        y = x1_ref[...] + o_ref[...]
        ms = jnp.mean(y * y, axis=-1, keepdims=True)
        o_ref[...] = y * lax.rsqrt(ms + EPS) * gf_ref[...]


def _peer_u_blocks(u):
    ec = PEER_NH * N_KEYS
    return u.reshape(N_EXPERTS // ec, ec, D_MODEL).transpose(0, 2, 1).astype(BF16)


def _peer(x1, xn, qp, keys_bf, u_blk, v_bf, gf):
    N = x1.shape[0]
    tq = PEER_TILE
    nh = PEER_NH
    ec = nh * N_KEYS
    n_chunks = (N_KEYS // 2) // nh
    assert n_chunks == PEER_HEADS
    nt = N // tq
    nsel = PEER_HEADS * TOPK
    ids = _candidate_ids()
    ids = jnp.asarray(np.broadcast_to(ids[:, None], (ids.shape[0], tq)))
    prev = lambda i: jnp.maximum(i - 1, 0)
    tok = lambda w: pl.BlockSpec((tq, w), lambda i, c: (prev(i), 0), pipeline_mode=pl.Buffered(1))
    const = lambda shape: pl.BlockSpec(shape, lambda i, c: (0,) * len(shape))
    return pl.pallas_call(
        functools.partial(_peer_body, tq, nh),
        grid=(nt + 1, n_chunks),
        in_specs=[tok(D_MODEL), tok(D_MODEL),
                  pl.BlockSpec((tq, 2 * N_KEYS), lambda i, c: (jnp.minimum(i, nt - 1), c)),
                  const((2 * PEER_HEADS, N_KEYS, N_KEYS)), const(ids.shape),
                  pl.BlockSpec((None, D_MODEL, ec), lambda i, c: (c, 0, 0)),
                  pl.BlockSpec((None, D_MODEL, ec), lambda i, c: (c + n_chunks, 0, 0)),
                  pl.BlockSpec((ec, D_MODEL), lambda i, c: (c, 0)),
                  pl.BlockSpec((ec, D_MODEL), lambda i, c: (c + n_chunks, 0)),
                  const((1, D_MODEL))],
        out_specs=pl.BlockSpec((tq, D_MODEL), lambda i, c: (prev(i), 0)),
        out_shape=jax.ShapeDtypeStruct((N, D_MODEL), F32),
        scratch_shapes=[pltpu.VMEM((tq * G_PITCH, N_KEYS), jnp.int32),
                        pltpu.VMEM((2, 3, nsel, tq), F32),
                        pltpu.VMEM((tq, nsel), jnp.int32),
                        pltpu.VMEM((tq, nsel), F32),
                        pltpu.VMEM((tq // LANES, nsel, LANES), jnp.int32)],
        compiler_params=_params(("arbitrary", "arbitrary")),
        name="peer",
    )(x1, xn, qp, keys_bf, ids, u_blk, u_blk, v_bf, v_bf, gf.reshape(1, D_MODEL))


def _encoder(x, w):
    B, S, _ = x.shape
    f_in, qkv1, qkv4, qkv16 = _proj(x, w["norm1_g"], w["w_in"])
    f_out = _fourier(f_in, w["w_fourier"])
    a_out = _attention([qkv1, qkv4, qkv16], S)
    N = B * S
    flat = lambda a: a.reshape(N, a.shape[-1])
    x1, xn, qp = _mix(flat(x), flat(f_out), flat(a_out), w["w_out"], w["norm2_g"], w["w_query"])
    y = _peer(x1, xn, qp, w["sub_keys"], w["expert_u"], w["expert_v"], w["final_g"])
    return y.reshape(B, S, D_MODEL)


def kernel(x_prompt, x_sample, norm1_g, w_in, w_fourier, w_out, norm2_g, w_query, sub_keys, expert_u, expert_v, final_g):
    w = {
        "norm1_g": norm1_g[0], "w_in": w_in[0].astype(BF16), "w_fourier": w_fourier[0],
        "w_out": w_out[0].astype(BF16), "norm2_g": norm2_g[0], "w_query": w_query[0].astype(BF16),
        "sub_keys": sub_keys[0].reshape(2 * PEER_HEADS, N_KEYS, N_KEYS).astype(BF16),
        "expert_u": _peer_u_blocks(expert_u[0]), "expert_v": expert_v[0].astype(BF16), "final_g": final_g,
    }
    return (_encoder(x_prompt, w), _encoder(x_sample, w))
```

```python
import functools

import numpy as np
import jax
import jax.numpy as jnp
from jax import lax
from jax.experimental import pallas as pl
from jax.experimental.pallas import tpu as pltpu

F32 = jnp.float32
BF16 = jnp.bfloat16

D_MODEL = 1024
HEAD_DIM = 64
FOURIER_WIDTH = 512
ATTN_WIDTH = 512
IN_WIDTH = 2048
N_GROUPS = 8
DILATIONS = (1, 4, 16)
HALF_WINDOW = 64
ROPE_THETA = 500000.0
ROPE_DIM = 16
PEER_HEADS = 8
N_KEYS = 128
N_EXPERTS = N_KEYS * N_KEYS
TOPK = 16
EPS = 1e-6
NEG = -1e30

LANES = 128
DFT_INNER = 128
G_PITCH = 72
ID_NONE = 1e9
VMEM_LIMIT = 56 * 2**20

PROJ_TILE = 512
DFT_COL_TILE = 4096
DFT_ROW_TILE = 8
ATTN_QUERY_BLOCK = 128
ATTN_TILE = 2048
ATTN_MERGE_CHUNK = 256
MIX_TILE = 512
PEER_TILE = 512
PEER_NH = 8


def _params(sem, vmem=None):
    return pltpu.CompilerParams(dimension_semantics=sem, vmem_limit_bytes=vmem or VMEM_LIMIT)


def _dot(a, b):
    return jnp.dot(a, b, preferred_element_type=F32)


def _dot_nt(a, b):
    return lax.dot_general(a, b, (((1,), (1,)), ((), ())), preferred_element_type=F32)


def _proj_body(tm, x_ref, g_ref, w_ref, c_ref, sa_ref, sb_ref,
               f_ref, q_ref, k_ref, v_ref, q4_ref, k4_ref, v4_ref, q16_ref, k16_ref, v16_ref, stage_ref, hop_ref):
    x = x_ref[...]
    ms = jnp.mean(x * x, axis=-1, keepdims=True)
    h = (x * lax.rsqrt(ms + EPS) * g_ref[...]).astype(BF16)
    p = _dot(h, w_ref[...])
    f_ref[...] = p[:, :FOURIER_WIDTH].astype(BF16)
    c, sa, sb = c_ref[...], sa_ref[...], sb_ref[...]
    for j in range(ATTN_WIDTH // LANES):
        lo = j * LANES
        qc = p[:, FOURIER_WIDTH + lo:FOURIER_WIDTH + lo + LANES]
        kc = p[:, FOURIER_WIDTH + ATTN_WIDTH + lo:FOURIER_WIDTH + ATTN_WIDTH + lo + LANES]
        vc = p[:, FOURIER_WIDTH + 2 * ATTN_WIDTH + lo:FOURIER_WIDTH + 2 * ATTN_WIDTH + lo + LANES]
        qr = (qc * c + pltpu.roll(qc, LANES - 8, 1) * sa + pltpu.roll(qc, 8, 1) * sb) * (HEAD_DIM ** -0.5)
        kr = kc * c + pltpu.roll(kc, LANES - 8, 1) * sa + pltpu.roll(kc, 8, 1) * sb
        for a, (val, nat) in enumerate(((qr, q_ref), (kr, k_ref), (vc, v_ref))):
            nat[:, lo:lo + LANES] = val.astype(BF16)
            stage_ref[a, j] = val
    _, R1, R2 = DILATIONS
    step = R2 // R1
    for a, (d1_ref, d2_ref) in enumerate(((q4_ref, q16_ref), (k4_ref, k16_ref), (v4_ref, v16_ref))):
        for j in range(ATTN_WIDTH // LANES):
            for r in range(R1):
                piece = stage_ref[a, j, pl.ds(r, tm // R1, stride=R1), :]
                d1_ref[:, (j * R1 + r) * LANES:(j * R1 + r + 1) * LANES] = piece.astype(BF16)
                hop_ref[r] = piece
            for r2 in range(R2):
                piece = hop_ref[r2 % R1, pl.ds(r2 // R1, tm // R2, stride=step), :]
                d2_ref[:, (j * R2 + r2) * LANES:(j * R2 + r2 + 1) * LANES] = piece.astype(BF16)


def _rope_tables(S):
    half = ROPE_DIM // 2
    inv = ROPE_THETA ** (-(jnp.arange(half, dtype=F32) * 2.0) / ROPE_DIM)
    ang = jnp.arange(S, dtype=F32)[:, None] * inv[None, :]
    cos, sin = jnp.cos(ang), jnp.sin(ang)
    one = jnp.ones((S, HEAD_DIM - ROPE_DIM), F32)
    zero = jnp.zeros((S, HEAD_DIM - ROPE_DIM), F32)
    z8 = jnp.zeros((S, half), F32)
    c = jnp.concatenate([cos, cos, one], axis=1)
    sa = jnp.concatenate([-sin, z8, zero], axis=1)
    sb = jnp.concatenate([z8, sin, zero], axis=1)
    rep = LANES // HEAD_DIM
    return jnp.tile(c, (1, rep)), jnp.tile(sa, (1, rep)), jnp.tile(sb, (1, rep))


def _proj(x, g, w_bf):
    B, S, _ = x.shape
    tm = min(PROJ_TILE, S)
    c, sa, sb = _rope_tables(S)
    tab = pl.BlockSpec((tm, LANES), lambda b, i: (i, 0))
    out = pl.BlockSpec((None, tm, ATTN_WIDTH), lambda b, i: (b, i, 0))
    shp = jax.ShapeDtypeStruct((B, S, ATTN_WIDTH), BF16)
    dil_specs, dil_shapes = [], []
    for R in DILATIONS[1:]:
        dil_specs.append(pl.BlockSpec((None, tm // R, R * ATTN_WIDTH), lambda b, i: (b, i, 0)))
        dil_shapes.append(jax.ShapeDtypeStruct((B, S // R, R * ATTN_WIDTH), BF16))
    outs = pl.pallas_call(
        functools.partial(_proj_body, tm),
        grid=(B, S // tm),
        in_specs=[pl.BlockSpec((None, tm, D_MODEL), lambda b, i: (b, i, 0)),
                  pl.BlockSpec((1, D_MODEL), lambda b, i: (0, 0)),
                  pl.BlockSpec((D_MODEL, IN_WIDTH), lambda b, i: (0, 0)),
                  tab, tab, tab],
        out_specs=[out] * 4 + [dil_specs[0]] * 3 + [dil_specs[1]] * 3,
        out_shape=[shp] * 4 + [dil_shapes[0]] * 3 + [dil_shapes[1]] * 3,
        scratch_shapes=[pltpu.VMEM((3, ATTN_WIDTH // LANES, tm, LANES), F32),
                        pltpu.VMEM((DILATIONS[1], tm // DILATIONS[1], LANES), F32)],
        compiler_params=_params(("parallel", "parallel")),
        name="proj",
    )(x, g.reshape(1, D_MODEL), w_bf, c, sa, sb)
    return outs[0], outs[1:4], outs[4:7], outs[7:10]


def _cos_sin(n_rows, n_cols, period):
    i = jnp.arange(n_rows, dtype=jnp.int32)[:, None]
    j = jnp.arange(n_cols, dtype=jnp.int32)[None, :]
    ang = ((i * j) % period).astype(F32) * (2.0 * np.pi / period)
    return jnp.cos(ang), jnp.sin(ang)


def _dft_a_body(n1, d_ref, x_ref, yr_ref, yi_ref):
    y = _dot(d_ref[...], x_ref[...])
    yr_ref[...] = y[:n1].astype(BF16)
    yi_ref[...] = y[n1:].astype(BF16)


def _dft_b_body(ct, scale, yr_ref, yi_ref, tc_ref, ts_ref, d2_ref, c64_ref, s64_ref, wf_ref, o_ref):
    rep = FOURIER_WIDTH // LANES
    xr, xi = [], []
    for cc in range(ct):
        yr = yr_ref[cc].astype(F32)
        yi = yi_ref[cc].astype(F32)
        tc = jnp.concatenate([tc_ref[cc]] * rep, axis=1)
        ts = jnp.concatenate([ts_ref[cc]] * rep, axis=1)
        zr = yr * tc + yi * ts
        zi = yi * tc - yr * ts
        z = jnp.concatenate([zr, zi], axis=0).astype(BF16)
        xx = _dot(d2_ref[...], z)
        xr.append(xx[:DFT_INNER].astype(BF16))
        xi.append(xx[DFT_INNER:].astype(BF16))
    xr, xi = jnp.concatenate(xr, axis=0), jnp.concatenate(xi, axis=0)
    re = (_dot(xr, c64_ref[...]) + _dot(xi, s64_ref[...])) * scale
    y = _dot(re.astype(BF16), wf_ref[...]).astype(BF16)
    for cc in range(ct):
        o_ref[cc] = y[cc * DFT_INNER:(cc + 1) * DFT_INNER]


def _block_diag(m):
    G, n, _ = m.shape
    eye = jnp.eye(G, dtype=m.dtype)
    return (eye[:, None, :, None] * m[:, :, None, :]).reshape(G * n, G * n)


def _fourier(f_in, w_fourier):
    B, S, W = f_in.shape
    n2 = DFT_INNER
    n1 = S // n2
    cols = n2 * W
    tcw = DFT_COL_TILE
    c1, s1 = _cos_sin(n1, n1, n1)
    d1 = jnp.concatenate([c1, -s1], axis=0).astype(BF16)
    xa = f_in.reshape(B, n1, cols)
    blk = pl.BlockSpec((None, n1, tcw), lambda b, j: (b, 0, j))
    yshape = jax.ShapeDtypeStruct((B, n1, cols), BF16)
    yr, yi = pl.pallas_call(
        functools.partial(_dft_a_body, n1),
        grid=(B, cols // tcw),
        in_specs=[pl.BlockSpec((2 * n1, n1), lambda b, j: (0, 0)), blk],
        out_specs=[blk, blk],
        out_shape=[yshape, yshape],
        compiler_params=_params(("parallel", "parallel")),
        name="dft_a",
    )(d1, xa)

    yr = yr.reshape(B, n1, n2, W)
    yi = yi.reshape(B, n1, n2, W)
    twc, tws = _cos_sin(n1, n2, S)
    twc = jnp.broadcast_to(twc[:, :, None], (n1, n2, LANES))
    tws = jnp.broadcast_to(tws[:, :, None], (n1, n2, LANES))
    c2, s2 = _cos_sin(n2, n2, n2)
    d2 = jnp.concatenate([jnp.concatenate([c2, s2], axis=1),
                          jnp.concatenate([-s2, c2], axis=1)], axis=0).astype(BF16)
    cg, sg = _cos_sin(HEAD_DIM, HEAD_DIM, HEAD_DIM)
    c64 = _block_diag(jnp.broadcast_to(cg, (N_GROUPS, HEAD_DIM, HEAD_DIM))).astype(BF16)
    s64 = _block_diag(jnp.broadcast_to(sg, (N_GROUPS, HEAD_DIM, HEAD_DIM))).astype(BF16)
    wf = _block_diag(w_fourier).astype(BF16)
    ct = DFT_ROW_TILE
    scale = float(1.0 / np.sqrt(S * HEAD_DIM))
    yblk = pl.BlockSpec((None, ct, n2, W), lambda b, i: (b, i, 0, 0))
    tblk = pl.BlockSpec((ct, n2, LANES), lambda b, i: (i, 0, 0))
    full = lambda r, c: pl.BlockSpec((r, c), lambda b, i: (0, 0))
    out = pl.pallas_call(
        functools.partial(_dft_b_body, ct, scale),
        grid=(B, n1 // ct),
        in_specs=[yblk, yblk, tblk, tblk, full(2 * n2, 2 * n2), full(W, W), full(W, W), full(W, W)],
        out_specs=yblk,
        out_shape=jax.ShapeDtypeStruct((B, n1, n2, W), BF16),
        compiler_params=_params(("parallel", "parallel")),
        name="dft_b",
    )(yr, yi, twc, tws, d2, c64, s64, wf)
    return out.transpose(0, 2, 1, 3).reshape(B, S, W)


def _band_attend(qf, ks, vs, bias, lo_half):
    n = qf.shape[0]
    q2 = jnp.concatenate([jnp.where(lo_half, qf, 0.0), jnp.where(lo_half, 0.0, qf)], axis=0).astype(BF16)
    s = _dot_nt(q2, ks) + jnp.concatenate([bias, bias], axis=0)
    m = jnp.max(s, axis=1, keepdims=True)
    p = jnp.exp(s - m)
    l = jnp.sum(p, axis=1, keepdims=True)
    o = _dot(p.astype(BF16), vs) * (1.0 / l)
    lse = m + jnp.log(l)
    return jnp.where(lo_half, o[:n], o[n:]), jnp.where(lo_half, lse[:n], lse[n:])


def _attn_bias_terms():
    sub, win = ATTN_QUERY_BLOCK, ATTN_QUERY_BLOCK + 2 * HALF_WINDOW
    lane = lax.broadcasted_iota(jnp.int32, (sub, LANES), 1)
    r = lax.broadcasted_iota(jnp.int32, (sub, win), 0)
    c = lax.broadcasted_iota(jnp.int32, (sub, win), 1)
    band = jnp.where((c >= r) & (c <= r + 2 * HALF_WINDOW), 0.0, NEG)
    return lane < HEAD_DIM, band, lax.broadcasted_iota(jnp.int32, (1, win), 1)


def _attn_body(tb, S, *refs):
    ins, (a_ref, og_ref, lg_ref) = refs[:-3], refs[-3:]
    sub, win = ATTN_QUERY_BLOCK, ATTN_QUERY_BLOCK + 2 * HALF_WINDOW
    lo_half, band, key = _attn_bias_terms()
    i = pl.program_id(1)
    for g, R in enumerate(DILATIONS):
        q_ref, kp_ref, kc_ref, kn_ref, vp_ref, vc_ref, vn_ref = ins[7 * g:7 * g + 7]
        tq, L = tb // R, S // R
        n_sub = tq // sub
        for r in range(R):
            sl = slice(r * LANES, (r + 1) * LANES)
            k2 = jnp.concatenate([kp_ref[:, sl], kc_ref[:, sl], kn_ref[:, sl]], axis=0)
            v2 = jnp.concatenate([vp_ref[:, sl], vc_ref[:, sl], vn_ref[:, sl]], axis=0)
            for sb in range(n_sub):
                bias = band
                if sb == 0 or sb == n_sub - 1:
                    kpos = i * tq + sb * sub - HALF_WINDOW + key
                    bias = band + jnp.where((kpos >= 0) & (kpos < L), 0.0, NEG)
                o, lse = _band_attend(q_ref[sb * sub:(sb + 1) * sub, sl].astype(F32),
                                      k2[sb * sub:sb * sub + win], v2[sb * sub:sb * sub + win], bias, lo_half)
                dst = pl.ds(R * sb * sub + r, sub, stride=R)
                og_ref[g, dst, :] = o
                lg_ref[g, dst, :] = lse
    chunk = min(ATTN_MERGE_CHUNK, tb)
    for t in range(0, tb, chunk):
        rows = slice(t, t + chunk)
        l1, l2, l3 = lg_ref[0, rows, :], lg_ref[1, rows, :], lg_ref[2, rows, :]
        mx = jnp.maximum(jnp.maximum(l1, l2), l3)
        e1, e2, e3 = jnp.exp(l1 - mx), jnp.exp(l2 - mx), jnp.exp(l3 - mx)
        a = (e1 * og_ref[0, rows, :] + e2 * og_ref[1, rows, :] + e3 * og_ref[2, rows, :]) * (1.0 / (e1 + e2 + e3))
        a_ref[rows, :] = a.astype(BF16)


def _halo_index(hb, nhalo, offset):
    return lambda b, i, p: (b, jnp.clip(i * hb + offset, 0, nhalo - 1), p)


def _attention(qkv_groups, S):
    B = qkv_groups[0][0].shape[0]
    tb = min(ATTN_TILE, S)
    specs, args = [], []
    for (q, k, v), R in zip(qkv_groups, DILATIONS):
        tq, L, wb = tb // R, S // R, R * LANES
        hb, nhalo = tq // HALF_WINDOW, L // HALF_WINDOW
        ctr = pl.BlockSpec((None, tq, wb), lambda b, i, p: (b, i, p))
        prev = pl.BlockSpec((None, HALF_WINDOW, wb), _halo_index(hb, nhalo, -1))
        nxt = pl.BlockSpec((None, HALF_WINDOW, wb), _halo_index(hb, nhalo, hb))
        specs += [ctr, prev, ctr, nxt, prev, ctr, nxt]
        args += [q, k, k, k, v, v, v]
    return pl.pallas_call(
        functools.partial(_attn_body, tb, S),
        grid=(B, S // tb, ATTN_WIDTH // LANES),
        in_specs=specs,
        out_specs=pl.BlockSpec((None, tb, LANES), lambda b, i, p: (b, i, p)),
        out_shape=jax.ShapeDtypeStruct((B, S, ATTN_WIDTH), BF16),
        scratch_shapes=[pltpu.VMEM((len(DILATIONS), tb, LANES), F32), pltpu.VMEM((len(DILATIONS), tb, LANES), F32)],
        compiler_params=_params(("parallel", "parallel", "parallel")),
        name="attn",
    )(*args)


def _mix_body(x_ref, f_ref, a_ref, wf_ref, wa_ref, g_ref, wq_ref, x1_ref, xn_ref, qp_ref):
    x1 = x_ref[...] + _dot(f_ref[...], wf_ref[...]) + _dot(a_ref[...], wa_ref[...])
    x1_ref[...] = x1
    ms = jnp.mean(x1 * x1, axis=-1, keepdims=True)
    xn = (x1 * lax.rsqrt(ms + EPS) * g_ref[...]).astype(BF16)
    xn_ref[...] = xn
    qp_ref[...] = _dot(xn, wq_ref[...]).astype(BF16)


def _mix(x, f_out, a_out, w_out_bf, g2, wq_bf):
    N = x.shape[0]
    tm = MIX_TILE
    row = lambda w: pl.BlockSpec((tm, w), lambda i: (i, 0))
    full = lambda r, c: pl.BlockSpec((r, c), lambda i: (0, 0))
    qw = wq_bf.shape[1]
    return pl.pallas_call(
        _mix_body,
        grid=(N // tm,),
        in_specs=[row(D_MODEL), row(FOURIER_WIDTH), row(ATTN_WIDTH),
                  full(FOURIER_WIDTH, D_MODEL), full(ATTN_WIDTH, D_MODEL), full(1, D_MODEL), full(D_MODEL, qw)],
        out_specs=[row(D_MODEL), row(D_MODEL), row(qw)],
        out_shape=[jax.ShapeDtypeStruct((N, D_MODEL), F32), jax.ShapeDtypeStruct((N, D_MODEL), BF16),
                   jax.ShapeDtypeStruct((N, qw), BF16)],
        compiler_params=_params(("parallel",)),
        name="mix",
    )(x, f_out, a_out, w_out_bf[:FOURIER_WIDTH], w_out_bf[FOURIER_WIDTH:], g2.reshape(1, D_MODEL), wq_bf)


def _argmax_rows(parts):
    while len(parts) > 1:
        nxt = []
        for k in range(0, len(parts) - 1, 2):
            (va, ia), (vb, ib) = parts[k], parts[k + 1]
            take_b = vb > va
            nxt.append((jnp.where(take_b, vb, va), jnp.where(take_b, ib, ia)))
        if len(parts) % 2:
            nxt.append(parts[-1])
        parts = nxt
    v, i = parts[0]
    m = jnp.max(v, axis=0, keepdims=True)
    return m, jnp.min(jnp.where(v == m, i, ID_NONE), axis=0, keepdims=True)


def _extract_topk_sorted(s, id_parts, depth):
    tq = s.shape[1]
    blocks = [(s[8 * k:8 * k + 8], ids) for k, ids in enumerate(id_parts)]
    groups = [blocks[g:g + depth] for g in range(0, len(blocks), depth)]
    for col in groups:
        for done in range(depth - 1):
            for j in range(depth - 1 - done):
                (va, ia), (vb, ib) = col[j], col[j + 1]
                swap = vb > va
                col[j] = (jnp.where(swap, vb, va), jnp.where(swap, ib, ia))
                col[j + 1] = (jnp.where(swap, va, vb), jnp.where(swap, ia, ib))
    rank = lax.broadcasted_iota(jnp.int32, (TOPK, tq), 0)
    vals = jnp.zeros((TOPK, tq), F32)
    idxs = jnp.zeros((TOPK, tq), F32)
    for kk in range(TOPK):
        m, first = _argmax_rows([col[0] for col in groups])
        for col in groups:
            won = col[0][1] == first
            for d in range(depth - 1):
                col[d] = (jnp.where(won, col[d + 1][0], col[d][0]), jnp.where(won, col[d + 1][1], col[d][1]))
            col[depth - 1] = (jnp.where(won, -jnp.inf, col[depth - 1][0]), col[depth - 1][1])
        vals = jnp.where(rank == kk, m, vals)
        idxs = jnp.where(rank == kk, first, idxs)
    return vals, idxs


def _select_rows(sel, table):
    out = jnp.zeros_like(sel)
    for a in range(TOPK):
        out = jnp.where(sel == float(a), table[a:a + 1, :], out)
    return out


def _candidate_ids():
    ids = list(range(TOPK))
    for a in range(1, 8):
        ids += [a * TOPK + b if (a + 1) * (b + 1) <= TOPK else ID_NONE for b in range(8)]
    ids += [a * TOPK for a in range(8, TOPK)]
    return np.asarray(ids, np.float32)


def _topk_head(q0, q1, k0, k1, key_ids, cand_ids, cand_id_parts):
    (v1, x1), (v2, x2) = [_extract_topk_sorted(_dot_nt(k, q), key_ids, 4) for k, q in ((k0, q0), (k1, q1))]
    cand = jnp.concatenate([v1[0:1, :] + v2] + [v1[a:a + 1, :] + v2[0:8, :] for a in range(1, 8)]
                           + [v1[8:TOPK, :] + v2[0:1, :]], axis=0)
    cand = jnp.where(cand_ids < ID_NONE, cand, -jnp.inf)
    top_s, flat = _extract_topk_sorted(cand, cand_id_parts, 5)
    a_sel = jnp.floor(flat * (1.0 / TOPK))
    b_sel = flat - a_sel * TOPK
    e = jnp.exp(top_s - top_s[0:1, :])
    return _select_rows(a_sel, x1), _select_rows(b_sel, x2), e * (1.0 / jnp.sum(e, axis=0, keepdims=True))


def _pack_bf16_pair(a, b):
    abits = lax.bitcast_convert_type(a.astype(BF16).astype(F32), jnp.uint32)
    bbits = lax.bitcast_convert_type(b.astype(BF16).astype(F32), jnp.uint32)
    return lax.bitcast_convert_type(abits | (bbits >> 16), jnp.int32)


def _unpack_bf16_pair(word):
    bits = lax.bitcast_convert_type(word, jnp.uint32)
    hi = lax.bitcast_convert_type(bits & jnp.uint32(0xFFFF0000), F32)
    lo = lax.bitcast_convert_type(bits << 16, F32)
    return hi, lo

def _peer_body(tq, nh, x1_ref, xn_ref, qp_ref, keys_ref, ids_ref, ul_ref, uh_ref, vl_ref, vh_ref, gf_ref,
               o_ref, gate_ref, sel_ref, i1_ref, g_ref, i2t_ref):
    i = pl.program_id(0)
    c = pl.program_id(1)
    half_keys = N_KEYS // 2
    slot = lax.rem(i, 2)

    @pl.when(jnp.logical_and(c == 0, i == 0))
    def _():
        gate_ref[...] = jnp.zeros_like(gate_ref)

    @pl.when(jnp.logical_and(c == 0, i > 0))
    def _():
        for grp in range(tq // LANES):
            cols = slice(grp * LANES, (grp + 1) * LANES)
            i1_ref[cols, :] = sel_ref[1 - slot, 0, :, cols].T.astype(jnp.int32)
            i2t_ref[grp] = sel_ref[1 - slot, 1, :, cols].astype(jnp.int32)
            g_ref[cols, :] = sel_ref[1 - slot, 2, :, cols].T
        row = lax.broadcasted_iota(jnp.int32, (N_KEYS, N_KEYS), 0)
        lane = lax.broadcasted_iota(jnp.int32, (N_KEYS, N_KEYS), 1)

        def build(grp, carry):
            base = pl.multiple_of(grp * LANES, LANES)
            i1g, gg = i1_ref[pl.ds(base, LANES), :], g_ref[pl.ds(base, LANES), :]
            i2t = i2t_ref[grp]
            for r in range(LANES):
                p1 = jnp.where(i1g[r:r + 1, :] == row, gg[r:r + 1, :], 0.0).astype(BF16)
                p2 = jnp.where(i2t[:, r:r + 1] == lane, 1.0, 0.0).astype(BF16)
                gt = _dot(p1, p2)
                word = _pack_bf16_pair(gt[:half_keys], gt[half_keys:])
                gate_ref[pl.ds(pl.multiple_of((base + r) * G_PITCH, 8), half_keys), :] = word
            return carry

        lax.fori_loop(0, tq // LANES, build, 0)

    @pl.when(c == 0)
    def _():
        o_ref[...] = jnp.zeros_like(o_ref)

    sub = lax.broadcasted_iota(jnp.int32, (8, tq), 0).astype(F32)
    key_ids = [sub + float(r) for r in range(0, N_KEYS, 8)]
    cand_ids = ids_ref[...]
    cand_id_parts = [cand_ids[r:r + 8] for r in range(0, cand_ids.shape[0], 8)]
    sel = _topk_head(qp_ref[:, :N_KEYS], qp_ref[:, N_KEYS:], keys_ref[2 * c], keys_ref[2 * c + 1],
                     key_ids, cand_ids, cand_id_parts)
    rows = pl.ds(pl.multiple_of(c * TOPK, TOPK), TOPK)
    for k, val in enumerate(sel):
        sel_ref[slot, k, rows, :] = val

    xn = xn_ref[...]
    gates = [_unpack_bf16_pair(gate_ref[pl.ds(c * nh + ii, tq, stride=G_PITCH), :])
             for ii in range(nh)]
    contrib = []
    for half, (u_ref, v_ref) in enumerate(((ul_ref, vl_ref), (uh_ref, vh_ref))):
        a = _dot(xn, u_ref[...])
        act = 0.5 * a * (1.0 + lax.erf(a * float(np.sqrt(0.5))))
        w = jnp.concatenate([(act[:, ii * N_KEYS:(ii + 1) * N_KEYS] * gates[ii][half]).astype(BF16)
                             for ii in range(nh)], axis=1)
        contrib.append(_dot(w, v_ref[...]))
    o_ref[...] += contrib[0] + contrib[1]

    @pl.when(c == pl.num_programs(1) - 1)
    def _():
        y = x1_ref[...] + o_ref[...]
        ms = jnp.mean(y * y, axis=-1, keepdims=True)
        o_ref[...] = y * lax.rsqrt(ms + EPS) * gf_ref[...]


def _peer_u_blocks(u):
    ec = PEER_NH * N_KEYS
    return u.reshape(N_EXPERTS // ec, ec, D_MODEL).transpose(0, 2, 1).astype(BF16)


def _peer(x1, xn, qp, keys_bf, u_blk, v_bf, gf):
    N = x1.shape[0]
    tq = PEER_TILE
    nh = PEER_NH
    ec = nh * N_KEYS
    n_chunks = (N_KEYS // 2) // nh
    assert n_chunks == PEER_HEADS
    nt = N // tq
    nsel = PEER_HEADS * TOPK
    ids = _candidate_ids()
    ids = jnp.asarray(np.broadcast_to(ids[:, None], (ids.shape[0], tq)))
    prev = lambda i: jnp.maximum(i - 1, 0)
    tok = lambda w: pl.BlockSpec((tq, w), lambda i, c: (prev(i), 0))
    const = lambda shape: pl.BlockSpec(shape, lambda i, c: (0,) * len(shape))
    return pl.pallas_call(
        functools.partial(_peer_body, tq, nh),
        grid=(nt + 1, n_chunks),
        in_specs=[tok(D_MODEL), tok(D_MODEL),
                  pl.BlockSpec((tq, 2 * N_KEYS), lambda i, c: (jnp.minimum(i, nt - 1), c)),
                  const((2 * PEER_HEADS, N_KEYS, N_KEYS)), const(ids.shape),
                  pl.BlockSpec((None, D_MODEL, ec), lambda i, c: (c, 0, 0)),
                  pl.BlockSpec((None, D_MODEL, ec), lambda i, c: (c + n_chunks, 0, 0)),
                  pl.BlockSpec((ec, D_MODEL), lambda i, c: (c, 0)),
                  pl.BlockSpec((ec, D_MODEL), lambda i, c: (c + n_chunks, 0)),
                  const((1, D_MODEL))],
        out_specs=pl.BlockSpec((tq, D_MODEL), lambda i, c: (prev(i), 0)),
        out_shape=jax.ShapeDtypeStruct((N, D_MODEL), F32),
        scratch_shapes=[pltpu.VMEM((tq * G_PITCH, N_KEYS), jnp.int32),
                        pltpu.VMEM((2, 3, nsel, tq), F32),
                        pltpu.VMEM((tq, nsel), jnp.int32),
                        pltpu.VMEM((tq, nsel), F32),
                        pltpu.VMEM((tq // LANES, nsel, LANES), jnp.int32)],
        compiler_params=_params(("arbitrary", "arbitrary")),
        name="peer",
    )(x1, xn, qp, keys_bf, ids, u_blk, u_blk, v_bf, v_bf, gf.reshape(1, D_MODEL))


def _encoder(x, w):
    B, S, _ = x.shape
    f_in, qkv1, qkv4, qkv16 = _proj(x, w["norm1_g"], w["w_in"])
    f_out = _fourier(f_in, w["w_fourier"])
    a_out = _attention([qkv1, qkv4, qkv16], S)
    N = B * S
    flat = lambda a: a.reshape(N, a.shape[-1])
    x1, xn, qp = _mix(flat(x), flat(f_out), flat(a_out), w["w_out"], w["norm2_g"], w["w_query"])
    y = _peer(x1, xn, qp, w["sub_keys"], w["expert_u"], w["expert_v"], w["final_g"])
    return y.reshape(B, S, D_MODEL)


def kernel(x_prompt, x_sample, norm1_g, w_in, w_fourier, w_out, norm2_g, w_query, sub_keys, expert_u, expert_v, final_g):
    w = {
        "norm1_g": norm1_g[0], "w_in": w_in[0].astype(BF16), "w_fourier": w_fourier[0],
        "w_out": w_out[0].astype(BF16), "norm2_g": norm2_g[0], "w_query": w_query[0].astype(BF16),
        "sub_keys": sub_keys[0].reshape(2 * PEER_HEADS, N_KEYS, N_KEYS).astype(BF16),
        "expert_u": _peer_u_blocks(expert_u[0]), "expert_v": expert_v[0].astype(BF16), "final_g": final_g,
    }
    return (_encoder(x_prompt, w), _encoder(x_sample, w))
```

```python
import functools

import numpy as np
import jax
import jax.numpy as jnp
from jax import lax
from jax.experimental import pallas as pl
from jax.experimental.pallas import tpu as pltpu

F32 = jnp.float32
BF16 = jnp.bfloat16

D_MODEL = 1024
HEAD_DIM = 64
FOURIER_WIDTH = 512
ATTN_WIDTH = 512
IN_WIDTH = 2048
N_GROUPS = 8
DILATIONS = (1, 4, 16)
HALF_WINDOW = 64
ROPE_THETA = 500000.0
ROPE_DIM = 16
PEER_HEADS = 8
N_KEYS = 128
N_EXPERTS = N_KEYS * N_KEYS
TOPK = 16
EPS = 1e-6
NEG = -1e30

LANES = 128
DFT_INNER = 128
G_PITCH = 72
ID_NONE = 1e9
VMEM_LIMIT = 56 * 2**20

PROJ_TILE = 512
DFT_COL_TILE = 4096
DFT_ROW_TILE = 8
ATTN_QUERY_BLOCK = 128
ATTN_TILE = 2048
ATTN_MERGE_CHUNK = 256
MIX_TILE = 512
PEER_TILE = 512
PEER_NH = 8


def _params(sem, vmem=None):
    return pltpu.CompilerParams(dimension_semantics=sem, vmem_limit_bytes=vmem or VMEM_LIMIT)


def _dot(a, b):
    return jnp.dot(a, b, preferred_element_type=F32)


def _dot_nt(a, b):
    return lax.dot_general(a, b, (((1,), (1,)), ((), ())), preferred_element_type=F32)


def _proj_body(tm, x_ref, g_ref, w_ref, c_ref, sa_ref, sb_ref,
               f_ref, q_ref, k_ref, v_ref, q4_ref, k4_ref, v4_ref, q16_ref, k16_ref, v16_ref, stage_ref, hop_ref):
    x = x_ref[...]
    ms = jnp.mean(x * x, axis=-1, keepdims=True)
    h = (x * lax.rsqrt(ms + EPS) * g_ref[...]).astype(BF16)
    p = _dot(h, w_ref[...])
    f_ref[...] = p[:, :FOURIER_WIDTH].astype(BF16)
    c, sa, sb = c_ref[...], sa_ref[...], sb_ref[...]
    for j in range(ATTN_WIDTH // LANES):
        lo = j * LANES
        qc = p[:, FOURIER_WIDTH + lo:FOURIER_WIDTH + lo + LANES]
        kc = p[:, FOURIER_WIDTH + ATTN_WIDTH + lo:FOURIER_WIDTH + ATTN_WIDTH + lo + LANES]
        vc = p[:, FOURIER_WIDTH + 2 * ATTN_WIDTH + lo:FOURIER_WIDTH + 2 * ATTN_WIDTH + lo + LANES]
        qr = (qc * c + pltpu.roll(qc, LANES - 8, 1) * sa + pltpu.roll(qc, 8, 1) * sb) * (HEAD_DIM ** -0.5)
        kr = kc * c + pltpu.roll(kc, LANES - 8, 1) * sa + pltpu.roll(kc, 8, 1) * sb
        for a, (val, nat) in enumerate(((qr, q_ref), (kr, k_ref), (vc, v_ref))):
            nat[:, lo:lo + LANES] = val.astype(BF16)
            stage_ref[a, j] = val
    _, R1, R2 = DILATIONS
    step = R2 // R1
    for a, (d1_ref, d2_ref) in enumerate(((q4_ref, q16_ref), (k4_ref, k16_ref), (v4_ref, v16_ref))):
        for j in range(ATTN_WIDTH // LANES):
            for r in range(R1):
                piece = stage_ref[a, j, pl.ds(r, tm // R1, stride=R1), :]
                d1_ref[:, (j * R1 + r) * LANES:(j * R1 + r + 1) * LANES] = piece.astype(BF16)
                hop_ref[r] = piece
            for r2 in range(R2):
                piece = hop_ref[r2 % R1, pl.ds(r2 // R1, tm // R2, stride=step), :]
                d2_ref[:, (j * R2 + r2) * LANES:(j * R2 + r2 + 1) * LANES] = piece.astype(BF16)


def _rope_tables(S):
    half = ROPE_DIM // 2
    inv = ROPE_THETA ** (-(jnp.arange(half, dtype=F32) * 2.0) / ROPE_DIM)
    ang = jnp.arange(S, dtype=F32)[:, None] * inv[None, :]
    cos, sin = jnp.cos(ang), jnp.sin(ang)
    one = jnp.ones((S, HEAD_DIM - ROPE_DIM), F32)
    zero = jnp.zeros((S, HEAD_DIM - ROPE_DIM), F32)
    z8 = jnp.zeros((S, half), F32)
    c = jnp.concatenate([cos, cos, one], axis=1)
    sa = jnp.concatenate([-sin, z8, zero], axis=1)
    sb = jnp.concatenate([z8, sin, zero], axis=1)
    rep = LANES // HEAD_DIM
    return jnp.tile(c, (1, rep)), jnp.tile(sa, (1, rep)), jnp.tile(sb, (1, rep))


def _proj(x, g, w_bf):
    B, S, _ = x.shape
    tm = min(PROJ_TILE, S)
    c, sa, sb = _rope_tables(S)
    tab = pl.BlockSpec((tm, LANES), lambda b, i: (i, 0))
    out = pl.BlockSpec((None, tm, ATTN_WIDTH), lambda b, i: (b, i, 0))
    shp = jax.ShapeDtypeStruct((B, S, ATTN_WIDTH), BF16)
    dil_specs, dil_shapes = [], []
    for R in DILATIONS[1:]:
        dil_specs.append(pl.BlockSpec((None, tm // R, R * ATTN_WIDTH), lambda b, i: (b, i, 0)))
        dil_shapes.append(jax.ShapeDtypeStruct((B, S // R, R * ATTN_WIDTH), BF16))
    outs = pl.pallas_call(
        functools.partial(_proj_body, tm),
        grid=(B, S // tm),
        in_specs=[pl.BlockSpec((None, tm, D_MODEL), lambda b, i: (b, i, 0)),
                  pl.BlockSpec((1, D_MODEL), lambda b, i: (0, 0)),
                  pl.BlockSpec((D_MODEL, IN_WIDTH), lambda b, i: (0, 0)),
                  tab, tab, tab],
        out_specs=[out] * 4 + [dil_specs[0]] * 3 + [dil_specs[1]] * 3,
        out_shape=[shp] * 4 + [dil_shapes[0]] * 3 + [dil_shapes[1]] * 3,
        scratch_shapes=[pltpu.VMEM((3, ATTN_WIDTH // LANES, tm, LANES), F32),
                        pltpu.VMEM((DILATIONS[1], tm // DILATIONS[1], LANES), F32)],
        compiler_params=_params(("parallel", "parallel")),
        name="proj",
    )(x, g.reshape(1, D_MODEL), w_bf, c, sa, sb)
    return outs[0], outs[1:4], outs[4:7], outs[7:10]


def _cos_sin(n_rows, n_cols, period):
    i = jnp.arange(n_rows, dtype=jnp.int32)[:, None]
    j = jnp.arange(n_cols, dtype=jnp.int32)[None, :]
    ang = ((i * j) % period).astype(F32) * (2.0 * np.pi / period)
    return jnp.cos(ang), jnp.sin(ang)


def _dft_a_body(n1, d_ref, x_ref, yr_ref, yi_ref):
    y = _dot(d_ref[...], x_ref[...])
    yr_ref[...] = y[:n1].astype(BF16)
    yi_ref[...] = y[n1:].astype(BF16)


def _dft_b_body(ct, scale, yr_ref, yi_ref, tc_ref, ts_ref, d2_ref, c64_ref, s64_ref, wf_ref, o_ref):
    rep = FOURIER_WIDTH // LANES
    xr, xi = [], []
    for cc in range(ct):
        yr = yr_ref[cc].astype(F32)
        yi = yi_ref[cc].astype(F32)
        tc = jnp.concatenate([tc_ref[cc]] * rep, axis=1)
        ts = jnp.concatenate([ts_ref[cc]] * rep, axis=1)
        zr = yr * tc + yi * ts
        zi = yi * tc - yr * ts
        z = jnp.concatenate([zr, zi], axis=0).astype(BF16)
        xx = _dot(d2_ref[...], z)
        xr.append(xx[:DFT_INNER].astype(BF16))
        xi.append(xx[DFT_INNER:].astype(BF16))
    xr, xi = jnp.concatenate(xr, axis=0), jnp.concatenate(xi, axis=0)
    re = (_dot(xr, c64_ref[...]) + _dot(xi, s64_ref[...])) * scale
    y = _dot(re.astype(BF16), wf_ref[...]).astype(BF16)
    for cc in range(ct):
        o_ref[cc] = y[cc * DFT_INNER:(cc + 1) * DFT_INNER]


def _block_diag(m):
    G, n, _ = m.shape
    eye = jnp.eye(G, dtype=m.dtype)
    return (eye[:, None, :, None] * m[:, :, None, :]).reshape(G * n, G * n)


def _fourier(f_in, w_fourier):
    B, S, W = f_in.shape
    n2 = DFT_INNER
    n1 = S // n2
    cols = n2 * W
    tcw = DFT_COL_TILE
    c1, s1 = _cos_sin(n1, n1, n1)
    d1 = jnp.concatenate([c1, -s1], axis=0).astype(BF16)
    xa = f_in.reshape(B, n1, cols)
    blk = pl.BlockSpec((None, n1, tcw), lambda b, j: (b, 0, j))
    yshape = jax.ShapeDtypeStruct((B, n1, cols), BF16)
    yr, yi = pl.pallas_call(
        functools.partial(_dft_a_body, n1),
        grid=(B, cols // tcw),
        in_specs=[pl.BlockSpec((2 * n1, n1), lambda b, j: (0, 0)), blk],
        out_specs=[blk, blk],
        out_shape=[yshape, yshape],
        compiler_params=_params(("parallel", "parallel")),
        name="dft_a",
    )(d1, xa)

    yr = yr.reshape(B, n1, n2, W)
    yi = yi.reshape(B, n1, n2, W)
    twc, tws = _cos_sin(n1, n2, S)
    twc = jnp.broadcast_to(twc[:, :, None], (n1, n2, LANES))
    tws = jnp.broadcast_to(tws[:, :, None], (n1, n2, LANES))
    c2, s2 = _cos_sin(n2, n2, n2)
    d2 = jnp.concatenate([jnp.concatenate([c2, s2], axis=1),
                          jnp.concatenate([-s2, c2], axis=1)], axis=0).astype(BF16)
    cg, sg = _cos_sin(HEAD_DIM, HEAD_DIM, HEAD_DIM)
    c64 = _block_diag(jnp.broadcast_to(cg, (N_GROUPS, HEAD_DIM, HEAD_DIM))).astype(BF16)
    s64 = _block_diag(jnp.broadcast_to(sg, (N_GROUPS, HEAD_DIM, HEAD_DIM))).astype(BF16)
    wf = _block_diag(w_fourier).astype(BF16)
    ct = DFT_ROW_TILE
    scale = float(1.0 / np.sqrt(S * HEAD_DIM))
    yblk = pl.BlockSpec((None, ct, n2, W), lambda b, i: (b, i, 0, 0))
    tblk = pl.BlockSpec((ct, n2, LANES), lambda b, i: (i, 0, 0))
    full = lambda r, c: pl.BlockSpec((r, c), lambda b, i: (0, 0))
    out = pl.pallas_call(
        functools.partial(_dft_b_body, ct, scale),
        grid=(B, n1 // ct),
        in_specs=[yblk, yblk, tblk, tblk, full(2 * n2, 2 * n2), full(W, W), full(W, W), full(W, W)],
        out_specs=yblk,
        out_shape=jax.ShapeDtypeStruct((B, n1, n2, W), BF16),
        compiler_params=_params(("parallel", "parallel")),
        name="dft_b",
    )(yr, yi, twc, tws, d2, c64, s64, wf)
    return out.transpose(0, 2, 1, 3).reshape(B, S, W)


def _band_attend(qf, ks, vs, bias, lo_half):
    n = qf.shape[0]
    q2 = jnp.concatenate([jnp.where(lo_half, qf, 0.0), jnp.where(lo_half, 0.0, qf)], axis=0).astype(BF16)
    s = _dot_nt(q2, ks) + jnp.concatenate([bias, bias], axis=0)
    m = jnp.max(s, axis=1, keepdims=True)
    p = jnp.exp(s - m)
    l = jnp.sum(p, axis=1, keepdims=True)
    o = _dot(p.astype(BF16), vs) * (1.0 / l)
    lse = m + jnp.log(l)
    return jnp.where(lo_half, o[:n], o[n:]), jnp.where(lo_half, lse[:n], lse[n:])


def _attn_bias_terms():
    sub, win = ATTN_QUERY_BLOCK, ATTN_QUERY_BLOCK + 2 * HALF_WINDOW
    lane = lax.broadcasted_iota(jnp.int32, (sub, LANES), 1)
    r = lax.broadcasted_iota(jnp.int32, (sub, win), 0)
    c = lax.broadcasted_iota(jnp.int32, (sub, win), 1)
    band = jnp.where((c >= r) & (c <= r + 2 * HALF_WINDOW), 0.0, NEG)
    return lane < HEAD_DIM, band, lax.broadcasted_iota(jnp.int32, (1, win), 1)


def _attn_body(tb, S, *refs):
    ins, (a_ref, og_ref, lg_ref) = refs[:-3], refs[-3:]
    sub, win = ATTN_QUERY_BLOCK, ATTN_QUERY_BLOCK + 2 * HALF_WINDOW
    lo_half, band, key = _attn_bias_terms()
    i = pl.program_id(1)
    for g, R in enumerate(DILATIONS):
        q_ref, kp_ref, kc_ref, kn_ref, vp_ref, vc_ref, vn_ref = ins[7 * g:7 * g + 7]
        tq, L = tb // R, S // R
        n_sub = tq // sub
        for r in range(R):
            sl = slice(r * LANES, (r + 1) * LANES)
            k2 = jnp.concatenate([kp_ref[:, sl], kc_ref[:, sl], kn_ref[:, sl]], axis=0)
            v2 = jnp.concatenate([vp_ref[:, sl], vc_ref[:, sl], vn_ref[:, sl]], axis=0)
            for sb in range(n_sub):
                bias = band
                if sb == 0 or sb == n_sub - 1:
                    kpos = i * tq + sb * sub - HALF_WINDOW + key
                    bias = band + jnp.where((kpos >= 0) & (kpos < L), 0.0, NEG)
                o, lse = _band_attend(q_ref[sb * sub:(sb + 1) * sub, sl].astype(F32),
                                      k2[sb * sub:sb * sub + win], v2[sb * sub:sb * sub + win], bias, lo_half)
                dst = pl.ds(R * sb * sub + r, sub, stride=R)
                og_ref[g, dst, :] = o
                lg_ref[g, dst, :] = lse
    chunk = min(ATTN_MERGE_CHUNK, tb)
    for t in range(0, tb, chunk):
        rows = slice(t, t + chunk)
        l1, l2, l3 = lg_ref[0, rows, :], lg_ref[1, rows, :], lg_ref[2, rows, :]
        mx = jnp.maximum(jnp.maximum(l1, l2), l3)
        e1, e2, e3 = jnp.exp(l1 - mx), jnp.exp(l2 - mx), jnp.exp(l3 - mx)
        a = (e1 * og_ref[0, rows, :] + e2 * og_ref[1, rows, :] + e3 * og_ref[2, rows, :]) * (1.0 / (e1 + e2 + e3))
        a_ref[rows, :] = a.astype(BF16)


def _halo_index(hb, nhalo, offset):
    return lambda b, i, p: (b, jnp.clip(i * hb + offset, 0, nhalo - 1), p)


def _attention(qkv_groups, S):
    B = qkv_groups[0][0].shape[0]
    tb = min(ATTN_TILE, S)
    specs, args = [], []
    for (q, k, v), R in zip(qkv_groups, DILATIONS):
        tq, L, wb = tb // R, S // R, R * LANES
        hb, nhalo = tq // HALF_WINDOW, L // HALF_WINDOW
        ctr = pl.BlockSpec((None, tq, wb), lambda b, i, p: (b, i, p))
        prev = pl.BlockSpec((None, HALF_WINDOW, wb), _halo_index(hb, nhalo, -1))
        nxt = pl.BlockSpec((None, HALF_WINDOW, wb), _halo_index(hb, nhalo, hb))
        specs += [ctr, prev, ctr, nxt, prev, ctr, nxt]
        args += [q, k, k, k, v, v, v]
    return pl.pallas_call(
        functools.partial(_attn_body, tb, S),
        grid=(B, S // tb, ATTN_WIDTH // LANES),
        in_specs=specs,
        out_specs=pl.BlockSpec((None, tb, LANES), lambda b, i, p: (b, i, p)),
        out_shape=jax.ShapeDtypeStruct((B, S, ATTN_WIDTH), BF16),
        scratch_shapes=[pltpu.VMEM((len(DILATIONS), tb, LANES), F32), pltpu.VMEM((len(DILATIONS), tb, LANES), F32)],
        compiler_params=_params(("parallel", "parallel", "parallel")),
        name="attn",
    )(*args)


def _mix_body(x_ref, f_ref, a_ref, wf_ref, wa_ref, g_ref, wq_ref, x1_ref, xn_ref, qp_ref):
    x1 = x_ref[...] + _dot(f_ref[...], wf_ref[...]) + _dot(a_ref[...], wa_ref[...])
    x1_ref[...] = x1
    ms = jnp.mean(x1 * x1, axis=-1, keepdims=True)
    xn = (x1 * lax.rsqrt(ms + EPS) * g_ref[...]).astype(BF16)
    xn_ref[...] = xn
    qp_ref[...] = _dot(xn, wq_ref[...]).astype(BF16)


def _mix(x, f_out, a_out, w_out_bf, g2, wq_bf):
    N = x.shape[0]
    tm = MIX_TILE
    row = lambda w: pl.BlockSpec((tm, w), lambda i: (i, 0))
    full = lambda r, c: pl.BlockSpec((r, c), lambda i: (0, 0))
    qw = wq_bf.shape[1]
    return pl.pallas_call(
        _mix_body,
        grid=(N // tm,),
        in_specs=[row(D_MODEL), row(FOURIER_WIDTH), row(ATTN_WIDTH),
                  full(FOURIER_WIDTH, D_MODEL), full(ATTN_WIDTH, D_MODEL), full(1, D_MODEL), full(D_MODEL, qw)],
        out_specs=[row(D_MODEL), row(D_MODEL), row(qw)],
        out_shape=[jax.ShapeDtypeStruct((N, D_MODEL), F32), jax.ShapeDtypeStruct((N, D_MODEL), BF16),
                   jax.ShapeDtypeStruct((N, qw), BF16)],
        compiler_params=_params(("parallel",)),
        name="mix",
    )(x, f_out, a_out, w_out_bf[:FOURIER_WIDTH], w_out_bf[FOURIER_WIDTH:], g2.reshape(1, D_MODEL), wq_bf)


def _argmax_rows(parts):
    while len(parts) > 1:
        nxt = []
        for k in range(0, len(parts) - 1, 2):
            (va, ia), (vb, ib) = parts[k], parts[k + 1]
            take_b = vb > va
            nxt.append((jnp.where(take_b, vb, va), jnp.where(take_b, ib, ia)))
        if len(parts) % 2:
            nxt.append(parts[-1])
        parts = nxt
    v, i = parts[0]
    m = jnp.max(v, axis=0, keepdims=True)
    return m, jnp.min(jnp.where(v == m, i, ID_NONE), axis=0, keepdims=True)


def _extract_topk_sorted(s, id_parts, depth):
    tq = s.shape[1]
    blocks = [(s[8 * k:8 * k + 8], ids) for k, ids in enumerate(id_parts)]
    groups = [blocks[g:g + depth] for g in range(0, len(blocks), depth)]
    for col in groups:
        for done in range(depth - 1):
            for j in range(depth - 1 - done):
                (va, ia), (vb, ib) = col[j], col[j + 1]
                swap = vb > va
                col[j] = (jnp.where(swap, vb, va), jnp.where(swap, ib, ia))
                col[j + 1] = (jnp.where(swap, va, vb), jnp.where(swap, ia, ib))
    rank = lax.broadcasted_iota(jnp.int32, (TOPK, tq), 0)
    vals = jnp.zeros((TOPK, tq), F32)
    idxs = jnp.zeros((TOPK, tq), F32)
    for kk in range(TOPK):
        m, first = _argmax_rows([col[0] for col in groups])
        for col in groups:
            won = col[0][1] == first
            for d in range(depth - 1):
                col[d] = (jnp.where(won, col[d + 1][0], col[d][0]), jnp.where(won, col[d + 1][1], col[d][1]))
            col[depth - 1] = (jnp.where(won, -jnp.inf, col[depth - 1][0]), col[depth - 1][1])
        vals = jnp.where(rank == kk, m, vals)
        idxs = jnp.where(rank == kk, first, idxs)
    return vals, idxs


def _select_rows(sel, table):
    out = jnp.zeros_like(sel)
    for a in range(TOPK):
        out = jnp.where(sel == float(a), table[a:a + 1, :], out)
    return out


def _candidate_ids():
    ids = list(range(TOPK))
    for a in range(1, 8):
        ids += [a * TOPK + b if (a + 1) * (b + 1) <= TOPK else ID_NONE for b in range(8)]
    ids += [a * TOPK for a in range(8, TOPK)]
    return np.asarray(ids, np.float32)


def _topk_head(q0, q1, k0, k1, key_ids, cand_ids, cand_id_parts):
    (v1, x1), (v2, x2) = [_extract_topk_sorted(_dot_nt(k, q), key_ids, 4) for k, q in ((k0, q0), (k1, q1))]
    cand = jnp.concatenate([v1[0:1, :] + v2] + [v1[a:a + 1, :] + v2[0:8, :] for a in range(1, 8)]
                           + [v1[8:TOPK, :] + v2[0:1, :]], axis=0)
    cand = jnp.where(cand_ids < ID_NONE, cand, -jnp.inf)
    top_s, flat = _extract_topk_sorted(cand, cand_id_parts, 5)
    a_sel = jnp.floor(flat * (1.0 / TOPK))
    b_sel = flat - a_sel * TOPK
    e = jnp.exp(top_s - top_s[0:1, :])
    return _select_rows(a_sel, x1), _select_rows(b_sel, x2), e * (1.0 / jnp.sum(e, axis=0, keepdims=True))


def _pack_bf16_pair(a, b):
    abits = lax.bitcast_convert_type(a.astype(BF16).astype(F32), jnp.uint32)
    bbits = lax.bitcast_convert_type(b.astype(BF16).astype(F32), jnp.uint32)
    return lax.bitcast_convert_type(abits | (bbits >> 16), jnp.int32)


def _unpack_bf16_pair(word):
    bits = lax.bitcast_convert_type(word, jnp.uint32)
    hi = lax.bitcast_convert_type(bits & jnp.uint32(0xFFFF0000), F32)
    lo = lax.bitcast_convert_type(bits << 16, F32)
    return hi, lo

def _peer_body(tq, nh, x1_ref, xn_ref, qp_ref, keys_ref, ids_ref, u_ref, v_ref, gf_ref,
               o_ref, gate_ref, sel_ref, i1_ref, g_ref, i2t_ref):
    i = pl.program_id(0)
    c = pl.program_id(1)
    half_keys = N_KEYS // 2
    slot = lax.rem(i, 2)

    @pl.when(jnp.logical_and(c == 0, i == 0))
    def _():
        gate_ref[...] = jnp.zeros_like(gate_ref)

    @pl.when(jnp.logical_and(c == 0, i > 0))
    def _():
        for grp in range(tq // LANES):
            cols = slice(grp * LANES, (grp + 1) * LANES)
            i1_ref[cols, :] = sel_ref[1 - slot, 0, :, cols].T.astype(jnp.int32)
            i2t_ref[grp] = sel_ref[1 - slot, 1, :, cols].astype(jnp.int32)
            g_ref[cols, :] = sel_ref[1 - slot, 2, :, cols].T
        row = lax.broadcasted_iota(jnp.int32, (N_KEYS, N_KEYS), 0)
        lane = lax.broadcasted_iota(jnp.int32, (N_KEYS, N_KEYS), 1)

        def build(grp, carry):
            base = pl.multiple_of(grp * LANES, LANES)
            i1g, gg = i1_ref[pl.ds(base, LANES), :], g_ref[pl.ds(base, LANES), :]
            i2t = i2t_ref[grp]
            for r in range(LANES):
                p1 = jnp.where(i1g[r:r + 1, :] == row, gg[r:r + 1, :], 0.0).astype(BF16)
                p2 = jnp.where(i2t[:, r:r + 1] == lane, 1.0, 0.0).astype(BF16)
                gt = _dot(p1, p2)
                word = _pack_bf16_pair(gt[:half_keys], gt[half_keys:])
                gate_ref[pl.ds(pl.multiple_of((base + r) * G_PITCH, 8), half_keys), :] = word
            return carry

        lax.fori_loop(0, tq // LANES, build, 0)

    @pl.when(c == 0)
    def _():
        o_ref[...] = jnp.zeros_like(o_ref)

    sub = lax.broadcasted_iota(jnp.int32, (8, tq), 0).astype(F32)
    key_ids = [sub + float(r) for r in range(0, N_KEYS, 8)]
    cand_ids = ids_ref[...]
    cand_id_parts = [cand_ids[r:r + 8] for r in range(0, cand_ids.shape[0], 8)]
    sel = _topk_head(qp_ref[:, :N_KEYS], qp_ref[:, N_KEYS:], keys_ref[2 * c], keys_ref[2 * c + 1],
                     key_ids, cand_ids, cand_id_parts)
    rows = pl.ds(pl.multiple_of(c * TOPK, TOPK), TOPK)
    for k, val in enumerate(sel):
        sel_ref[slot, k, rows, :] = val

    gates = [_unpack_bf16_pair(gate_ref[pl.ds(c * nh + ii, tq, stride=G_PITCH), :])
             for ii in range(nh)]
    a = _dot(xn_ref[...], u_ref[...])
    act = 0.5 * a * (1.0 + lax.erf(a * float(np.sqrt(0.5))))
    w = jnp.concatenate([(act[:, (half * nh + ii) * N_KEYS:(half * nh + ii + 1) * N_KEYS] * gates[ii][half]).astype(BF16)
                         for half in range(2) for ii in range(nh)], axis=1)
    o_ref[...] += _dot(w, v_ref[...])

    @pl.when(c == pl.num_programs(1) - 1)
    def _():
        y = x1_ref[...] + o_ref[...]
        ms = jnp.mean(y * y, axis=-1, keepdims=True)
        o_ref[...] = y * lax.rsqrt(ms + EPS) * gf_ref[...]


def _peer_tables(u, v):
    ec = PEER_NH * N_KEYS
    n_chunks = N_EXPERTS // (2 * ec)
    u4 = u.reshape(2, n_chunks, ec, D_MODEL).transpose(1, 3, 0, 2)
    v4 = v.reshape(2, n_chunks, ec, D_MODEL).transpose(1, 0, 2, 3)
    return (u4.reshape(n_chunks, D_MODEL, 2 * ec).astype(BF16), v4.reshape(n_chunks, 2 * ec, D_MODEL).astype(BF16))


def _peer(x1, xn, qp, keys_bf, u_blk, v_blk, gf):
    N = x1.shape[0]
    tq = PEER_TILE
    nh = PEER_NH
    ec = nh * N_KEYS
    n_chunks = (N_KEYS // 2) // nh
    assert n_chunks == PEER_HEADS
    nt = N // tq
    nsel = PEER_HEADS * TOPK
    ids = _candidate_ids()
    ids = jnp.asarray(np.broadcast_to(ids[:, None], (ids.shape[0], tq)))
    prev = lambda i: jnp.maximum(i - 1, 0)
    tok = lambda w: pl.BlockSpec((tq, w), lambda i, c: (prev(i), 0))
    const = lambda shape: pl.BlockSpec(shape, lambda i, c: (0,) * len(shape))
    return pl.pallas_call(
        functools.partial(_peer_body, tq, nh),
        grid=(nt + 1, n_chunks),
        in_specs=[tok(D_MODEL), tok(D_MODEL),
                  pl.BlockSpec((tq, 2 * N_KEYS), lambda i, c: (jnp.minimum(i, nt - 1), c)),
                  const((2 * PEER_HEADS, N_KEYS, N_KEYS)), const(ids.shape),
                  pl.BlockSpec((None, D_MODEL, 2 * ec), lambda i, c: (c, 0, 0)),
                  pl.BlockSpec((None, 2 * ec, D_MODEL), lambda i, c: (c, 0, 0)),
                  const((1, D_MODEL))],
        out_specs=pl.BlockSpec((tq, D_MODEL), lambda i, c: (prev(i), 0)),
        out_shape=jax.ShapeDtypeStruct((N, D_MODEL), F32),
        scratch_shapes=[pltpu.VMEM((tq * G_PITCH, N_KEYS), jnp.int32),
                        pltpu.VMEM((2, 3, nsel, tq), F32),
                        pltpu.VMEM((tq, nsel), jnp.int32),
                        pltpu.VMEM((tq, nsel), F32),
                        pltpu.VMEM((tq // LANES, nsel, LANES), jnp.int32)],
        compiler_params=_params(("arbitrary", "arbitrary")),
        name="peer",
    )(x1, xn, qp, keys_bf, ids, u_blk, v_blk, gf.reshape(1, D_MODEL))


def _encoder(x, w):
    B, S, _ = x.shape
    f_in, qkv1, qkv4, qkv16 = _proj(x, w["norm1_g"], w["w_in"])
    f_out = _fourier(f_in, w["w_fourier"])
    a_out = _attention([qkv1, qkv4, qkv16], S)
    N = B * S
    flat = lambda a: a.reshape(N, a.shape[-1])
    x1, xn, qp = _mix(flat(x), flat(f_out), flat(a_out), w["w_out"], w["norm2_g"], w["w_query"])
    y = _peer(x1, xn, qp, w["sub_keys"], w["expert_u"], w["expert_v"], w["final_g"])
    return y.reshape(B, S, D_MODEL)


def kernel(x_prompt, x_sample, norm1_g, w_in, w_fourier, w_out, norm2_g, w_query, sub_keys, expert_u, expert_v, final_g):
    u_blk, v_blk = _peer_tables(expert_u[0], expert_v[0])
    w = {
        "norm1_g": norm1_g[0], "w_in": w_in[0].astype(BF16), "w_fourier": w_fourier[0],
        "w_out": w_out[0].astype(BF16), "norm2_g": norm2_g[0], "w_query": w_query[0].astype(BF16),
        "sub_keys": sub_keys[0].reshape(2 * PEER_HEADS, N_KEYS, N_KEYS).astype(BF16),
        "expert_u": u_blk, "expert_v": v_blk, "final_g": final_g,
    }
    return (_encoder(x_prompt, w), _encoder(x_sample, w))
```

```python
import functools

import numpy as np
import jax
import jax.numpy as jnp
from jax import lax
from jax.experimental import pallas as pl
from jax.experimental.pallas import tpu as pltpu

F32 = jnp.float32
BF16 = jnp.bfloat16

D_MODEL = 1024
HEAD_DIM = 64
FOURIER_WIDTH = 512
ATTN_WIDTH = 512
IN_WIDTH = 2048
N_GROUPS = 8
DILATIONS = (1, 4, 16)
HALF_WINDOW = 64
ROPE_THETA = 500000.0
ROPE_DIM = 16
PEER_HEADS = 8
N_KEYS = 128
N_EXPERTS = N_KEYS * N_KEYS
TOPK = 16
EPS = 1e-6
NEG = -1e30

LANES = 128
DFT_INNER = 128
G_PITCH = 72
ID_NONE = 1e9
VMEM_LIMIT = 56 * 2**20

PROJ_TILE = 512
DFT_COL_TILE = 4096
DFT_ROW_TILE = 8
ATTN_QUERY_BLOCK = 128
ATTN_TILE = 2048
ATTN_MERGE_CHUNK = 256
MIX_TILE = 512
PEER_TILE = 512
PEER_NH = 8


def _params(sem, vmem=None):
    return pltpu.CompilerParams(dimension_semantics=sem, vmem_limit_bytes=vmem or VMEM_LIMIT)


def _dot(a, b):
    return jnp.dot(a, b, preferred_element_type=F32)


def _dot_nt(a, b):
    return lax.dot_general(a, b, (((1,), (1,)), ((), ())), preferred_element_type=F32)


def _proj_body(tm, x_ref, g_ref, w_ref, c_ref, sa_ref, sb_ref,
               f_ref, q_ref, k_ref, v_ref, q4_ref, k4_ref, v4_ref, q16_ref, k16_ref, v16_ref, stage_ref, hop_ref):
    x = x_ref[...]
    ms = jnp.mean(x * x, axis=-1, keepdims=True)
    h = (x * lax.rsqrt(ms + EPS) * g_ref[...]).astype(BF16)
    p = _dot(h, w_ref[...])
    f_ref[...] = p[:, :FOURIER_WIDTH].astype(BF16)
    c, sa, sb = c_ref[...], sa_ref[...], sb_ref[...]
    for j in range(ATTN_WIDTH // LANES):
        lo = j * LANES
        qc = p[:, FOURIER_WIDTH + lo:FOURIER_WIDTH + lo + LANES]
        kc = p[:, FOURIER_WIDTH + ATTN_WIDTH + lo:FOURIER_WIDTH + ATTN_WIDTH + lo + LANES]
        vc = p[:, FOURIER_WIDTH + 2 * ATTN_WIDTH + lo:FOURIER_WIDTH + 2 * ATTN_WIDTH + lo + LANES]
        qr = (qc * c + pltpu.roll(qc, LANES - 8, 1) * sa + pltpu.roll(qc, 8, 1) * sb) * (HEAD_DIM ** -0.5)
        kr = kc * c + pltpu.roll(kc, LANES - 8, 1) * sa + pltpu.roll(kc, 8, 1) * sb
        for a, (val, nat) in enumerate(((qr, q_ref), (kr, k_ref), (vc, v_ref))):
            nat[:, lo:lo + LANES] = val.astype(BF16)
            stage_ref[a, j] = val
    _, R1, R2 = DILATIONS
    step = R2 // R1
    for a, (d1_ref, d2_ref) in enumerate(((q4_ref, q16_ref), (k4_ref, k16_ref), (v4_ref, v16_ref))):
        for j in range(ATTN_WIDTH // LANES):
            for r in range(R1):
                piece = stage_ref[a, j, pl.ds(r, tm // R1, stride=R1), :]
                d1_ref[:, (j * R1 + r) * LANES:(j * R1 + r + 1) * LANES] = piece.astype(BF16)
                hop_ref[r] = piece
            for r2 in range(R2):
                piece = hop_ref[r2 % R1, pl.ds(r2 // R1, tm // R2, stride=step), :]
                d2_ref[:, (j * R2 + r2) * LANES:(j * R2 + r2 + 1) * LANES] = piece.astype(BF16)


def _rope_tables(S):
    half = ROPE_DIM // 2
    inv = ROPE_THETA ** (-(jnp.arange(half, dtype=F32) * 2.0) / ROPE_DIM)
    ang = jnp.arange(S, dtype=F32)[:, None] * inv[None, :]
    cos, sin = jnp.cos(ang), jnp.sin(ang)
    one = jnp.ones((S, HEAD_DIM - ROPE_DIM), F32)
    zero = jnp.zeros((S, HEAD_DIM - ROPE_DIM), F32)
    z8 = jnp.zeros((S, half), F32)
    c = jnp.concatenate([cos, cos, one], axis=1)
    sa = jnp.concatenate([-sin, z8, zero], axis=1)
    sb = jnp.concatenate([z8, sin, zero], axis=1)
    rep = LANES // HEAD_DIM
    return jnp.tile(c, (1, rep)), jnp.tile(sa, (1, rep)), jnp.tile(sb, (1, rep))


def _proj(x, g, w_bf):
    B, S, _ = x.shape
    tm = min(PROJ_TILE, S)
    c, sa, sb = _rope_tables(S)
    tab = pl.BlockSpec((tm, LANES), lambda b, i: (i, 0))
    out = pl.BlockSpec((None, tm, ATTN_WIDTH), lambda b, i: (b, i, 0))
    shp = jax.ShapeDtypeStruct((B, S, ATTN_WIDTH), BF16)
    dil_specs, dil_shapes = [], []
    for R in DILATIONS[1:]:
        dil_specs.append(pl.BlockSpec((None, tm // R, R * ATTN_WIDTH), lambda b, i: (b, i, 0)))
        dil_shapes.append(jax.ShapeDtypeStruct((B, S // R, R * ATTN_WIDTH), BF16))
    outs = pl.pallas_call(
        functools.partial(_proj_body, tm),
        grid=(B, S // tm),
        in_specs=[pl.BlockSpec((None, tm, D_MODEL), lambda b, i: (b, i, 0)),
                  pl.BlockSpec((1, D_MODEL), lambda b, i: (0, 0)),
                  pl.BlockSpec((D_MODEL, IN_WIDTH), lambda b, i: (0, 0)),
                  tab, tab, tab],
        out_specs=[out] * 4 + [dil_specs[0]] * 3 + [dil_specs[1]] * 3,
        out_shape=[shp] * 4 + [dil_shapes[0]] * 3 + [dil_shapes[1]] * 3,
        scratch_shapes=[pltpu.VMEM((3, ATTN_WIDTH // LANES, tm, LANES), F32),
                        pltpu.VMEM((DILATIONS[1], tm // DILATIONS[1], LANES), F32)],
        compiler_params=_params(("parallel", "parallel")),
        name="proj",
    )(x, g.reshape(1, D_MODEL), w_bf, c, sa, sb)
    return outs[0], outs[1:4], outs[4:7], outs[7:10]


def _cos_sin(n_rows, n_cols, period):
    i = jnp.arange(n_rows, dtype=jnp.int32)[:, None]
    j = jnp.arange(n_cols, dtype=jnp.int32)[None, :]
    ang = ((i * j) % period).astype(F32) * (2.0 * np.pi / period)
    return jnp.cos(ang), jnp.sin(ang)


def _dft_a_body(n1, d_ref, x_ref, yr_ref, yi_ref):
    y = _dot(d_ref[...], x_ref[...])
    yr_ref[...] = y[:n1].astype(BF16)
    yi_ref[...] = y[n1:].astype(BF16)


def _dft_b_body(ct, scale, yr_ref, yi_ref, tc_ref, ts_ref, d2_ref, c64_ref, s64_ref, wf_ref, o_ref):
    rep = FOURIER_WIDTH // LANES
    xr, xi = [], []
    for cc in range(ct):
        yr = yr_ref[cc].astype(F32)
        yi = yi_ref[cc].astype(F32)
        tc = jnp.concatenate([tc_ref[cc]] * rep, axis=1)
        ts = jnp.concatenate([ts_ref[cc]] * rep, axis=1)
        zr = yr * tc + yi * ts
        zi = yi * tc - yr * ts
        z = jnp.concatenate([zr, zi], axis=0).astype(BF16)
        xx = _dot(d2_ref[...], z)
        xr.append(xx[:DFT_INNER].astype(BF16))
        xi.append(xx[DFT_INNER:].astype(BF16))
    xr, xi = jnp.concatenate(xr, axis=0), jnp.concatenate(xi, axis=0)
    re = (_dot(xr, c64_ref[...]) + _dot(xi, s64_ref[...])) * scale
    y = _dot(re.astype(BF16), wf_ref[...]).astype(BF16)
    for cc in range(ct):
        o_ref[cc] = y[cc * DFT_INNER:(cc + 1) * DFT_INNER]


def _block_diag(m):
    G, n, _ = m.shape
    eye = jnp.eye(G, dtype=m.dtype)
    return (eye[:, None, :, None] * m[:, :, None, :]).reshape(G * n, G * n)


def _fourier(f_in, w_fourier):
    B, S, W = f_in.shape
    n2 = DFT_INNER
    n1 = S // n2
    cols = n2 * W
    tcw = DFT_COL_TILE
    c1, s1 = _cos_sin(n1, n1, n1)
    d1 = jnp.concatenate([c1, -s1], axis=0).astype(BF16)
    xa = f_in.reshape(B, n1, cols)
    blk = pl.BlockSpec((None, n1, tcw), lambda b, j: (b, 0, j))
    yshape = jax.ShapeDtypeStruct((B, n1, cols), BF16)
    yr, yi = pl.pallas_call(
        functools.partial(_dft_a_body, n1),
        grid=(B, cols // tcw),
        in_specs=[pl.BlockSpec((2 * n1, n1), lambda b, j: (0, 0)), blk],
        out_specs=[blk, blk],
        out_shape=[yshape, yshape],
        compiler_params=_params(("parallel", "parallel")),
        name="dft_a",
    )(d1, xa)

    yr = yr.reshape(B, n1, n2, W)
    yi = yi.reshape(B, n1, n2, W)
    twc, tws = _cos_sin(n1, n2, S)
    twc = jnp.broadcast_to(twc[:, :, None], (n1, n2, LANES))
    tws = jnp.broadcast_to(tws[:, :, None], (n1, n2, LANES))
    c2, s2 = _cos_sin(n2, n2, n2)
    d2 = jnp.concatenate([jnp.concatenate([c2, s2], axis=1),
                          jnp.concatenate([-s2, c2], axis=1)], axis=0).astype(BF16)
    cg, sg = _cos_sin(HEAD_DIM, HEAD_DIM, HEAD_DIM)
    c64 = _block_diag(jnp.broadcast_to(cg, (N_GROUPS, HEAD_DIM, HEAD_DIM))).astype(BF16)
    s64 = _block_diag(jnp.broadcast_to(sg, (N_GROUPS, HEAD_DIM, HEAD_DIM))).astype(BF16)
    wf = _block_diag(w_fourier).astype(BF16)
    ct = DFT_ROW_TILE
    scale = float(1.0 / np.sqrt(S * HEAD_DIM))
    yblk = pl.BlockSpec((None, ct, n2, W), lambda b, i: (b, i, 0, 0))
    tblk = pl.BlockSpec((ct, n2, LANES), lambda b, i: (i, 0, 0))
    full = lambda r, c: pl.BlockSpec((r, c), lambda b, i: (0, 0))
    out = pl.pallas_call(
        functools.partial(_dft_b_body, ct, scale),
        grid=(B, n1 // ct),
        in_specs=[yblk, yblk, tblk, tblk, full(2 * n2, 2 * n2), full(W, W), full(W, W), full(W, W)],
        out_specs=yblk,
        out_shape=jax.ShapeDtypeStruct((B, n1, n2, W), BF16),
        compiler_params=_params(("parallel", "parallel")),
        name="dft_b",
    )(yr, yi, twc, tws, d2, c64, s64, wf)
    return out.transpose(0, 2, 1, 3).reshape(B, S, W)


def _band_attend(qf, ks, vs, bias, lo_half):
    n = qf.shape[0]
    q2 = jnp.concatenate([jnp.where(lo_half, qf, 0.0), jnp.where(lo_half, 0.0, qf)], axis=0).astype(BF16)
    s = _dot_nt(q2, ks) + jnp.concatenate([bias, bias], axis=0)
    m = jnp.max(s, axis=1, keepdims=True)
    p = jnp.exp(s - m)
    l = jnp.sum(p, axis=1, keepdims=True)
    o = _dot(p.astype(BF16), vs) * (1.0 / l)
    lse = m + jnp.log(l)
    return jnp.where(lo_half, o[:n], o[n:]), jnp.where(lo_half, lse[:n], lse[n:])


def _attn_bias_terms():
    sub, win = ATTN_QUERY_BLOCK, ATTN_QUERY_BLOCK + 2 * HALF_WINDOW
    lane = lax.broadcasted_iota(jnp.int32, (sub, LANES), 1)
    r = lax.broadcasted_iota(jnp.int32, (sub, win), 0)
    c = lax.broadcasted_iota(jnp.int32, (sub, win), 1)
    band = jnp.where((c >= r) & (c <= r + 2 * HALF_WINDOW), 0.0, NEG)
    return lane < HEAD_DIM, band, lax.broadcasted_iota(jnp.int32, (1, win), 1)


def _attn_body(tb, S, *refs):
    ins, (a_ref, og_ref, lg_ref) = refs[:-3], refs[-3:]
    sub, win = ATTN_QUERY_BLOCK, ATTN_QUERY_BLOCK + 2 * HALF_WINDOW
    lo_half, band, key = _attn_bias_terms()
    i = pl.program_id(1)
    for g, R in enumerate(DILATIONS):
        q_ref, kp_ref, kc_ref, kn_ref, vp_ref, vc_ref, vn_ref = ins[7 * g:7 * g + 7]
        tq, L = tb // R, S // R
        n_sub = tq // sub
        for r in range(R):
            sl = slice(r * LANES, (r + 1) * LANES)
            k2 = jnp.concatenate([kp_ref[:, sl], kc_ref[:, sl], kn_ref[:, sl]], axis=0)
            v2 = jnp.concatenate([vp_ref[:, sl], vc_ref[:, sl], vn_ref[:, sl]], axis=0)
            for sb in range(n_sub):
                bias = band
                if sb == 0 or sb == n_sub - 1:
                    kpos = i * tq + sb * sub - HALF_WINDOW + key
                    bias = band + jnp.where((kpos >= 0) & (kpos < L), 0.0, NEG)
                o, lse = _band_attend(q_ref[sb * sub:(sb + 1) * sub, sl].astype(F32),
                                      k2[sb * sub:sb * sub + win], v2[sb * sub:sb * sub + win], bias, lo_half)
                dst = pl.ds(R * sb * sub + r, sub, stride=R)
                og_ref[g, dst, :] = o
                lg_ref[g, dst, :] = lse
    chunk = min(ATTN_MERGE_CHUNK, tb)
    for t in range(0, tb, chunk):
        rows = slice(t, t + chunk)
        l1, l2, l3 = lg_ref[0, rows, :], lg_ref[1, rows, :], lg_ref[2, rows, :]
        mx = jnp.maximum(jnp.maximum(l1, l2), l3)
        e1, e2, e3 = jnp.exp(l1 - mx), jnp.exp(l2 - mx), jnp.exp(l3 - mx)
        a = (e1 * og_ref[0, rows, :] + e2 * og_ref[1, rows, :] + e3 * og_ref[2, rows, :]) * (1.0 / (e1 + e2 + e3))
        a_ref[rows, :] = a.astype(BF16)


def _halo_index(hb, nhalo, offset):
    return lambda b, i, p: (b, jnp.clip(i * hb + offset, 0, nhalo - 1), p)


def _attention(qkv_groups, S):
    B = qkv_groups[0][0].shape[0]
    tb = min(ATTN_TILE, S)
    specs, args = [], []
    for (q, k, v), R in zip(qkv_groups, DILATIONS):
        tq, L, wb = tb // R, S // R, R * LANES
        hb, nhalo = tq // HALF_WINDOW, L // HALF_WINDOW
        ctr = pl.BlockSpec((None, tq, wb), lambda b, i, p: (b, i, p))
        prev = pl.BlockSpec((None, HALF_WINDOW, wb), _halo_index(hb, nhalo, -1))
        nxt = pl.BlockSpec((None, HALF_WINDOW, wb), _halo_index(hb, nhalo, hb))
        specs += [ctr, prev, ctr, nxt, prev, ctr, nxt]
        args += [q, k, k, k, v, v, v]
    return pl.pallas_call(
        functools.partial(_attn_body, tb, S),
        grid=(B, S // tb, ATTN_WIDTH // LANES),
        in_specs=specs,
        out_specs=pl.BlockSpec((None, tb, LANES), lambda b, i, p: (b, i, p)),
        out_shape=jax.ShapeDtypeStruct((B, S, ATTN_WIDTH), BF16),
        scratch_shapes=[pltpu.VMEM((len(DILATIONS), tb, LANES), F32), pltpu.VMEM((len(DILATIONS), tb, LANES), F32)],
        compiler_params=_params(("parallel", "parallel", "parallel")),
        name="attn",
    )(*args)


def _mix_body(x_ref, f_ref, a_ref, wf_ref, wa_ref, g_ref, wq_ref, x1_ref, xn_ref, qp_ref):
    x1 = x_ref[...] + _dot(f_ref[...], wf_ref[...]) + _dot(a_ref[...], wa_ref[...])
    x1_ref[...] = x1
    ms = jnp.mean(x1 * x1, axis=-1, keepdims=True)
    xn = (x1 * lax.rsqrt(ms + EPS) * g_ref[...]).astype(BF16)
    xn_ref[...] = xn
    qp_ref[...] = _dot(xn, wq_ref[...]).astype(BF16)


def _mix(x, f_out, a_out, w_out_bf, g2, wq_bf):
    N = x.shape[0]
    tm = MIX_TILE
    row = lambda w: pl.BlockSpec((tm, w), lambda i: (i, 0))
    full = lambda r, c: pl.BlockSpec((r, c), lambda i: (0, 0))
    qw = wq_bf.shape[1]
    return pl.pallas_call(
        _mix_body,
        grid=(N // tm,),
        in_specs=[row(D_MODEL), row(FOURIER_WIDTH), row(ATTN_WIDTH),
                  full(FOURIER_WIDTH, D_MODEL), full(ATTN_WIDTH, D_MODEL), full(1, D_MODEL), full(D_MODEL, qw)],
        out_specs=[row(D_MODEL), row(D_MODEL), row(qw)],
        out_shape=[jax.ShapeDtypeStruct((N, D_MODEL), F32), jax.ShapeDtypeStruct((N, D_MODEL), BF16),
                   jax.ShapeDtypeStruct((N, qw), BF16)],
        compiler_params=_params(("parallel",)),
        name="mix",
    )(x, f_out, a_out, w_out_bf[:FOURIER_WIDTH], w_out_bf[FOURIER_WIDTH:], g2.reshape(1, D_MODEL), wq_bf)


def _argmax_rows(parts):
    while len(parts) > 1:
        nxt = []
        for k in range(0, len(parts) - 1, 2):
            (va, ia), (vb, ib) = parts[k], parts[k + 1]
            take_b = vb > va
            nxt.append((jnp.where(take_b, vb, va), jnp.where(take_b, ib, ia)))
        if len(parts) % 2:
            nxt.append(parts[-1])
        parts = nxt
    v, i = parts[0]
    m = jnp.max(v, axis=0, keepdims=True)
    return m, jnp.min(jnp.where(v == m, i, ID_NONE), axis=0, keepdims=True)


def _extract_topk_sorted(s, id_parts, depth):
    tq = s.shape[1]
    blocks = [(s[8 * k:8 * k + 8], ids) for k, ids in enumerate(id_parts)]
    groups = [blocks[g:g + depth] for g in range(0, len(blocks), depth)]
    for col in groups:
        for done in range(depth - 1):
            for j in range(depth - 1 - done):
                (va, ia), (vb, ib) = col[j], col[j + 1]
                swap = vb > va
                col[j] = (jnp.where(swap, vb, va), jnp.where(swap, ib, ia))
                col[j + 1] = (jnp.where(swap, va, vb), jnp.where(swap, ia, ib))
    rank = lax.broadcasted_iota(jnp.int32, (TOPK, tq), 0)
    vals = jnp.zeros((TOPK, tq), F32)
    idxs = jnp.zeros((TOPK, tq), F32)
    for kk in range(TOPK):
        m, first = _argmax_rows([col[0] for col in groups])
        for col in groups:
            won = col[0][1] == first
            for d in range(depth - 1):
                col[d] = (jnp.where(won, col[d + 1][0], col[d][0]), jnp.where(won, col[d + 1][1], col[d][1]))
            col[depth - 1] = (jnp.where(won, -jnp.inf, col[depth - 1][0]), col[depth - 1][1])
        vals = jnp.where(rank == kk, m, vals)
        idxs = jnp.where(rank == kk, first, idxs)
    return vals, idxs


def _select_rows(sel, table):
    out = jnp.zeros_like(sel)
    for a in range(TOPK):
        out = jnp.where(sel == float(a), table[a:a + 1, :], out)
    return out


def _candidate_ids():
    ids = list(range(TOPK))
    for a in range(1, 8):
        ids += [a * TOPK + b if (a + 1) * (b + 1) <= TOPK else ID_NONE for b in range(8)]
    ids += [a * TOPK for a in range(8, TOPK)]
    return np.asarray(ids, np.float32)


def _topk_head(q0, q1, k0, k1, key_ids, cand_ids, cand_id_parts):
    (v1, x1), (v2, x2) = [_extract_topk_sorted(_dot_nt(k, q), key_ids, 4) for k, q in ((k0, q0), (k1, q1))]
    cand = jnp.concatenate([v1[0:1, :] + v2] + [v1[a:a + 1, :] + v2[0:8, :] for a in range(1, 8)]
                           + [v1[8:TOPK, :] + v2[0:1, :]], axis=0)
    cand = jnp.where(cand_ids < ID_NONE, cand, -jnp.inf)
    top_s, flat = _extract_topk_sorted(cand, cand_id_parts, 5)
    a_sel = jnp.floor(flat * (1.0 / TOPK))
    b_sel = flat - a_sel * TOPK
    e = jnp.exp(top_s - top_s[0:1, :])
    return _select_rows(a_sel, x1), _select_rows(b_sel, x2), e * (1.0 / jnp.sum(e, axis=0, keepdims=True))


def _pack_bf16_pair(a, b):
    abits = lax.bitcast_convert_type(a.astype(BF16).astype(F32), jnp.uint32)
    bbits = lax.bitcast_convert_type(b.astype(BF16).astype(F32), jnp.uint32)
    return lax.bitcast_convert_type(abits | (bbits >> 16), jnp.int32)


def _unpack_bf16_pair(word):
    bits = lax.bitcast_convert_type(word, jnp.uint32)
    hi = lax.bitcast_convert_type(bits & jnp.uint32(0xFFFF0000), F32)
    lo = lax.bitcast_convert_type(bits << 16, F32)
    return hi, lo

def _peer_body(tq, nh, x1_ref, xn_ref, qp_ref, keys_ref, ids_ref, ul_ref, uh_ref, vl_ref, vh_ref, gf_ref,
               o_ref, gate_ref, sel_ref, i1_ref, g_ref, i2t_ref):
    i = pl.program_id(0)
    c = pl.program_id(1)
    half_keys = N_KEYS // 2
    slot = lax.rem(i, 2)

    @pl.when(jnp.logical_and(c == 0, i == 0))
    def _():
        gate_ref[...] = jnp.zeros_like(gate_ref)

    @pl.when(jnp.logical_and(c == 0, i > 0))
    def _():
        for grp in range(tq // LANES):
            cols = slice(grp * LANES, (grp + 1) * LANES)
            i1_ref[cols, :] = sel_ref[1 - slot, 0, :, cols].T.astype(jnp.int32)
            i2t_ref[grp] = sel_ref[1 - slot, 1, :, cols].astype(jnp.int32)
            g_ref[cols, :] = sel_ref[1 - slot, 2, :, cols].T
        row = lax.broadcasted_iota(jnp.int32, (N_KEYS, N_KEYS), 0)
        lane = lax.broadcasted_iota(jnp.int32, (N_KEYS, N_KEYS), 1)

        def build(grp, carry):
            base = pl.multiple_of(grp * LANES, LANES)
            i1g, gg = i1_ref[pl.ds(base, LANES), :], g_ref[pl.ds(base, LANES), :]
            i2t = i2t_ref[grp]
            for r in range(LANES):
                p1 = jnp.where(i1g[r:r + 1, :] == row, gg[r:r + 1, :], 0.0).astype(BF16)
                p2 = jnp.where(i2t[:, r:r + 1] == lane, 1.0, 0.0).astype(BF16)
                gt = _dot(p1, p2)
                word = _pack_bf16_pair(gt[:half_keys], gt[half_keys:])
                gate_ref[pl.ds(pl.multiple_of((base + r) * G_PITCH, 8), half_keys), :] = word
            return carry

        lax.fori_loop(0, tq // LANES, build, 0)

    @pl.when(c == 0)
    def _():
        o_ref[...] = jnp.zeros_like(o_ref)

    sub = lax.broadcasted_iota(jnp.int32, (8, tq), 0).astype(F32)
    key_ids = [sub + float(r) for r in range(0, N_KEYS, 8)]
    cand_ids = ids_ref[...]
    cand_id_parts = [cand_ids[r:r + 8] for r in range(0, cand_ids.shape[0], 8)]
    sel = _topk_head(qp_ref[:, :N_KEYS], qp_ref[:, N_KEYS:], keys_ref[2 * c], keys_ref[2 * c + 1],
                     key_ids, cand_ids, cand_id_parts)
    rows = pl.ds(pl.multiple_of(c * TOPK, TOPK), TOPK)
    for k, val in enumerate(sel):
        sel_ref[slot, k, rows, :] = val

    xn = xn_ref[...]
    contrib = []
    for half, (u_ref, v_ref) in enumerate(((ul_ref, vl_ref), (uh_ref, vh_ref))):
        w = []
        for ii in range(0, nh, 2):
            a = _dot(xn, u_ref[:, ii * N_KEYS:(ii + 2) * N_KEYS])
            act = 0.5 * a * (1.0 + lax.erf(a * float(np.sqrt(0.5))))
            for k in range(2):
                g = _unpack_bf16_pair(gate_ref[pl.ds(c * nh + ii + k, tq, stride=G_PITCH), :])[half]
                w.append((act[:, k * N_KEYS:(k + 1) * N_KEYS] * g).astype(BF16))
        contrib.append(_dot(jnp.concatenate(w, axis=1), v_ref[...]))
    o_ref[...] += contrib[0] + contrib[1]

    @pl.when(c == pl.num_programs(1) - 1)
    def _():
        y = x1_ref[...] + o_ref[...]
        ms = jnp.mean(y * y, axis=-1, keepdims=True)
        o_ref[...] = y * lax.rsqrt(ms + EPS) * gf_ref[...]


def _peer_u_blocks(u):
    ec = PEER_NH * N_KEYS
    return u.reshape(N_EXPERTS // ec, ec, D_MODEL).transpose(0, 2, 1).astype(BF16)


def _peer(x1, xn, qp, keys_bf, u_blk, v_bf, gf):
    N = x1.shape[0]
    tq = PEER_TILE
    nh = PEER_NH
    ec = nh * N_KEYS
    n_chunks = (N_KEYS // 2) // nh
    assert n_chunks == PEER_HEADS
    nt = N // tq
    nsel = PEER_HEADS * TOPK
    ids = _candidate_ids()
    ids = jnp.asarray(np.broadcast_to(ids[:, None], (ids.shape[0], tq)))
    prev = lambda i: jnp.maximum(i - 1, 0)
    tok = lambda w: pl.BlockSpec((tq, w), lambda i, c: (prev(i), 0))
    const = lambda shape: pl.BlockSpec(shape, lambda i, c: (0,) * len(shape))
    return pl.pallas_call(
        functools.partial(_peer_body, tq, nh),
        grid=(nt + 1, n_chunks),
        in_specs=[tok(D_MODEL), tok(D_MODEL),
                  pl.BlockSpec((tq, 2 * N_KEYS), lambda i, c: (jnp.minimum(i, nt - 1), c)),
                  const((2 * PEER_HEADS, N_KEYS, N_KEYS)), const(ids.shape),
                  pl.BlockSpec((None, D_MODEL, ec), lambda i, c: (c, 0, 0)),
                  pl.BlockSpec((None, D_MODEL, ec), lambda i, c: (c + n_chunks, 0, 0)),
                  pl.BlockSpec((ec, D_MODEL), lambda i, c: (c, 0)),
                  pl.BlockSpec((ec, D_MODEL), lambda i, c: (c + n_chunks, 0)),
                  const((1, D_MODEL))],
        out_specs=pl.BlockSpec((tq, D_MODEL), lambda i, c: (prev(i), 0)),
        out_shape=jax.ShapeDtypeStruct((N, D_MODEL), F32),
        scratch_shapes=[pltpu.VMEM((tq * G_PITCH, N_KEYS), jnp.int32),
                        pltpu.VMEM((2, 3, nsel, tq), F32),
                        pltpu.VMEM((tq, nsel), jnp.int32),
                        pltpu.VMEM((tq, nsel), F32),
                        pltpu.VMEM((tq // LANES, nsel, LANES), jnp.int32)],
        compiler_params=_params(("arbitrary", "arbitrary")),
        name="peer",
    )(x1, xn, qp, keys_bf, ids, u_blk, u_blk, v_bf, v_bf, gf.reshape(1, D_MODEL))


def _encoder(x, w):
    B, S, _ = x.shape
    f_in, qkv1, qkv4, qkv16 = _proj(x, w["norm1_g"], w["w_in"])
    f_out = _fourier(f_in, w["w_fourier"])
    a_out = _attention([qkv1, qkv4, qkv16], S)
    N = B * S
    flat = lambda a: a.reshape(N, a.shape[-1])
    x1, xn, qp = _mix(flat(x), flat(f_out), flat(a_out), w["w_out"], w["norm2_g"], w["w_query"])
    y = _peer(x1, xn, qp, w["sub_keys"], w["expert_u"], w["expert_v"], w["final_g"])
    return y.reshape(B, S, D_MODEL)


def kernel(x_prompt, x_sample, norm1_g, w_in, w_fourier, w_out, norm2_g, w_query, sub_keys, expert_u, expert_v, final_g):
    w = {
        "norm1_g": norm1_g[0], "w_in": w_in[0].astype(BF16), "w_fourier": w_fourier[0],
        "w_out": w_out[0].astype(BF16), "norm2_g": norm2_g[0], "w_query": w_query[0].astype(BF16),
        "sub_keys": sub_keys[0].reshape(2 * PEER_HEADS, N_KEYS, N_KEYS).astype(BF16),
        "expert_u": _peer_u_blocks(expert_u[0]), "expert_v": expert_v[0].astype(BF16), "final_g": final_g,
    }
    return (_encoder(x_prompt, w), _encoder(x_sample, w))
```

```python
import functools

import numpy as np
import jax
import jax.numpy as jnp
from jax import lax
from jax.experimental import pallas as pl
from jax.experimental.pallas import tpu as pltpu

F32 = jnp.float32
BF16 = jnp.bfloat16

D_MODEL = 1024
HEAD_DIM = 64
FOURIER_WIDTH = 512
ATTN_WIDTH = 512
IN_WIDTH = 2048
N_GROUPS = 8
DILATIONS = (1, 4, 16)
HALF_WINDOW = 64
ROPE_THETA = 500000.0
ROPE_DIM = 16
PEER_HEADS = 8
N_KEYS = 128
N_EXPERTS = N_KEYS * N_KEYS
TOPK = 16
EPS = 1e-6
NEG = -1e30

LANES = 128
DFT_INNER = 128
G_PITCH = 72
ID_NONE = 1e9
VMEM_LIMIT = 56 * 2**20

PROJ_TILE = 512
DFT_COL_TILE = 4096
DFT_ROW_TILE = 8
ATTN_QUERY_BLOCK = 128
ATTN_TILE = 2048
ATTN_MERGE_CHUNK = 256
MIX_TILE = 512
PEER_TILE = 512
PEER_NH = 8


def _params(sem, vmem=None):
    return pltpu.CompilerParams(dimension_semantics=sem, vmem_limit_bytes=vmem or VMEM_LIMIT)


def _dot(a, b):
    return jnp.dot(a, b, preferred_element_type=F32)


def _dot_nt(a, b):
    return lax.dot_general(a, b, (((1,), (1,)), ((), ())), preferred_element_type=F32)


def _proj_body(tm, x_ref, g_ref, w_ref, c_ref, sa_ref, sb_ref,
               f_ref, q_ref, k_ref, v_ref, q4_ref, k4_ref, v4_ref, q16_ref, k16_ref, v16_ref, stage_ref, hop_ref):
    x = x_ref[...]
    ms = jnp.mean(x * x, axis=-1, keepdims=True)
    h = (x * lax.rsqrt(ms + EPS) * g_ref[...]).astype(BF16)
    p = _dot(h, w_ref[...])
    f_ref[...] = p[:, :FOURIER_WIDTH].astype(BF16)
    c, sa, sb = c_ref[...], sa_ref[...], sb_ref[...]
    for j in range(ATTN_WIDTH // LANES):
        lo = j * LANES
        qc = p[:, FOURIER_WIDTH + lo:FOURIER_WIDTH + lo + LANES]
        kc = p[:, FOURIER_WIDTH + ATTN_WIDTH + lo:FOURIER_WIDTH + ATTN_WIDTH + lo + LANES]
        vc = p[:, FOURIER_WIDTH + 2 * ATTN_WIDTH + lo:FOURIER_WIDTH + 2 * ATTN_WIDTH + lo + LANES]
        qr = (qc * c + pltpu.roll(qc, LANES - 8, 1) * sa + pltpu.roll(qc, 8, 1) * sb) * (HEAD_DIM ** -0.5)
        kr = kc * c + pltpu.roll(kc, LANES - 8, 1) * sa + pltpu.roll(kc, 8, 1) * sb
        for a, (val, nat) in enumerate(((qr, q_ref), (kr, k_ref), (vc, v_ref))):
            nat[:, lo:lo + LANES] = val.astype(BF16)
            stage_ref[a, j] = val
    _, R1, R2 = DILATIONS
    step = R2 // R1
    for a, (d1_ref, d2_ref) in enumerate(((q4_ref, q16_ref), (k4_ref, k16_ref), (v4_ref, v16_ref))):
        for j in range(ATTN_WIDTH // LANES):
            for r in range(R1):
                piece = stage_ref[a, j, pl.ds(r, tm // R1, stride=R1), :]
                d1_ref[:, (j * R1 + r) * LANES:(j * R1 + r + 1) * LANES] = piece.astype(BF16)
                hop_ref[r] = piece
            for r2 in range(R2):
                piece = hop_ref[r2 % R1, pl.ds(r2 // R1, tm // R2, stride=step), :]
                d2_ref[:, (j * R2 + r2) * LANES:(j * R2 + r2 + 1) * LANES] = piece.astype(BF16)


def _rope_tables(S):
    half = ROPE_DIM // 2
    inv = ROPE_THETA ** (-(jnp.arange(half, dtype=F32) * 2.0) / ROPE_DIM)
    ang = jnp.arange(S, dtype=F32)[:, None] * inv[None, :]
    cos, sin = jnp.cos(ang), jnp.sin(ang)
    one = jnp.ones((S, HEAD_DIM - ROPE_DIM), F32)
    zero = jnp.zeros((S, HEAD_DIM - ROPE_DIM), F32)
    z8 = jnp.zeros((S, half), F32)
    c = jnp.concatenate([cos, cos, one], axis=1)
    sa = jnp.concatenate([-sin, z8, zero], axis=1)
    sb = jnp.concatenate([z8, sin, zero], axis=1)
    rep = LANES // HEAD_DIM
    return jnp.tile(c, (1, rep)), jnp.tile(sa, (1, rep)), jnp.tile(sb, (1, rep))


def _proj(x, g, w_bf):
    B, S, _ = x.shape
    tm = min(PROJ_TILE, S)
    c, sa, sb = _rope_tables(S)
    tab = pl.BlockSpec((tm, LANES), lambda b, i: (i, 0))
    out = pl.BlockSpec((None, tm, ATTN_WIDTH), lambda b, i: (b, i, 0))
    shp = jax.ShapeDtypeStruct((B, S, ATTN_WIDTH), BF16)
    dil_specs, dil_shapes = [], []
    for R in DILATIONS[1:]:
        dil_specs.append(pl.BlockSpec((None, tm // R, R * ATTN_WIDTH), lambda b, i: (b, i, 0)))
        dil_shapes.append(jax.ShapeDtypeStruct((B, S // R, R * ATTN_WIDTH), BF16))
    outs = pl.pallas_call(
        functools.partial(_proj_body, tm),
        grid=(B, S // tm),
        in_specs=[pl.BlockSpec((None, tm, D_MODEL), lambda b, i: (b, i, 0)),
                  pl.BlockSpec((1, D_MODEL), lambda b, i: (0, 0)),
                  pl.BlockSpec((D_MODEL, IN_WIDTH), lambda b, i: (0, 0)),
                  tab, tab, tab],
        out_specs=[out] * 4 + [dil_specs[0]] * 3 + [dil_specs[1]] * 3,
        out_shape=[shp] * 4 + [dil_shapes[0]] * 3 + [dil_shapes[1]] * 3,
        scratch_shapes=[pltpu.VMEM((3, ATTN_WIDTH // LANES, tm, LANES), F32),
                        pltpu.VMEM((DILATIONS[1], tm // DILATIONS[1], LANES), F32)],
        compiler_params=_params(("parallel", "parallel")),
        name="proj",
    )(x, g.reshape(1, D_MODEL), w_bf, c, sa, sb)
    return outs[0], outs[1:4], outs[4:7], outs[7:10]


def _cos_sin(n_rows, n_cols, period):
    i = jnp.arange(n_rows, dtype=jnp.int32)[:, None]
    j = jnp.arange(n_cols, dtype=jnp.int32)[None, :]
    ang = ((i * j) % period).astype(F32) * (2.0 * np.pi / period)
    return jnp.cos(ang), jnp.sin(ang)


def _dft_a_body(n1, d_ref, x_ref, yr_ref, yi_ref):
    y = _dot(d_ref[...], x_ref[...])
    yr_ref[...] = y[:n1].astype(BF16)
    yi_ref[...] = y[n1:].astype(BF16)


def _dft_b_body(ct, scale, yr_ref, yi_ref, tc_ref, ts_ref, d2_ref, c64_ref, s64_ref, wf_ref, o_ref):
    rep = FOURIER_WIDTH // LANES
    xr, xi = [], []
    for cc in range(ct):
        yr = yr_ref[cc].astype(F32)
        yi = yi_ref[cc].astype(F32)
        tc = jnp.concatenate([tc_ref[cc]] * rep, axis=1)
        ts = jnp.concatenate([ts_ref[cc]] * rep, axis=1)
        zr = yr * tc + yi * ts
        zi = yi * tc - yr * ts
        z = jnp.concatenate([zr, zi], axis=0).astype(BF16)
        xx = _dot(d2_ref[...], z)
        xr.append(xx[:DFT_INNER].astype(BF16))
        xi.append(xx[DFT_INNER:].astype(BF16))
    xr, xi = jnp.concatenate(xr, axis=0), jnp.concatenate(xi, axis=0)
    re = (_dot(xr, c64_ref[...]) + _dot(xi, s64_ref[...])) * scale
    y = _dot(re.astype(BF16), wf_ref[...]).astype(BF16)
    for cc in range(ct):
        o_ref[cc] = y[cc * DFT_INNER:(cc + 1) * DFT_INNER]


def _block_diag(m):
    G, n, _ = m.shape
    eye = jnp.eye(G, dtype=m.dtype)
    return (eye[:, None, :, None] * m[:, :, None, :]).reshape(G * n, G * n)


def _fourier(f_in, w_fourier):
    B, S, W = f_in.shape
    n2 = DFT_INNER
    n1 = S // n2
    cols = n2 * W
    tcw = DFT_COL_TILE
    c1, s1 = _cos_sin(n1, n1, n1)
    d1 = jnp.concatenate([c1, -s1], axis=0).astype(BF16)
    xa = f_in.reshape(B, n1, cols)
    blk = pl.BlockSpec((None, n1, tcw), lambda b, j: (b, 0, j))
    yshape = jax.ShapeDtypeStruct((B, n1, cols), BF16)
    yr, yi = pl.pallas_call(
        functools.partial(_dft_a_body, n1),
        grid=(B, cols // tcw),
        in_specs=[pl.BlockSpec((2 * n1, n1), lambda b, j: (0, 0)), blk],
        out_specs=[blk, blk],
        out_shape=[yshape, yshape],
        compiler_params=_params(("parallel", "parallel")),
        name="dft_a",
    )(d1, xa)

    yr = yr.reshape(B, n1, n2, W)
    yi = yi.reshape(B, n1, n2, W)
    twc, tws = _cos_sin(n1, n2, S)
    twc = jnp.broadcast_to(twc[:, :, None], (n1, n2, LANES))
    tws = jnp.broadcast_to(tws[:, :, None], (n1, n2, LANES))
    c2, s2 = _cos_sin(n2, n2, n2)
    d2 = jnp.concatenate([jnp.concatenate([c2, s2], axis=1),
                          jnp.concatenate([-s2, c2], axis=1)], axis=0).astype(BF16)
    cg, sg = _cos_sin(HEAD_DIM, HEAD_DIM, HEAD_DIM)
    c64 = _block_diag(jnp.broadcast_to(cg, (N_GROUPS, HEAD_DIM, HEAD_DIM))).astype(BF16)
    s64 = _block_diag(jnp.broadcast_to(sg, (N_GROUPS, HEAD_DIM, HEAD_DIM))).astype(BF16)
    wf = _block_diag(w_fourier).astype(BF16)
    ct = DFT_ROW_TILE
    scale = float(1.0 / np.sqrt(S * HEAD_DIM))
    yblk = pl.BlockSpec((None, ct, n2, W), lambda b, i: (b, i, 0, 0))
    tblk = pl.BlockSpec((ct, n2, LANES), lambda b, i: (i, 0, 0))
    full = lambda r, c: pl.BlockSpec((r, c), lambda b, i: (0, 0))
    out = pl.pallas_call(
        functools.partial(_dft_b_body, ct, scale),
        grid=(B, n1 // ct),
        in_specs=[yblk, yblk, tblk, tblk, full(2 * n2, 2 * n2), full(W, W), full(W, W), full(W, W)],
        out_specs=yblk,
        out_shape=jax.ShapeDtypeStruct((B, n1, n2, W), BF16),
        compiler_params=_params(("parallel", "parallel")),
        name="dft_b",
    )(yr, yi, twc, tws, d2, c64, s64, wf)
    return out.transpose(0, 2, 1, 3).reshape(B, S, W)


def _band_attend(qf, ks, vs, bias, lo_half):
    n = qf.shape[0]
    q2 = jnp.concatenate([jnp.where(lo_half, qf, 0.0), jnp.where(lo_half, 0.0, qf)], axis=0).astype(BF16)
    s = _dot_nt(q2, ks) + jnp.concatenate([bias, bias], axis=0)
    m = jnp.max(s, axis=1, keepdims=True)
    p = jnp.exp(s - m)
    l = jnp.sum(p, axis=1, keepdims=True)
    o = _dot(p.astype(BF16), vs) * (1.0 / l)
    lse = m + jnp.log(l)
    return jnp.where(lo_half, o[:n], o[n:]), jnp.where(lo_half, lse[:n], lse[n:])


def _attn_bias_terms():
    sub, win = ATTN_QUERY_BLOCK, ATTN_QUERY_BLOCK + 2 * HALF_WINDOW
    lane = lax.broadcasted_iota(jnp.int32, (sub, LANES), 1)
    r = lax.broadcasted_iota(jnp.int32, (sub, win), 0)
    c = lax.broadcasted_iota(jnp.int32, (sub, win), 1)
    band = jnp.where((c >= r) & (c <= r + 2 * HALF_WINDOW), 0.0, NEG)
    return lane < HEAD_DIM, band, lax.broadcasted_iota(jnp.int32, (1, win), 1)


def _attn_body(tb, S, *refs):
    ins, (a_ref, og_ref, lg_ref) = refs[:-3], refs[-3:]
    sub, win = ATTN_QUERY_BLOCK, ATTN_QUERY_BLOCK + 2 * HALF_WINDOW
    lo_half, band, key = _attn_bias_terms()
    i = pl.program_id(1)
    for g, R in enumerate(DILATIONS):
        q_ref, kp_ref, kc_ref, kn_ref, vp_ref, vc_ref, vn_ref = ins[7 * g:7 * g + 7]
        tq, L = tb // R, S // R
        n_sub = tq // sub
        for r in range(R):
            sl = slice(r * LANES, (r + 1) * LANES)
            k2 = jnp.concatenate([kp_ref[:, sl], kc_ref[:, sl], kn_ref[:, sl]], axis=0)
            v2 = jnp.concatenate([vp_ref[:, sl], vc_ref[:, sl], vn_ref[:, sl]], axis=0)
            for sb in range(n_sub):
                bias = band
                if sb == 0 or sb == n_sub - 1:
                    kpos = i * tq + sb * sub - HALF_WINDOW + key
                    bias = band + jnp.where((kpos >= 0) & (kpos < L), 0.0, NEG)
                o, lse = _band_attend(q_ref[sb * sub:(sb + 1) * sub, sl].astype(F32),
                                      k2[sb * sub:sb * sub + win], v2[sb * sub:sb * sub + win], bias, lo_half)
                dst = pl.ds(R * sb * sub + r, sub, stride=R)
                og_ref[g, dst, :] = o
                lg_ref[g, dst, :] = lse
    chunk = min(ATTN_MERGE_CHUNK, tb)
    for t in range(0, tb, chunk):
        rows = slice(t, t + chunk)
        l1, l2, l3 = lg_ref[0, rows, :], lg_ref[1, rows, :], lg_ref[2, rows, :]
        mx = jnp.maximum(jnp.maximum(l1, l2), l3)
        e1, e2, e3 = jnp.exp(l1 - mx), jnp.exp(l2 - mx), jnp.exp(l3 - mx)
        a = (e1 * og_ref[0, rows, :] + e2 * og_ref[1, rows, :] + e3 * og_ref[2, rows, :]) * (1.0 / (e1 + e2 + e3))
        a_ref[rows, :] = a.astype(BF16)


def _halo_index(hb, nhalo, offset):
    return lambda b, i, p: (b, jnp.clip(i * hb + offset, 0, nhalo - 1), p)


def _attention(qkv_groups, S):
    B = qkv_groups[0][0].shape[0]
    tb = min(ATTN_TILE, S)
    specs, args = [], []
    for (q, k, v), R in zip(qkv_groups, DILATIONS):
        tq, L, wb = tb // R, S // R, R * LANES
        hb, nhalo = tq // HALF_WINDOW, L // HALF_WINDOW
        ctr = pl.BlockSpec((None, tq, wb), lambda b, i, p: (b, i, p))
        prev = pl.BlockSpec((None, HALF_WINDOW, wb), _halo_index(hb, nhalo, -1))
        nxt = pl.BlockSpec((None, HALF_WINDOW, wb), _halo_index(hb, nhalo, hb))
        specs += [ctr, prev, ctr, nxt, prev, ctr, nxt]
        args += [q, k, k, k, v, v, v]
    return pl.pallas_call(
        functools.partial(_attn_body, tb, S),
        grid=(B, S // tb, ATTN_WIDTH // LANES),
        in_specs=specs,
        out_specs=pl.BlockSpec((None, tb, LANES), lambda b, i, p: (b, i, p)),
        out_shape=jax.ShapeDtypeStruct((B, S, ATTN_WIDTH), BF16),
        scratch_shapes=[pltpu.VMEM((len(DILATIONS), tb, LANES), F32), pltpu.VMEM((len(DILATIONS), tb, LANES), F32)],
        compiler_params=_params(("parallel", "parallel", "parallel")),
        name="attn",
    )(*args)


def _mix_body(x_ref, f_ref, a_ref, wf_ref, wa_ref, g_ref, wq_ref, x1_ref, xn_ref, qp_ref):
    x1 = x_ref[...] + _dot(f_ref[...], wf_ref[...]) + _dot(a_ref[...], wa_ref[...])
    x1_ref[...] = x1
    ms = jnp.mean(x1 * x1, axis=-1, keepdims=True)
    xn = (x1 * lax.rsqrt(ms + EPS) * g_ref[...]).astype(BF16)
    xn_ref[...] = xn
    piece = 4 * LANES
    for lo in range(0, qp_ref.shape[1], piece):
        qp_ref[:, lo:lo + piece] = _dot(xn, wq_ref[:, lo:lo + piece]).astype(BF16)


def _mix(x, f_out, a_out, w_out_bf, g2, wq_bf):
    N = x.shape[0]
    tm = MIX_TILE
    row = lambda w: pl.BlockSpec((tm, w), lambda i: (i, 0))
    full = lambda r, c: pl.BlockSpec((r, c), lambda i: (0, 0))
    qw = wq_bf.shape[1]
    return pl.pallas_call(
        _mix_body,
        grid=(N // tm,),
        in_specs=[row(D_MODEL), row(FOURIER_WIDTH), row(ATTN_WIDTH),
                  full(FOURIER_WIDTH, D_MODEL), full(ATTN_WIDTH, D_MODEL), full(1, D_MODEL), full(D_MODEL, qw)],
        out_specs=[row(D_MODEL), row(D_MODEL), row(qw)],
        out_shape=[jax.ShapeDtypeStruct((N, D_MODEL), F32), jax.ShapeDtypeStruct((N, D_MODEL), BF16),
                   jax.ShapeDtypeStruct((N, qw), BF16)],
        compiler_params=_params(("parallel",)),
        name="mix",
    )(x, f_out, a_out, w_out_bf[:FOURIER_WIDTH], w_out_bf[FOURIER_WIDTH:], g2.reshape(1, D_MODEL), wq_bf)


def _argmax_rows(parts):
    while len(parts) > 1:
        nxt = []
        for k in range(0, len(parts) - 1, 2):
            (va, ia), (vb, ib) = parts[k], parts[k + 1]
            take_b = vb > va
            nxt.append((jnp.where(take_b, vb, va), jnp.where(take_b, ib, ia)))
        if len(parts) % 2:
            nxt.append(parts[-1])
        parts = nxt
    v, i = parts[0]
    m = jnp.max(v, axis=0, keepdims=True)
    return m, jnp.min(jnp.where(v == m, i, ID_NONE), axis=0, keepdims=True)


def _extract_topk_sorted(s, id_parts, depth):
    tq = s.shape[1]
    blocks = [(s[8 * k:8 * k + 8], ids) for k, ids in enumerate(id_parts)]
    groups = [blocks[g:g + depth] for g in range(0, len(blocks), depth)]
    for col in groups:
        for done in range(depth - 1):
            for j in range(depth - 1 - done):
                (va, ia), (vb, ib) = col[j], col[j + 1]
                swap = vb > va
                col[j] = (jnp.where(swap, vb, va), jnp.where(swap, ib, ia))
                col[j + 1] = (jnp.where(swap, va, vb), jnp.where(swap, ia, ib))
    rank = lax.broadcasted_iota(jnp.int32, (TOPK, tq), 0)
    vals = jnp.zeros((TOPK, tq), F32)
    idxs = jnp.zeros((TOPK, tq), F32)
    for kk in range(TOPK):
        m, first = _argmax_rows([col[0] for col in groups])
        for col in groups:
            won = col[0][1] == first
            for d in range(depth - 1):
                col[d] = (jnp.where(won, col[d + 1][0], col[d][0]), jnp.where(won, col[d + 1][1], col[d][1]))
            col[depth - 1] = (jnp.where(won, -jnp.inf, col[depth - 1][0]), col[depth - 1][1])
        vals = jnp.where(rank == kk, m, vals)
        idxs = jnp.where(rank == kk, first, idxs)
    return vals, idxs


def _select_rows(sel, table):
    out = jnp.zeros_like(sel)
    for a in range(TOPK):
        out = jnp.where(sel == float(a), table[a:a + 1, :], out)
    return out


def _candidate_ids():
    ids = list(range(TOPK))
    for a in range(1, 8):
        ids += [a * TOPK + b if (a + 1) * (b + 1) <= TOPK else ID_NONE for b in range(8)]
    ids += [a * TOPK for a in range(8, TOPK)]
    return np.asarray(ids, np.float32)


def _topk_head(q0, q1, k0, k1, key_ids, cand_ids, cand_id_parts):
    (v1, x1), (v2, x2) = [_extract_topk_sorted(_dot_nt(k, q), key_ids, 4) for k, q in ((k0, q0), (k1, q1))]
    cand = jnp.concatenate([v1[0:1, :] + v2] + [v1[a:a + 1, :] + v2[0:8, :] for a in range(1, 8)]
                           + [v1[8:TOPK, :] + v2[0:1, :]], axis=0)
    cand = jnp.where(cand_ids < ID_NONE, cand, -jnp.inf)
    top_s, flat = _extract_topk_sorted(cand, cand_id_parts, 5)
    a_sel = jnp.floor(flat * (1.0 / TOPK))
    b_sel = flat - a_sel * TOPK
    e = jnp.exp(top_s - top_s[0:1, :])
    return _select_rows(a_sel, x1), _select_rows(b_sel, x2), e * (1.0 / jnp.sum(e, axis=0, keepdims=True))


def _pack_bf16_pair(a, b):
    abits = lax.bitcast_convert_type(a.astype(BF16).astype(F32), jnp.uint32)
    bbits = lax.bitcast_convert_type(b.astype(BF16).astype(F32), jnp.uint32)
    return lax.bitcast_convert_type(abits | (bbits >> 16), jnp.int32)


def _unpack_bf16_pair(word):
    bits = lax.bitcast_convert_type(word, jnp.uint32)
    hi = lax.bitcast_convert_type(bits & jnp.uint32(0xFFFF0000), F32)
    lo = lax.bitcast_convert_type(bits << 16, F32)
    return hi, lo

def _peer_body(tq, nh, x1_ref, xn_ref, qp_ref, keys_ref, ids_ref, ul_ref, uh_ref, vl_ref, vh_ref, gf_ref,
               o_ref, gate_ref, sel_ref, i1_ref, g_ref, i2t_ref):
    i = pl.program_id(0)
    c = pl.program_id(1)
    half_keys = N_KEYS // 2
    slot = lax.rem(i, 2)

    @pl.when(jnp.logical_and(c == 0, i == 0))
    def _():
        gate_ref[...] = jnp.zeros_like(gate_ref)

    @pl.when(jnp.logical_and(c == 0, i > 0))
    def _():
        for grp in range(tq // LANES):
            cols = slice(grp * LANES, (grp + 1) * LANES)
            i1_ref[cols, :] = sel_ref[1 - slot, 0, :, cols].T.astype(jnp.int32)
            i2t_ref[grp] = sel_ref[1 - slot, 1, :, cols].astype(jnp.int32)
            g_ref[cols, :] = sel_ref[1 - slot, 2, :, cols].T
        row = lax.broadcasted_iota(jnp.int32, (N_KEYS, N_KEYS), 0)
        lane = lax.broadcasted_iota(jnp.int32, (N_KEYS, N_KEYS), 1)

        def build(grp, carry):
            base = pl.multiple_of(grp * LANES, LANES)
            i1g, gg = i1_ref[pl.ds(base, LANES), :], g_ref[pl.ds(base, LANES), :]
            i2t = i2t_ref[grp]
            for r in range(LANES):
                p1 = jnp.where(i1g[r:r + 1, :] == row, gg[r:r + 1, :], 0.0).astype(BF16)
                p2 = jnp.where(i2t[:, r:r + 1] == lane, 1.0, 0.0).astype(BF16)
                gt = _dot(p1, p2)
                word = _pack_bf16_pair(gt[:half_keys], gt[half_keys:])
                gate_ref[pl.ds(pl.multiple_of((base + r) * G_PITCH, 8), half_keys), :] = word
            return carry

        lax.fori_loop(0, tq // LANES, build, 0)

    @pl.when(c == 0)
    def _():
        o_ref[...] = jnp.zeros_like(o_ref)

    sub = lax.broadcasted_iota(jnp.int32, (8, tq), 0).astype(F32)
    key_ids = [sub + float(r) for r in range(0, N_KEYS, 8)]
    cand_ids = ids_ref[...]
    cand_id_parts = [cand_ids[r:r + 8] for r in range(0, cand_ids.shape[0], 8)]
    sel = _topk_head(qp_ref[:, :N_KEYS], qp_ref[:, N_KEYS:], keys_ref[2 * c], keys_ref[2 * c + 1],
                     key_ids, cand_ids, cand_id_parts)
    rows = pl.ds(pl.multiple_of(c * TOPK, TOPK), TOPK)
    for k, val in enumerate(sel):
        sel_ref[slot, k, rows, :] = val

    xn = xn_ref[...]
    contrib = []
    for half, (u_ref, v_ref) in enumerate(((ul_ref, vl_ref), (uh_ref, vh_ref))):
        w = []
        for ii in range(0, nh, 2):
            a = _dot(xn, u_ref[:, ii * N_KEYS:(ii + 2) * N_KEYS])
            act = 0.5 * a * (1.0 + lax.erf(a * float(np.sqrt(0.5))))
            for k in range(2):
                g = _unpack_bf16_pair(gate_ref[pl.ds(c * nh + ii + k, tq, stride=G_PITCH), :])[half]
                w.append((act[:, k * N_KEYS:(k + 1) * N_KEYS] * g).astype(BF16))
        contrib.append(_dot(jnp.concatenate(w, axis=1), v_ref[...]))
    o_ref[...] += contrib[0] + contrib[1]

    @pl.when(c == pl.num_programs(1) - 1)
    def _():
        y = x1_ref[...] + o_ref[...]
        ms = jnp.mean(y * y, axis=-1, keepdims=True)
        o_ref[...] = y * lax.rsqrt(ms + EPS) * gf_ref[...]


def _peer_u_blocks(u):
    ec = PEER_NH * N_KEYS
    return u.reshape(N_EXPERTS // ec, ec, D_MODEL).transpose(0, 2, 1).astype(BF16)


def _peer(x1, xn, qp, keys_bf, u_blk, v_bf, gf):
    N = x1.shape[0]
    tq = PEER_TILE
    nh = PEER_NH
    ec = nh * N_KEYS
    n_chunks = (N_KEYS // 2) // nh
    assert n_chunks == PEER_HEADS
    nt = N // tq
    nsel = PEER_HEADS * TOPK
    ids = _candidate_ids()
    ids = jnp.asarray(np.broadcast_to(ids[:, None], (ids.shape[0], tq)))
    prev = lambda i: jnp.maximum(i - 1, 0)
    tok = lambda w: pl.BlockSpec((tq, w), lambda i, c: (prev(i), 0))
    const = lambda shape: pl.BlockSpec(shape, lambda i, c: (0,) * len(shape))
    return pl.pallas_call(
        functools.partial(_peer_body, tq, nh),
        grid=(nt + 1, n_chunks),
        in_specs=[tok(D_MODEL), tok(D_MODEL),
                  pl.BlockSpec((tq, 2 * N_KEYS), lambda i, c: (jnp.minimum(i, nt - 1), c)),
                  const((2 * PEER_HEADS, N_KEYS, N_KEYS)), const(ids.shape),
                  pl.BlockSpec((None, D_MODEL, ec), lambda i, c: (c, 0, 0)),
                  pl.BlockSpec((None, D_MODEL, ec), lambda i, c: (c + n_chunks, 0, 0)),
                  pl.BlockSpec((ec, D_MODEL), lambda i, c: (c, 0)),
                  pl.BlockSpec((ec, D_MODEL), lambda i, c: (c + n_chunks, 0)),
                  const((1, D_MODEL))],
        out_specs=pl.BlockSpec((tq, D_MODEL), lambda i, c: (prev(i), 0)),
        out_shape=jax.ShapeDtypeStruct((N, D_MODEL), F32),
        scratch_shapes=[pltpu.VMEM((tq * G_PITCH, N_KEYS), jnp.int32),
                        pltpu.VMEM((2, 3, nsel, tq), F32),
                        pltpu.VMEM((tq, nsel), jnp.int32),
                        pltpu.VMEM((tq, nsel), F32),
                        pltpu.VMEM((tq // LANES, nsel, LANES), jnp.int32)],
        compiler_params=_params(("arbitrary", "arbitrary")),
        name="peer",
    )(x1, xn, qp, keys_bf, ids, u_blk, u_blk, v_bf, v_bf, gf.reshape(1, D_MODEL))


def _encoder(x, w):
    B, S, _ = x.shape
    f_in, qkv1, qkv4, qkv16 = _proj(x, w["norm1_g"], w["w_in"])
    f_out = _fourier(f_in, w["w_fourier"])
    a_out = _attention([qkv1, qkv4, qkv16], S)
    N = B * S
    flat = lambda a: a.reshape(N, a.shape[-1])
    x1, xn, qp = _mix(flat(x), flat(f_out), flat(a_out), w["w_out"], w["norm2_g"], w["w_query"])
    y = _peer(x1, xn, qp, w["sub_keys"], w["expert_u"], w["expert_v"], w["final_g"])
    return y.reshape(B, S, D_MODEL)


def kernel(x_prompt, x_sample, norm1_g, w_in, w_fourier, w_out, norm2_g, w_query, sub_keys, expert_u, expert_v, final_g):
    w = {
        "norm1_g": norm1_g[0], "w_in": w_in[0].astype(BF16), "w_fourier": w_fourier[0],
        "w_out": w_out[0].astype(BF16), "norm2_g": norm2_g[0], "w_query": w_query[0].astype(BF16),
        "sub_keys": sub_keys[0].reshape(2 * PEER_HEADS, N_KEYS, N_KEYS).astype(BF16),
        "expert_u": _peer_u_blocks(expert_u[0]), "expert_v": expert_v[0].astype(BF16), "final_g": final_g,
    }
    return (_encoder(x_prompt, w), _encoder(x_sample, w))
```
